```python
import math
import jax
import jax.numpy as jnp
from jax import lax
import numpy as np

D_MODEL = 2048
BATCH = 8
SEQ = 2048
DEPTH = 2

GRID_W = 64
CTX_LEN = 256
MIX_W = D_MODEL // 2
N_BRANCH = 3
A_QK_DIM = 64
A_V_DIM = 2 * A_QK_DIM
A_HEADS = MIX_W // A_V_DIM
ROPE_BASE = 10000.0
Q_BLOCK = 128
SG_CHUNK = 128
SG_GROUPS = 8
CONV_K = 3
IN_COLS = 8 * MIX_W + N_BRANCH * D_MODEL
IN_SPLIT = (MIX_W, 2 * MIX_W, 3 * MIX_W, 4 * MIX_W, 5 * MIX_W, 6 * MIX_W, 7 * MIX_W, 8 * MIX_W)
N_EXPERTS = 64
TOP_K = 8
N_GROUPS = 8
TOPK_GROUPS = 4
EXPERT_F = D_MODEL // 4
SHARED_F = D_MODEL // 4
ROUTED_SCALE = 2.5
NORM_EPS = 1e-6

kernel_name = 'hybrid_diffusion_block'


def rmsnorm(x, g):
    xf = x.astype(jnp.float32)
    y = xf * lax.rsqrt(jnp.mean(xf * xf, axis=-1, keepdims=True) + NORM_EPS)
    return (y * g.astype(jnp.float32)).astype(x.dtype)


def layernorm(x, g, b):
    xf = x.astype(jnp.float32)
    mu = jnp.mean(xf, axis=-1, keepdims=True)
    d = xf - mu
    y = d * lax.rsqrt(jnp.mean(d * d, axis=-1, keepdims=True) + NORM_EPS)
    return (y * g.astype(jnp.float32) + b.astype(jnp.float32)).astype(x.dtype)


def axial_rope_tables(rows, dtype):
    n_freq = A_QK_DIM // 4
    inv = ROPE_BASE ** (-jnp.arange(n_freq, dtype=jnp.float32) / n_freq)
    row = jnp.repeat(jnp.arange(rows, dtype=jnp.float32), GRID_W)
    col = jnp.tile(jnp.arange(GRID_W, dtype=jnp.float32), rows)
    ang = jnp.stack([row[:, None] * inv, col[:, None] * inv], axis=1)
    return jnp.cos(ang).astype(dtype), jnp.sin(ang).astype(dtype)


def apply_axial_rope(t, cos, sin):
    n_freq = A_QK_DIM // 4
    ts = t.reshape(t.shape[:-1] + (2, 2, n_freq))
    t1, t2 = ts[..., 0, :], ts[..., 1, :]
    cs = cos[None, :, None, None]
    sn = sin[None, :, None, None]
    out = jnp.stack([t1 * cs - t2 * sn, t2 * cs + t1 * sn], axis=-2)
    return out.reshape(t.shape)


def heads_qk(t):
    return t.reshape(t.shape[:-1] + (A_HEADS, 2, A_QK_DIM))


def heads_v(t):
    return t.reshape(t.shape[:-1] + (A_HEADS, A_V_DIM))


def diff_attn_block(qb, k, v, lam):
    s = jnp.einsum('bqhid,bkhid->bhiqk', qb, k).astype(jnp.float32) * (A_QK_DIM ** -0.5)
    p = jax.nn.softmax(s, axis=-1)
    p = p[:, :, 0] - lam * p[:, :, 1]
    return jnp.einsum('bhqk,bkhe->bqhe', p.astype(v.dtype), v)


def diff_attn_post(o, g_sub, lam_init):
    o = rmsnorm(o, g_sub) * (1.0 - lam_init)
    return o.reshape(o.shape[:2] + (MIX_W,))


def spatial_gating(u, v, ln_g, ln_b, w_s, b_s):
    bsz, t_len, w = u.shape
    u = jax.nn.gelu(u)
    v = layernorm(jax.nn.gelu(v), ln_g, ln_b)
    vc = v.reshape(bsz, t_len // SG_CHUNK, SG_CHUNK, SG_GROUPS, w // SG_GROUPS)
    mixed = jnp.einsum('gpq,bnqgc->bnpgc', w_s, vc) + b_s.T[None, None, :, :, None]
    return u * mixed.reshape(bsz, t_len, w)


def short_gated_conv(xin, gate_b, gate_c, w):
    z = gate_c * xin
    y = lax.conv_general_dilated(z, w[:, None, :], window_strides=(1,), padding=((1, 1),),
                                 dimension_numbers=('NWC', 'WIO', 'NWC'),
                                 feature_group_count=z.shape[-1])
    return gate_b * y


def branches_merge(parts, attn_o, sg_ln_g, sg_ln_b, sg_w, sg_b, conv_w, w_branch, w_out):
    sg_o = spatial_gating(parts[3], parts[4], sg_ln_g, sg_ln_b, sg_w, sg_b)
    cv_o = short_gated_conv(parts[5], parts[6], parts[7], conv_w)
    br = jnp.stack([attn_o, sg_o, cv_o], axis=-2)
    br = jnp.einsum('btgw,gwd->btgd', br, w_branch)
    gl = parts[8]
    g = jax.nn.sigmoid(gl.reshape(gl.shape[:-1] + (N_BRANCH, D_MODEL)))
    return jnp.sum(g * br, axis=-2) @ w_out


def moe_ffn(t, w_router, router_bias, wg, wu, wd, sg, su, sd):
    n = t.shape[0]
    scores = jax.nn.sigmoid((t @ w_router).astype(jnp.float32))
    choice = scores + router_bias.astype(jnp.float32)
    per_group = N_EXPERTS // N_GROUPS
    grp_score = jnp.sum(lax.top_k(choice.reshape(n, N_GROUPS, per_group), 2)[0], axis=-1)
    _, grp_idx = lax.top_k(grp_score, TOPK_GROUPS)
    grp_mask = jnp.any(grp_idx[..., None] == jnp.arange(N_GROUPS), axis=1)
    choice = jnp.where(jnp.repeat(grp_mask, per_group, axis=1), choice, -jnp.inf)
    _, idx = lax.top_k(choice, TOP_K)
    w = jnp.take_along_axis(scores, idx, axis=1)
    w = w / jnp.sum(w, axis=-1, keepdims=True) * ROUTED_SCALE
    gate = jnp.zeros((n, N_EXPERTS), jnp.float32).at[jnp.arange(n)[:, None], idx].set(w)
    gate = gate.T.astype(t.dtype)
    shared = (jax.nn.silu(t @ sg) * (t @ su)) @ sd

    def expert_step(acc, xs):
        g_w, u_w, d_w, g_e = xs
        hdn = jax.nn.silu(t @ g_w) * (t @ u_w)
        return acc + g_e[:, None] * (hdn @ d_w), None

    out, _ = lax.scan(expert_step, shared, (wg, wu, wd, gate))
    return out


def setup_inputs(seed: int = 0) -> dict:
    key = jax.random.key(seed)
    ks = jax.random.split(key, 26)
    f32 = jnp.float32

    def nrm(k, shape, scale):
        return jax.random.normal(k, shape, f32) * scale

    D = D_MODEL
    E, F, FS = N_EXPERTS, EXPERT_F, SHARED_F
    return {
        'x': nrm(ks[0], (BATCH, SEQ, D), 1.0),
        'c': nrm(ks[1], (BATCH, D), 1.0),
        'ctx': nrm(ks[2], (BATCH, CTX_LEN, D), 1.0),
        'c_ctx': nrm(ks[3], (D,), 1.0),
        'w_ada': nrm(ks[4], (DEPTH, D, 6 * D), 0.5 * D ** -0.5),
        'b_ada': nrm(ks[5], (DEPTH, 6 * D), 0.02),
        'norm_g': 1.0 + nrm(ks[6], (DEPTH, 4, D), 0.02),
        'w_in': nrm(ks[7], (DEPTH, D, IN_COLS), D ** -0.5),
        'diff_lambda': nrm(ks[8], (DEPTH, 4, A_QK_DIM), 0.1),
        'g_subln': 1.0 + nrm(ks[9], (DEPTH, A_V_DIM), 0.02),
        'sg_ln_g': 1.0 + nrm(ks[10], (DEPTH, MIX_W), 0.02),
        'sg_ln_b': nrm(ks[11], (DEPTH, MIX_W), 0.02),
        'sg_w': nrm(ks[12], (DEPTH, SG_GROUPS, SG_CHUNK, SG_CHUNK), SG_CHUNK ** -0.5),
        'sg_b': 1.0 + nrm(ks[13], (DEPTH, SG_GROUPS, SG_CHUNK), 0.02),
        'conv_w': nrm(ks[14], (DEPTH, CONV_K, MIX_W), CONV_K ** -0.5),
        'w_branch': nrm(ks[15], (DEPTH, N_BRANCH, MIX_W, D), MIX_W ** -0.5),
        'w_out': nrm(ks[16], (DEPTH, D, D), D ** -0.5),
        'w_router': nrm(ks[17], (DEPTH, D, E), D ** -0.5),
        'router_bias': nrm(ks[18], (DEPTH, E), 0.01),
        'w_exp_gate': nrm(ks[19], (DEPTH, E, D, F), D ** -0.5),
        'w_exp_up': nrm(ks[20], (DEPTH, E, D, F), D ** -0.5),
        'w_exp_down': nrm(ks[21], (DEPTH, E, F, D), F ** -0.5),
        'w_sh_gate': nrm(ks[22], (DEPTH, D, FS), D ** -0.5),
        'w_sh_up': nrm(ks[23], (DEPTH, D, FS), D ** -0.5),
        'w_sh_down': nrm(ks[24], (DEPTH, FS, D), FS ** -0.5),
    }


def reference(x, c, ctx, c_ctx, w_ada, b_ada, norm_g, w_in, diff_lambda, g_subln,
              sg_ln_g, sg_ln_b, sg_w, sg_b, conv_w, w_branch, w_out,
              w_router, router_bias, w_exp_gate, w_exp_up, w_exp_down,
              w_sh_gate, w_sh_up, w_sh_down):
    bsz, s_len, d = x.shape
    c_len = ctx.shape[1]
    rows = s_len // GRID_W
    cos, sin = axial_rope_tables(rows, x.dtype)
    n_blk = s_len // Q_BLOCK
    xc = ctx
    for l in range(DEPTH):
        last = l == DEPTH - 1
        lam_init = 0.8 - 0.6 * math.exp(-0.3 * l)
        dl = diff_lambda[l].astype(jnp.float32)
        lam = jnp.exp(jnp.sum(dl[0] * dl[1])) - jnp.exp(jnp.sum(dl[2] * dl[3])) + lam_init
        mod = jax.nn.silu(c) @ w_ada[l] + b_ada[l]
        mod_c = jax.nn.silu(c_ctx) @ w_ada[l] + b_ada[l]
        sh1, sc1, gt1, sh2, sc2, gt2 = jnp.split(mod[:, None, :], 6, axis=-1)
        csh1, csc1, cgt1, csh2, csc2, cgt2 = jnp.split(mod_c, 6, axis=-1)
        ng = norm_g[l]
        mix_args = (sg_ln_g[l], sg_ln_b[l], sg_w[l], sg_b[l], conv_w[l], w_branch[l], w_out[l])

        hc = rmsnorm(xc, ng[0]) * (1 + csc1) + csh1
        if last:
            kc, vc = jnp.split(hc @ w_in[l][:, MIX_W:3 * MIX_W], 2, axis=-1)
            kc, vc = heads_qk(kc), heads_v(vc)
        else:
            parts_c = jnp.split(hc @ w_in[l], IN_SPLIT, axis=-1)
            qc, kc, vc = heads_qk(parts_c[0]), heads_qk(parts_c[1]), heads_v(parts_c[2])
            attn_c = diff_attn_post(diff_attn_block(qc, kc, vc, lam), g_subln[l], lam_init)
            mix_c = branches_merge(parts_c, attn_c, *mix_args)
            xc = xc + cgt1 * rmsnorm(mix_c, ng[1])

        h = rmsnorm(x, ng[0]) * (1 + sc1) + sh1
        parts = jnp.split(h @ w_in[l], IN_SPLIT, axis=-1)
        q = apply_axial_rope(heads_qk(parts[0]), cos, sin)
        k = apply_axial_rope(heads_qk(parts[1]), cos, sin)
        k_all = jnp.concatenate([kc, k], axis=1)
        v_all = jnp.concatenate([vc, heads_v(parts[2])], axis=1)
        qb = jnp.moveaxis(q.reshape((bsz, n_blk, Q_BLOCK) + q.shape[2:]), 1, 0)
        o = lax.map(lambda qq: diff_attn_block(qq, k_all, v_all, lam), qb)
        o = jnp.moveaxis(o, 0, 1).reshape(bsz, s_len, A_HEADS, A_V_DIM)
        attn = diff_attn_post(o, g_subln[l], lam_init)
        mix = branches_merge(parts, attn, *mix_args)
        x = x + gt1 * rmsnorm(mix, ng[1])

        h2 = (rmsnorm(x, ng[2]) * (1 + sc2) + sh2).reshape(bsz * s_len, d)
        moe_args = (w_router[l], router_bias[l], w_exp_gate[l], w_exp_up[l], w_exp_down[l],
                    w_sh_gate[l], w_sh_up[l], w_sh_down[l])
        if last:
            y = moe_ffn(h2, *moe_args)
        else:
            h2c = (rmsnorm(xc, ng[2]) * (1 + csc2) + csh2).reshape(bsz * c_len, d)
            y_all = moe_ffn(jnp.concatenate([h2, h2c], axis=0), *moe_args)
            y = y_all[:bsz * s_len]
            xc = xc + cgt2 * rmsnorm(y_all[bsz * s_len:].reshape(bsz, c_len, d), ng[3])
        x = x + gt2 * rmsnorm(y.reshape(bsz, s_len, d), ng[3])
    return x
```

```python
import functools
import math

import jax
import jax.numpy as jnp
from jax import lax
from jax.experimental import pallas as pl
from jax.experimental.pallas import tpu as pltpu

GRID_W = 64
QK_DIM = 64
HEAD_W = 2 * QK_DIM
ROPE_BASE = 10000.0
SG_CHUNK = 128
SG_GROUPS = 8
TOP_K = 8
N_GROUPS = 8
TOPK_GROUPS = 4
ROUTED_SCALE = 2.5
NORM_EPS = 1e-6
N_MOD = 6

LANES = 128
VMEM_LIMIT = 56 * 1024 * 1024
SEQ_TILE = 256
EXPERT_TILE = 256
MOD_ROWS = 16

F32 = jnp.float32
BF16 = jnp.bfloat16
U32 = jnp.uint32
HIGHEST = lax.Precision.HIGHEST


def _cparams(*sem):
    return pltpu.CompilerParams(dimension_semantics=sem, vmem_limit_bytes=VMEM_LIMIT)


def _pick_tile(n, cands):
    for c in cands:
        if n % c == 0:
            return c
    raise ValueError(f"no tile for {n} in {cands}")


def _sigmoid(x):
    return 1.0 / (1.0 + jnp.exp(-x))


def _silu(x):
    return x * _sigmoid(x)


def _gelu_tanh(x):
    c = math.sqrt(2.0 / math.pi)
    return x * (0.5 * (1.0 + jnp.tanh(c * (x + 0.044715 * (x * x * x)))))


def _rms(x, g):
    return x * lax.rsqrt(jnp.mean(x * x, axis=-1, keepdims=True) + NORM_EPS) * g


def _pack_bf16_pairs(x):
    w = x.shape[1] // 2
    bits = pltpu.bitcast(x.astype(BF16).astype(F32), U32)
    return (bits[:, :w] >> 16) | bits[:, w:]


def _unpack_lo(w):
    return pltpu.bitcast(w << 16, F32)


def _unpack_hi(w):
    return pltpu.bitcast(w & jnp.uint32(0xFFFF0000), F32)


def _ada_body(c_ref, w_ref, b_ref, o_ref):
    a = _silu(c_ref[...])
    o_ref[0] = jnp.dot(a, w_ref[0], precision=HIGHEST, preferred_element_type=F32) + b_ref[0]


def _ada(cond, w_ada, b_ada):
    depth, d, n = w_ada.shape
    tn = _pick_tile(n, (1024, 512, 256, 128))
    return pl.pallas_call(
        _ada_body,
        grid=(depth, n // tn),
        in_specs=[
            pl.BlockSpec((MOD_ROWS, d), lambda l, j: (0, 0)),
            pl.BlockSpec((1, d, tn), lambda l, j: (l, 0, j)),
            pl.BlockSpec((1, 1, tn), lambda l, j: (l, 0, j)),
        ],
        out_specs=pl.BlockSpec((1, MOD_ROWS, tn), lambda l, j: (l, 0, j)),
        out_shape=jax.ShapeDtypeStruct((depth, MOD_ROWS, n), F32),
        compiler_params=_cparams("arbitrary", "arbitrary"),
        name="ada_mod",
    )(cond, w_ada, b_ada.reshape(depth, 1, n))


def _normmod_body(x_ref, g_ref, sc_ref, sh_ref, o_ref):
    y = _rms(x_ref[...], g_ref[...])
    o_ref[...] = (y * (1.0 + sc_ref[0]) + sh_ref[0]).astype(o_ref.dtype)


def _group_of_tile(i, tiles_per_seq, n_batch):
    return jnp.minimum(i // tiles_per_seq, n_batch)


def _normmod(xa, g, mod3, sc_idx, sh_idx, rows, seq, n_batch):
    d = xa.shape[1]
    tm = _pick_tile(seq, (512, 256))
    tps = seq // tm
    grp = lambda i: _group_of_tile(i, tps, n_batch)
    return pl.pallas_call(
        _normmod_body,
        grid=(rows // tm,),
        in_specs=[
            pl.BlockSpec((tm, d), lambda i: (i, 0)),
            pl.BlockSpec((1, d), lambda i: (0, 0)),
            pl.BlockSpec((1, 1, d), lambda i: (grp(i), 0, sc_idx)),
            pl.BlockSpec((1, 1, d), lambda i: (grp(i), 0, sh_idx)),
        ],
        out_specs=pl.BlockSpec((tm, d), lambda i: (i, 0)),
        out_shape=jax.ShapeDtypeStruct((rows, d), BF16),
        compiler_params=_cparams("parallel"),
        name="norm_mod",
    )(xa, g, mod3, mod3)


def _mm_body(a_ref, w_ref, o_ref):
    o_ref[...] = jnp.dot(a_ref[...], w_ref[...], preferred_element_type=F32).astype(o_ref.dtype)


def _matmul(a, w, rows, row_off, cols, col_off, out_dtype, name):
    k = a.shape[1]
    tm = _pick_tile(math.gcd(rows, row_off) if row_off else rows, (1024, 512, 256))
    tn = _pick_tile(math.gcd(cols, col_off) if col_off else cols, (1024, 512, 256))
    ro, co = row_off // tm, col_off // tn
    return pl.pallas_call(
        _mm_body,
        grid=(rows // tm, cols // tn),
        in_specs=[
            pl.BlockSpec((tm, k), lambda i, j: (i + ro, 0)),
            pl.BlockSpec((k, tn), lambda i, j: (0, j + co)),
        ],
        out_specs=pl.BlockSpec((tm, tn), lambda i, j: (i, j)),
        out_shape=jax.ShapeDtypeStruct((rows, cols), out_dtype),
        compiler_params=_cparams("parallel", "arbitrary"),
        name=name,
    )(a, w)


def _rope_body(y_ref, cos_ref, sin_ref, o_ref):
    cos = cos_ref[...]
    sin = sin_ref[...]
    lane = lax.broadcasted_iota(jnp.int32, cos.shape, 1)
    first_half = (lane % (QK_DIM // 2)) < (QK_DIM // 4)
    for j in range(y_ref.shape[1] // LANES):
        sl = slice(j * LANES, (j + 1) * LANES)
        x = y_ref[:, sl].astype(F32)
        partner = jnp.where(first_half,
                            pltpu.roll(x, LANES - QK_DIM // 4, 1),
                            pltpu.roll(x, QK_DIM // 4, 1))
        o_ref[:, sl] = (x * cos + partner * sin).astype(o_ref.dtype)


def _rope_tables(seq):
    n_freq = QK_DIM // 4
    inv = ROPE_BASE ** (-jnp.arange(n_freq, dtype=F32) / n_freq)
    t = jnp.arange(seq)
    row = (t // GRID_W).astype(F32)
    col = (t % GRID_W).astype(F32)
    ang = jnp.stack([row[:, None] * inv, col[:, None] * inv], axis=1)
    cos = jnp.cos(ang)
    sin = jnp.sin(ang)
    cos64 = jnp.stack([cos, cos], axis=2).reshape(seq, QK_DIM)
    sin64 = jnp.stack([-sin, sin], axis=2).reshape(seq, QK_DIM)
    reps = LANES // QK_DIM
    return jnp.tile(cos64, (1, reps)), jnp.tile(sin64, (1, reps))


def _rope(y, cos, sin, rows, seq, width):
    tm = SEQ_TILE
    tps = seq // tm
    return pl.pallas_call(
        _rope_body,
        grid=(rows // tm,),
        in_specs=[
            pl.BlockSpec((tm, width), lambda i: (i, 0)),
            pl.BlockSpec((tm, LANES), lambda i: (i % tps, 0)),
            pl.BlockSpec((tm, LANES), lambda i: (i % tps, 0)),
        ],
        out_specs=pl.BlockSpec((tm, width), lambda i: (i, 0)),
        out_shape=jax.ShapeDtypeStruct((rows, width), BF16),
        compiler_params=_cparams("parallel"),
        name="rope_qk",
    )(y, cos, sin)


def _attn_body(*refs, two_seg, lam_init):
    if two_seg:
        q_ref, k_ref, v_ref, kc_ref, vc_ref, dl_ref, g_ref, o_ref = refs
    else:
        q_ref, k_ref, v_ref, dl_ref, g_ref, o_ref = refs
    dl = dl_ref[...]
    lam = (jnp.exp(jnp.sum(dl[0:1] * dl[1:2], axis=-1, keepdims=True))
           - jnp.exp(jnp.sum(dl[2:3] * dl[3:4], axis=-1, keepdims=True)) + lam_init)
    q = q_ref[...] * jnp.asarray(QK_DIM ** -0.5, BF16)
    lane = lax.broadcasted_iota(jnp.int32, q.shape, 1)
    zero = jnp.zeros_like(q)
    nt = (((1,), (1,)), ((), ()))

    def probs(qm):
        s = lax.dot_general(qm, k_ref[...], nt, preferred_element_type=F32)
        m = jnp.max(s, axis=-1, keepdims=True)
        if two_seg:
            sc = lax.dot_general(qm, kc_ref[...], nt, preferred_element_type=F32)
            m = jnp.maximum(m, jnp.max(sc, axis=-1, keepdims=True))
        e = jnp.exp(s - m)
        den = jnp.sum(e, axis=-1, keepdims=True)
        if two_seg:
            ec = jnp.exp(sc - m)
            den = den + jnp.sum(ec, axis=-1, keepdims=True)
            return e / den, ec / den
        return e / den, None

    p1, pc1 = probs(jnp.where(lane < QK_DIM, q, zero))
    p2, pc2 = probs(jnp.where(lane >= QK_DIM, q, zero))
    p = (p1 - lam * p2).astype(BF16)
    o = jnp.dot(p, v_ref[...], preferred_element_type=F32)
    if two_seg:
        pc = (pc1 - lam * pc2).astype(BF16)
        o = o + jnp.dot(pc, vc_ref[...], preferred_element_type=F32)
    o_ref[...] = (_rms(o, g_ref[...]) * (1.0 - lam_init)).astype(o_ref.dtype)


def _attention(q_arr, q_cb, q_rb, k_arr, k_cb, v_arr, v_cb, kv_rb, seq_q, seq_k, n_batch, n_heads,
               dl, g_sub, lam_init, ctx=None, name="attn"):
    tq = _pick_tile(seq_q, (256, 128))
    nq = seq_q // tq
    in_specs = [
        pl.BlockSpec((tq, HEAD_W), lambda b, h, i: ((q_rb + b) * nq + i, q_cb + h)),
        pl.BlockSpec((seq_k, HEAD_W), lambda b, h, i: (kv_rb + b, k_cb + h)),
        pl.BlockSpec((seq_k, HEAD_W), lambda b, h, i: (kv_rb + b, v_cb + h)),
    ]
    args = [q_arr, k_arr, v_arr]
    if ctx is not None:
        kc_arr, kc_cb, vc_arr, vc_cb, c_rb, seq_c = ctx
        in_specs += [
            pl.BlockSpec((seq_c, HEAD_W), lambda b, h, i: (c_rb + b, kc_cb + h)),
            pl.BlockSpec((seq_c, HEAD_W), lambda b, h, i: (c_rb + b, vc_cb + h)),
        ]
        args += [kc_arr, vc_arr]
    in_specs += [
        pl.BlockSpec(dl.shape, lambda b, h, i: (0, 0)),
        pl.BlockSpec((1, HEAD_W), lambda b, h, i: (0, 0)),
    ]
    args += [dl, g_sub]
    return pl.pallas_call(
        functools.partial(_attn_body, two_seg=ctx is not None, lam_init=lam_init),
        grid=(n_batch, n_heads, nq),
        in_specs=in_specs,
        out_specs=pl.BlockSpec((tq, HEAD_W), lambda b, h, i: (b * nq + i, h)),
        out_shape=jax.ShapeDtypeStruct((n_batch * seq_q, n_heads * HEAD_W), BF16),
        compiler_params=_cparams("parallel", "parallel", "arbitrary"),
        name=name,
    )(*args)


def _branches_body(u_ref, v_ref, x_ref, b_ref, c_ref, xp_ref, cp_ref, xn_ref, cn_ref,
                   lng_ref, lnb_ref, sgw_ref, sgb_ref, cw_ref, sg_o_ref, cv_o_ref,
                   *, lat_tiles, tiles_per_seq):
    i = pl.program_id(0)
    tm, w = u_ref.shape
    u = _gelu_tanh(u_ref[...].astype(F32))
    v = _gelu_tanh(v_ref[...].astype(F32))
    mu = jnp.mean(v, axis=-1, keepdims=True)
    dv = v - mu
    v = dv * lax.rsqrt(jnp.mean(dv * dv, axis=-1, keepdims=True) + NORM_EPS)
    v = (v * lng_ref[...] + lnb_ref[...]).astype(BF16)
    gw = w // SG_GROUPS
    for c in range(tm // SG_CHUNK):
        rs = slice(c * SG_CHUNK, (c + 1) * SG_CHUNK)
        for g in range(SG_GROUPS):
            cs = slice(g * gw, (g + 1) * gw)
            mixed = jnp.dot(sgw_ref[g], v[rs, cs], preferred_element_type=F32) + sgb_ref[:, cs]
            sg_o_ref[rs, cs] = (u[rs, cs] * mixed).astype(sg_o_ref.dtype)
    is_lat = i < lat_tiles
    seq_first = jnp.logical_or(jnp.logical_not(is_lat), i % tiles_per_seq == 0)
    seq_last = jnp.logical_or(jnp.logical_not(is_lat), i % tiles_per_seq == tiles_per_seq - 1)
    z = c_ref[...].astype(F32) * x_ref[...].astype(F32)
    hp = xp_ref.shape[0]
    z_prev = (cp_ref[...].astype(F32) * xp_ref[...].astype(F32))[hp - 1:hp, :]
    z_next = (cn_ref[...].astype(F32) * xn_ref[...].astype(F32))[0:1, :]
    z_prev = jnp.where(seq_first, 0.0, z_prev)
    z_next = jnp.where(seq_last, 0.0, z_next)
    row = lax.broadcasted_iota(jnp.int32, z.shape, 0)
    z_dn = jnp.where(row == 0, z_prev, pltpu.roll(z, 1, 0))
    z_up = jnp.where(row == tm - 1, z_next, pltpu.roll(z, tm - 1, 0))
    y = cw_ref[0:1, :] * z_dn + cw_ref[1:2, :] * z + cw_ref[2:3, :] * z_up
    cv_o_ref[...] = (b_ref[...].astype(F32) * y).astype(cv_o_ref.dtype)


def _branches(y, rows, lat_rows, seq, mix_w, col0, sg_ln_g, sg_ln_b, sg_w, sg_bias_map, conv_w):
    tm = SEQ_TILE
    halo = 16
    hb = tm // halo
    n_tiles = rows // tm
    last_hblk = rows // halo - 1
    cur = lambda c: pl.BlockSpec((tm, mix_w), lambda i: (i, col0 + c))
    prev = lambda c: pl.BlockSpec((halo, mix_w), lambda i: (jnp.maximum(i * hb - 1, 0), col0 + c))
    nxt = lambda c: pl.BlockSpec((halo, mix_w), lambda i: (jnp.minimum((i + 1) * hb, last_hblk), col0 + c))
    full = lambda a: pl.BlockSpec(a.shape, lambda i: (0,) * a.ndim)
    consts = [sg_ln_g, sg_ln_b, sg_w, sg_bias_map, conv_w]
    out = jax.ShapeDtypeStruct((rows, mix_w), BF16)
    return pl.pallas_call(
        functools.partial(_branches_body, lat_tiles=lat_rows // tm, tiles_per_seq=seq // tm),
        grid=(n_tiles,),
        in_specs=[cur(0), cur(1), cur(2), cur(3), cur(4), prev(2), prev(4), nxt(2), nxt(4)]
                 + [full(a) for a in consts],
        out_specs=[pl.BlockSpec((tm, mix_w), lambda i: (i, 0))] * 2,
        out_shape=[out, out],
        compiler_params=_cparams("parallel"),
        name="sg_conv_branches",
    )(y, y, y, y, y, y, y, y, y, *consts)


def _merge_body(a_ref, s_ref, c_ref, g0_ref, g1_ref, g2_ref, wb_ref, o_ref):
    acc = None
    for g, (br, gl) in enumerate(((a_ref, g0_ref), (s_ref, g1_ref), (c_ref, g2_ref))):
        t = _sigmoid(gl[...].astype(F32)) * jnp.dot(br[...], wb_ref[g], preferred_element_type=F32)
        acc = t if acc is None else acc + t
    o_ref[...] = acc.astype(o_ref.dtype)


def _merge(attn_o, sg_o, cv_o, y, gate_cb, w_branch, rows):
    mix_w, d = w_branch.shape[1], w_branch.shape[2]
    tm = _pick_tile(rows, (512, 256))
    br = pl.BlockSpec((tm, mix_w), lambda i: (i, 0))
    gate = lambda g: pl.BlockSpec((tm, d), lambda i: (i, gate_cb + g))
    return pl.pallas_call(
        _merge_body,
        grid=(rows // tm,),
        in_specs=[br, br, br, gate(0), gate(1), gate(2),
                  pl.BlockSpec(w_branch.shape, lambda i: (0, 0, 0), pipeline_mode=pl.Buffered(1))],
        out_specs=pl.BlockSpec((tm, d), lambda i: (i, 0)),
        out_shape=jax.ShapeDtypeStruct((rows, d), BF16),
        compiler_params=_cparams("parallel"),
        name="branch_merge",
    )(attn_o, sg_o, cv_o, y, y, y, w_branch)


def _outproj_body(m_ref, w_ref, x_ref, g_ref, gt_ref, o_ref):
    mix = jnp.dot(m_ref[...], w_ref[...], preferred_element_type=F32)
    o_ref[...] = x_ref[...] + gt_ref[0] * _rms(mix, g_ref[...])


def _outproj(merged, w_out, xa, g, mod3, gt_idx, rows, seq, n_batch):
    d = xa.shape[1]
    tm = _pick_tile(seq, (512, 256))
    tps = seq // tm
    grp = lambda i: _group_of_tile(i, tps, n_batch)
    return pl.pallas_call(
        _outproj_body,
        grid=(rows // tm,),
        in_specs=[
            pl.BlockSpec((tm, d), lambda i: (i, 0)),
            pl.BlockSpec(w_out.shape, lambda i: (0, 0), pipeline_mode=pl.Buffered(1)),
            pl.BlockSpec((tm, d), lambda i: (i, 0)),
            pl.BlockSpec((1, d), lambda i: (0, 0)),
            pl.BlockSpec((1, 1, d), lambda i: (grp(i), 0, gt_idx)),
        ],
        out_specs=pl.BlockSpec((tm, d), lambda i: (i, 0)),
        out_shape=jax.ShapeDtypeStruct(xa.shape, F32),
        input_output_aliases={2: 0},
        compiler_params=_cparams("parallel"),
        name="out_proj_residual",
    )(merged, w_out, xa, g, mod3)


def _top1_mask(vals, idx, n, axis):
    m = jnp.max(vals, axis=axis, keepdims=True)
    first = jnp.min(jnp.where(vals == m, idx, n), axis=axis, keepdims=True)
    return idx == first, m, first


def _route_body(x_ref, g_ref, sc_ref, sh_ref, wr_ref, rb_ref, hp_ref, wt_ref, ix_ref):
    h2 = _rms(x_ref[...], g_ref[...]) * (1.0 + sc_ref[0]) + sh_ref[0]
    hp_ref[...] = _pack_bf16_pairs(h2)
    n_exp = wr_ref.shape[0]
    tm = x_ref.shape[0]
    per_group = n_exp // N_GROUPS
    logits = lax.dot_general(wr_ref[...], h2, (((1,), (1,)), ((), ())),
                             precision=HIGHEST, preferred_element_type=F32)
    scores = _sigmoid(logits)
    choice = scores + rb_ref[...]
    neg = -jnp.inf
    ji = lax.broadcasted_iota(jnp.int32, (per_group, tm), 0)
    groups, gs = [], []
    for g in range(N_GROUPS):
        cg = choice[g * per_group:(g + 1) * per_group, :]
        hit, m1, _ = _top1_mask(cg, ji, per_group, 0)
        m2 = jnp.max(jnp.where(hit, neg, cg), axis=0, keepdims=True)
        groups.append(cg)
        gs.append(m1 + m2)
    kept = []
    for g in range(N_GROUPS):
        beaten_by = jnp.zeros((1, tm), jnp.int32)
        for o in range(N_GROUPS):
            if o != g:
                beats = (gs[o] >= gs[g]) if o < g else (gs[o] > gs[g])
                beaten_by = beaten_by + beats.astype(jnp.int32)
        kept.append(jnp.where(beaten_by < TOPK_GROUPS, groups[g], neg))
    cm = jnp.concatenate(kept, axis=0)
    ei = lax.broadcasted_iota(jnp.int32, cm.shape, 0)
    ws, ids = [], []
    for _ in range(TOP_K):
        hit, _, first = _top1_mask(cm, ei, n_exp, 0)
        ws.append(jnp.sum(jnp.where(hit, scores, 0.0), axis=0, keepdims=True))
        ids.append(first)
        cm = jnp.where(hit, neg, cm)
    w = jnp.concatenate(ws, axis=0)
    wt_ref[...] = w / jnp.sum(w, axis=0, keepdims=True) * ROUTED_SCALE
    ix_ref[...] = jnp.concatenate(ids, axis=0)


def _route(xa, g, mod3, sc_idx, sh_idx, w_router_t, router_bias, rows, seq, n_batch):
    d = xa.shape[1]
    n_exp = w_router_t.shape[0]
    tm = SEQ_TILE
    tps = seq // tm
    grp = lambda i: _group_of_tile(i, tps, n_batch)
    return pl.pallas_call(
        _route_body,
        grid=(rows // tm,),
        in_specs=[
            pl.BlockSpec((tm, d), lambda i: (i, 0)),
            pl.BlockSpec((1, d), lambda i: (0, 0)),
            pl.BlockSpec((1, 1, d), lambda i: (grp(i), 0, sc_idx)),
            pl.BlockSpec((1, 1, d), lambda i: (grp(i), 0, sh_idx)),
            pl.BlockSpec((n_exp, d), lambda i: (0, 0)),
            pl.BlockSpec((n_exp, 1), lambda i: (0, 0)),
        ],
        out_specs=[
            pl.BlockSpec((tm, d // 2), lambda i: (i, 0)),
            pl.BlockSpec((TOP_K, tm), lambda i: (0, i)),
            pl.BlockSpec((TOP_K, tm), lambda i: (0, i)),
        ],
        out_shape=[
            jax.ShapeDtypeStruct((rows, d // 2), U32),
            jax.ShapeDtypeStruct((TOP_K, rows), F32),
            jax.ShapeDtypeStruct((TOP_K, rows), jnp.int32),
        ],
        compiler_params=_cparams("parallel"),
        name="moe_norm_route",
    )(xa, g, mod3, mod3, w_router_t, router_bias.reshape(n_exp, 1))


def _dispatch_plan(idx_t, n_exp, tile):
    rows = idx_t.shape[1]
    sel = jnp.any(idx_t[None, :, :] == jnp.arange(n_exp, dtype=jnp.int32)[:, None, None], axis=1)
    sel = sel.astype(jnp.int32)
    rank = jnp.cumsum(sel, axis=1) - sel
    counts = jnp.sum(sel, axis=1)
    tiles_e = (counts + tile - 1) // tile
    tile_end = jnp.cumsum(tiles_e)
    offs = (tile_end - tiles_e) * tile
    pos = jnp.take_along_axis(rank, idx_t, axis=0) + offs[idx_t]
    max_tiles = (TOP_K * rows) // tile + n_exp
    t = jnp.arange(max_tiles, dtype=jnp.int32)
    used = tile_end[-1]
    t_eff = jnp.minimum(t, used - 1)
    tile_expert = jnp.minimum(jnp.searchsorted(tile_end, t_eff, side="right"), n_exp - 1)
    return pos.astype(jnp.int32), tile_expert.astype(jnp.int32), t_eff.astype(jnp.int32), \
        (t < used).astype(jnp.int32), max_tiles


def _scatter_body(pos_ref, h_ref, xs_in_ref, xs_ref, sem):
    del xs_in_ref
    tm = h_ref.shape[0]

    def row_copy(r, slot):
        return pltpu.make_async_copy(h_ref.at[pl.ds(r, 1), :], xs_ref.at[pl.ds(slot, 1), :], sem)

    def issue(r, carry):
        for k in range(TOP_K):
            row_copy(r, pos_ref[k, r]).start()
        return carry

    def drain(r, carry):
        for k in range(TOP_K):
            row_copy(r, pos_ref[k, r]).wait()
        return carry

    lax.fori_loop(0, tm, issue, 0)
    lax.fori_loop(0, tm, drain, 0)


def _scatter(pos, hp, slots):
    rows, wp = hp.shape
    tm = SEQ_TILE
    return pl.pallas_call(
        _scatter_body,
        grid=(rows // tm,),
        in_specs=[
            pl.BlockSpec((TOP_K, tm), lambda i: (0, i), memory_space=pltpu.SMEM),
            pl.BlockSpec((tm, wp), lambda i: (i, 0)),
            pl.BlockSpec(memory_space=pl.ANY),
        ],
        out_specs=pl.BlockSpec(memory_space=pl.ANY),
        out_shape=jax.ShapeDtypeStruct((slots, wp), U32),
        scratch_shapes=[pltpu.SemaphoreType.DMA(())],
        input_output_aliases={2: 0},
        compiler_params=_cparams("arbitrary"),
        name="moe_dispatch",
    )(pos, hp, jnp.zeros((slots, wp), U32))


def _ffn_rows(xw, wgu_ref, wd_ref):
    x = jnp.concatenate([_unpack_lo(xw).astype(BF16), _unpack_hi(xw).astype(BF16)], axis=1)
    gu = jnp.dot(x, wgu_ref, preferred_element_type=F32)
    f = gu.shape[1] // 2
    hdn = (_silu(gu[:, :f]) * gu[:, f:]).astype(BF16)
    return jnp.dot(hdn, wd_ref, preferred_element_type=F32)


def _experts_body(te_ref, ts_ref, tv_ref, xs_ref, wgu_ref, wd_ref, ys_ref):
    del te_ref, ts_ref
    valid = tv_ref[pl.program_id(0)] == 1

    @pl.when(valid)
    def _():
        ys_ref[...] = _pack_bf16_pairs(_ffn_rows(xs_ref[...], wgu_ref[0], wd_ref[0]))

    @pl.when(jnp.logical_not(valid))
    def _():
        ys_ref[...] = jnp.zeros_like(ys_ref)


def _experts(tile_expert, tile_src, tile_valid, xs, w_gu, w_d, max_tiles):
    slots, wp = xs.shape
    tm = EXPERT_TILE
    d, f2 = w_gu.shape[1], w_gu.shape[2]
    grid_spec = pltpu.PrefetchScalarGridSpec(
        num_scalar_prefetch=3,
        grid=(max_tiles,),
        in_specs=[
            pl.BlockSpec((tm, wp), lambda t, te, ts, tv: (ts[t], 0)),
            pl.BlockSpec((1, d, f2), lambda t, te, ts, tv: (te[t], 0, 0)),
            pl.BlockSpec((1, f2 // 2, d), lambda t, te, ts, tv: (te[t], 0, 0)),
        ],
        out_specs=pl.BlockSpec((tm, wp), lambda t, te, ts, tv: (t, 0)),
    )
    return pl.pallas_call(
        _experts_body,
        grid_spec=grid_spec,
        out_shape=jax.ShapeDtypeStruct((slots, wp), U32),
        compiler_params=_cparams("arbitrary"),
        name="moe_experts",
    )(tile_expert, tile_src, tile_valid, xs, w_gu, w_d)


def _combine_body(pos_ref, wt_ref, hp_ref, ys_ref, wgu_ref, wd_ref, x_ref, g_ref, gt_ref, o_ref,
                  gbuf, sem):
    tm = hp_ref.shape[0]

    def row_copy(k, r, slot):
        return pltpu.make_async_copy(ys_ref.at[pl.ds(slot, 1), :], gbuf.at[k, pl.ds(r, 1), :], sem)

    def issue(r, carry):
        for k in range(TOP_K):
            row_copy(k, r, pos_ref[k, r]).start()
        return carry

    def drain(r, carry):
        for k in range(TOP_K):
            row_copy(k, r, pos_ref[k, r]).wait()
        return carry

    lax.fori_loop(0, tm, issue, 0)
    y = _ffn_rows(hp_ref[...], wgu_ref[...], wd_ref[...])
    lax.fori_loop(0, tm, drain, 0)
    half = y.shape[1] // 2
    lo, hi = y[:, :half], y[:, half:]
    wt = wt_ref[...]
    for k in range(TOP_K):
        wk = wt[:, k:k + 1]
        gk = gbuf[k]
        lo = lo + wk * _unpack_lo(gk)
        hi = hi + wk * _unpack_hi(gk)
    y = jnp.concatenate([lo, hi], axis=1)
    o_ref[...] = x_ref[...] + gt_ref[0] * _rms(y, g_ref[...])


def _combine(pos, wts, hp, ys, w_sgu, w_sd, xa, g, mod3, gt_idx, rows, seq, n_batch):
    d = xa.shape[1]
    wp = hp.shape[1]
    tm = SEQ_TILE
    tps = seq // tm
    grp = lambda i: _group_of_tile(i, tps, n_batch)
    return pl.pallas_call(
        _combine_body,
        grid=(rows // tm,),
        in_specs=[
            pl.BlockSpec((TOP_K, tm), lambda i: (0, i), memory_space=pltpu.SMEM),
            pl.BlockSpec((tm, TOP_K), lambda i: (i, 0)),
            pl.BlockSpec((tm, wp), lambda i: (i, 0)),
            pl.BlockSpec(memory_space=pl.ANY),
            pl.BlockSpec(w_sgu.shape, lambda i: (0, 0), pipeline_mode=pl.Buffered(1)),
            pl.BlockSpec(w_sd.shape, lambda i: (0, 0), pipeline_mode=pl.Buffered(1)),
            pl.BlockSpec((tm, d), lambda i: (i, 0)),
            pl.BlockSpec((1, d), lambda i: (0, 0)),
            pl.BlockSpec((1, 1, d), lambda i: (grp(i), 0, gt_idx)),
        ],
        out_specs=pl.BlockSpec((tm, d), lambda i: (i, 0)),
        out_shape=jax.ShapeDtypeStruct(xa.shape, F32),
        scratch_shapes=[pltpu.VMEM((TOP_K, tm, wp), U32), pltpu.SemaphoreType.DMA(())],
        input_output_aliases={6: 0},
        compiler_params=_cparams("arbitrary"),
        name="moe_combine",
    )(pos, wts, hp, ys, w_sgu, w_sd, xa, g, mod3)


def kernel(x, c, ctx, c_ctx, w_ada, b_ada, norm_g, w_in, diff_lambda, g_subln, sg_ln_g, sg_ln_b, sg_w, sg_b, conv_w, w_branch, w_out, w_router, router_bias, w_exp_gate, w_exp_up, w_exp_down, w_sh_gate, w_sh_up, w_sh_down):
    n_batch, seq, d = x.shape
    ctx_len = ctx.shape[1]
    depth = w_ada.shape[0]
    mix_w = w_branch.shape[2]
    n_heads = mix_w // HEAD_W
    n_exp = w_router.shape[2]
    n_lat = n_batch * seq
    n_ctx = n_batch * ctx_len
    n_all = n_lat + n_ctx
    assert n_batch + 1 <= MOD_ROWS and seq % SEQ_TILE == 0 and ctx_len % SEQ_TILE == 0
    assert n_lat % seq == 0 and n_lat % ctx_len == 0 and mix_w // SG_GROUPS == LANES

    xa = jnp.concatenate([x.reshape(n_lat, d), ctx.reshape(n_ctx, d)], axis=0)
    cond = jnp.zeros((MOD_ROWS, d), F32).at[:n_batch].set(c).at[n_batch].set(c_ctx)
    mod = _ada(cond, w_ada, b_ada)
    cos, sin = _rope_tables(seq)
    q_cb, k_cb, v_cb = 0, n_heads, 2 * n_heads
    gate_cb = (8 * mix_w) // d

    for l in range(depth):
        last = l == depth - 1
        lam_init = 0.8 - 0.6 * math.exp(-0.3 * l)
        mod3 = mod[l].reshape(MOD_ROWS, 1, N_MOD * d)
        ng = norm_g[l]
        w_in_l = w_in[l].astype(BF16)
        rows = n_lat if last else n_all

        h = _normmod(xa, ng[0:1], mod3, 1, 0, n_all, seq, n_batch)
        if last:
            y = _matmul(h, w_in_l, n_lat, 0, w_in_l.shape[1], 0, BF16, "in_proj")
            yc = _matmul(h, w_in_l, n_ctx, n_lat, 2 * mix_w, mix_w, BF16, "in_proj_ctx_kv")
            ctx_kv = (yc, 0, yc, n_heads, 0, ctx_len)
        else:
            y = _matmul(h, w_in_l, n_all, 0, w_in_l.shape[1], 0, BF16, "in_proj")
            ctx_kv = (y, k_cb, y, v_cb, n_lat // ctx_len, ctx_len)
        qk = _rope(y, cos, sin, n_lat, seq, 2 * mix_w)
        attn_o = _attention(qk, q_cb, 0, qk, k_cb, y, v_cb, 0, seq, seq, n_batch, n_heads,
                            diff_lambda[l], g_subln[l:l + 1], lam_init, ctx=ctx_kv, name="attn_latent")
        if not last:
            attn_c = _attention(y, q_cb, n_lat // ctx_len, y, k_cb, y, v_cb, n_lat // ctx_len,
                                ctx_len, ctx_len, n_batch, n_heads,
                                diff_lambda[l], g_subln[l:l + 1], lam_init, name="attn_ctx")
            attn_o = jnp.concatenate([attn_o, attn_c], axis=0)
        sg_bias_map = jnp.repeat(sg_b[l].T, mix_w // SG_GROUPS, axis=1)
        sg_o, cv_o = _branches(y, rows, n_lat, seq, mix_w, 3, sg_ln_g[l:l + 1], sg_ln_b[l:l + 1],
                               sg_w[l].astype(BF16), sg_bias_map, conv_w[l])
        merged = _merge(attn_o, sg_o, cv_o, y, gate_cb, w_branch[l].astype(BF16), rows)
        xa = _outproj(merged, w_out[l].astype(BF16), xa, ng[1:2], mod3, 2, rows, seq, n_batch)

        hp, wt_t, idx_t = _route(xa, ng[2:3], mod3, 4, 3, w_router[l].T, router_bias[l],
                                 rows, seq, n_batch)
        pos, tile_expert, tile_src, tile_valid, max_tiles = _dispatch_plan(idx_t, n_exp, EXPERT_TILE)
        xs = _scatter(pos, hp, max_tiles * EXPERT_TILE)
        w_gu = jnp.concatenate([w_exp_gate[l], w_exp_up[l]], axis=-1).astype(BF16)
        ys = _experts(tile_expert, tile_src, tile_valid, xs, w_gu, w_exp_down[l].astype(BF16), max_tiles)
        w_sgu = jnp.concatenate([w_sh_gate[l], w_sh_up[l]], axis=-1).astype(BF16)
        xa = _combine(pos, wt_t.T, hp, ys, w_sgu, w_sh_down[l].astype(BF16), xa, ng[3:4], mod3, 5,
                      rows, seq, n_batch)

    return xa[:n_lat].reshape(n_batch, seq, d)
```

```python
import functools
import math

import jax
import jax.numpy as jnp
from jax import lax
from jax.experimental import pallas as pl
from jax.experimental.pallas import tpu as pltpu

GRID_W = 64
QK_DIM = 64
HEAD_W = 2 * QK_DIM
ROPE_BASE = 10000.0
SG_CHUNK = 128
SG_GROUPS = 8
TOP_K = 8
N_GROUPS = 8
TOPK_GROUPS = 4
ROUTED_SCALE = 2.5
NORM_EPS = 1e-6
N_MOD = 6

LANES = 128
VMEM_LIMIT = 56 * 1024 * 1024
SEQ_TILE = 256
EXPERT_TILE = 256
MOD_ROWS = 16

F32 = jnp.float32
BF16 = jnp.bfloat16
U32 = jnp.uint32
HIGHEST = lax.Precision.HIGHEST


def _cparams(*sem):
    return pltpu.CompilerParams(dimension_semantics=sem, vmem_limit_bytes=VMEM_LIMIT)


def _pick_tile(n, cands):
    for c in cands:
        if n % c == 0:
            return c
    raise ValueError(f"no tile for {n} in {cands}")


def _sigmoid(x):
    return 1.0 / (1.0 + jnp.exp(-x))


def _silu(x):
    return x * _sigmoid(x)


def _gelu_tanh(x):
    c = math.sqrt(2.0 / math.pi)
    return x * (0.5 * (1.0 + jnp.tanh(c * (x + 0.044715 * (x * x * x)))))


def _rms(x, g):
    return x * lax.rsqrt(jnp.mean(x * x, axis=-1, keepdims=True) + NORM_EPS) * g


def _pack_bf16_pairs(x):
    w = x.shape[1] // 2
    bits = pltpu.bitcast(x.astype(BF16).astype(F32), U32)
    return (bits[:, :w] >> 16) | bits[:, w:]


def _unpack_lo(w):
    return pltpu.bitcast(w << 16, F32)


def _unpack_hi(w):
    return pltpu.bitcast(w & jnp.uint32(0xFFFF0000), F32)


def _concat_body(a_ref, b_ref, o_ref, *, a_tiles):
    i = pl.program_id(0)

    @pl.when(i < a_tiles)
    def _():
        o_ref[...] = a_ref[...]

    @pl.when(i >= a_tiles)
    def _():
        o_ref[...] = b_ref[...]


def _concat_rows(a, b):
    d = a.shape[1]
    tm = _pick_tile(math.gcd(a.shape[0], b.shape[0]), (512, 256))
    na, nb = a.shape[0] // tm, b.shape[0] // tm
    return pl.pallas_call(
        functools.partial(_concat_body, a_tiles=na),
        grid=(na + nb,),
        in_specs=[
            pl.BlockSpec((tm, d), lambda i: (jnp.minimum(i, na - 1), 0)),
            pl.BlockSpec((tm, d), lambda i: (jnp.maximum(i - na, 0), 0)),
        ],
        out_specs=pl.BlockSpec((tm, d), lambda i: (i, 0)),
        out_shape=jax.ShapeDtypeStruct((a.shape[0] + b.shape[0], d), a.dtype),
        compiler_params=_cparams("parallel"),
        name="concat_rows",
    )(a, b)


def _ada_body(c_ref, w_ref, b_ref, o_ref):
    a = _silu(c_ref[...])
    o_ref[0] = jnp.dot(a, w_ref[0], precision=HIGHEST, preferred_element_type=F32) + b_ref[0]


def _ada(cond, w_ada, b_ada):
    depth, d, n = w_ada.shape
    tn = _pick_tile(n, (1024, 512, 256, 128))
    return pl.pallas_call(
        _ada_body,
        grid=(depth, n // tn),
        in_specs=[
            pl.BlockSpec((MOD_ROWS, d), lambda l, j: (0, 0)),
            pl.BlockSpec((1, d, tn), lambda l, j: (l, 0, j)),
            pl.BlockSpec((1, 1, tn), lambda l, j: (l, 0, j)),
        ],
        out_specs=pl.BlockSpec((1, MOD_ROWS, tn), lambda l, j: (l, 0, j)),
        out_shape=jax.ShapeDtypeStruct((depth, MOD_ROWS, n), F32),
        compiler_params=_cparams("arbitrary", "arbitrary"),
        name="ada_mod",
    )(cond, w_ada, b_ada.reshape(depth, 1, n))


def _normmod_body(x_ref, g_ref, sc_ref, sh_ref, o_ref):
    y = _rms(x_ref[...], g_ref[...])
    o_ref[...] = (y * (1.0 + sc_ref[0]) + sh_ref[0]).astype(o_ref.dtype)


def _group_of_tile(i, tiles_per_seq, n_batch):
    return jnp.minimum(i // tiles_per_seq, n_batch)


def _normmod(xa, g, mod3, sc_idx, sh_idx, rows, seq, n_batch):
    d = xa.shape[1]
    tm = _pick_tile(seq, (512, 256))
    tps = seq // tm
    grp = lambda i: _group_of_tile(i, tps, n_batch)
    return pl.pallas_call(
        _normmod_body,
        grid=(rows // tm,),
        in_specs=[
            pl.BlockSpec((tm, d), lambda i: (i, 0)),
            pl.BlockSpec((1, d), lambda i: (0, 0)),
            pl.BlockSpec((1, 1, d), lambda i: (grp(i), 0, sc_idx)),
            pl.BlockSpec((1, 1, d), lambda i: (grp(i), 0, sh_idx)),
        ],
        out_specs=pl.BlockSpec((tm, d), lambda i: (i, 0)),
        out_shape=jax.ShapeDtypeStruct((rows, d), BF16),
        compiler_params=_cparams("parallel"),
        name="norm_mod",
    )(xa, g, mod3, mod3)


def _mm_body(a_ref, w_ref, o_ref):
    o_ref[...] = jnp.dot(a_ref[...], w_ref[...], preferred_element_type=F32).astype(o_ref.dtype)


def _matmul(a, w, layer, rows, row_off, cols, col_off, out_dtype, name):
    k = a.shape[1]
    tm = _pick_tile(math.gcd(rows, row_off) if row_off else rows, (1024, 512, 256))
    tn = _pick_tile(math.gcd(cols, col_off) if col_off else cols, (1024, 512, 256))
    ro, co = row_off // tm, col_off // tn
    return pl.pallas_call(
        _mm_body,
        grid=(rows // tm, cols // tn),
        in_specs=[
            pl.BlockSpec((tm, k), lambda i, j: (i + ro, 0)),
            pl.BlockSpec((None, k, tn), lambda i, j: (layer, 0, j + co)),
        ],
        out_specs=pl.BlockSpec((tm, tn), lambda i, j: (i, j)),
        out_shape=jax.ShapeDtypeStruct((rows, cols), out_dtype),
        compiler_params=_cparams("parallel", "arbitrary"),
        name=name,
    )(a, w)


def _rope_body(y_ref, cos_ref, sin_ref, o_ref):
    cos = cos_ref[...]
    sin = sin_ref[...]
    lane = lax.broadcasted_iota(jnp.int32, cos.shape, 1)
    first_half = (lane % (QK_DIM // 2)) < (QK_DIM // 4)
    for j in range(y_ref.shape[1] // LANES):
        sl = slice(j * LANES, (j + 1) * LANES)
        x = y_ref[:, sl].astype(F32)
        partner = jnp.where(first_half,
                            pltpu.roll(x, LANES - QK_DIM // 4, 1),
                            pltpu.roll(x, QK_DIM // 4, 1))
        o_ref[:, sl] = (x * cos + partner * sin).astype(o_ref.dtype)


def _rope_tables(seq):
    n_freq = QK_DIM // 4
    inv = ROPE_BASE ** (-jnp.arange(n_freq, dtype=F32) / n_freq)
    t = jnp.arange(seq)
    row = (t // GRID_W).astype(F32)
    col = (t % GRID_W).astype(F32)
    ang = jnp.stack([row[:, None] * inv, col[:, None] * inv], axis=1)
    cos = jnp.cos(ang)
    sin = jnp.sin(ang)
    cos64 = jnp.stack([cos, cos], axis=2).reshape(seq, QK_DIM)
    sin64 = jnp.stack([-sin, sin], axis=2).reshape(seq, QK_DIM)
    reps = LANES // QK_DIM
    return jnp.tile(cos64, (1, reps)), jnp.tile(sin64, (1, reps))


def _rope(y, cos, sin, rows, seq, width):
    tm = SEQ_TILE
    tps = seq // tm
    return pl.pallas_call(
        _rope_body,
        grid=(rows // tm,),
        in_specs=[
            pl.BlockSpec((tm, width), lambda i: (i, 0)),
            pl.BlockSpec((tm, LANES), lambda i: (i % tps, 0)),
            pl.BlockSpec((tm, LANES), lambda i: (i % tps, 0)),
        ],
        out_specs=pl.BlockSpec((tm, width), lambda i: (i, 0)),
        out_shape=jax.ShapeDtypeStruct((rows, width), BF16),
        compiler_params=_cparams("parallel"),
        name="rope_qk",
    )(y, cos, sin)


def _attn_body(*refs, two_seg, lam_init):
    if two_seg:
        q_ref, k_ref, v_ref, kc_ref, vc_ref, dl_ref, g_ref, o_ref, v1_ref, vc1_ref = refs
    else:
        q_ref, k_ref, v_ref, dl_ref, g_ref, o_ref, v1_ref = refs

    @pl.when(pl.program_id(2) == 0)
    def _():
        v1_ref[:, :HEAD_W] = v_ref[...]
        v1_ref[:, HEAD_W:] = jnp.ones(v_ref.shape, BF16)
        if two_seg:
            vc1_ref[:, :HEAD_W] = vc_ref[...]
            vc1_ref[:, HEAD_W:] = jnp.ones(vc_ref.shape, BF16)

    dl = dl_ref[...]
    lam = (jnp.exp(jnp.sum(dl[0:1] * dl[1:2], axis=-1, keepdims=True))
           - jnp.exp(jnp.sum(dl[2:3] * dl[3:4], axis=-1, keepdims=True)) + lam_init)
    q = q_ref[...] * jnp.asarray(QK_DIM ** -0.5, BF16)
    lane = lax.broadcasted_iota(jnp.int32, q.shape, 1)
    zero = jnp.zeros_like(q)
    nt = (((1,), (1,)), ((), ()))

    def softmax_v(qm):
        s = lax.dot_general(qm, k_ref[...], nt, preferred_element_type=F32)
        m = jnp.max(s, axis=-1, keepdims=True)
        if two_seg:
            sc = lax.dot_general(qm, kc_ref[...], nt, preferred_element_type=F32)
            m = jnp.maximum(m, jnp.max(sc, axis=-1, keepdims=True))
        acc = jnp.dot(jnp.exp(s - m).astype(BF16), v1_ref[...], preferred_element_type=F32)
        if two_seg:
            acc = acc + jnp.dot(jnp.exp(sc - m).astype(BF16), vc1_ref[...], preferred_element_type=F32)
        return acc[:, :HEAD_W] / acc[:, HEAD_W:]

    o = softmax_v(jnp.where(lane < QK_DIM, q, zero)) - lam * softmax_v(jnp.where(lane >= QK_DIM, q, zero))
    o_ref[...] = (_rms(o, g_ref[...]) * (1.0 - lam_init)).astype(o_ref.dtype)


def _attention(q_arr, q_cb, q_rb, k_arr, k_cb, v_arr, v_cb, kv_rb, seq_q, seq_k, n_batch, n_heads,
               dl, g_sub, lam_init, ctx=None, name="attn"):
    tq = _pick_tile(seq_q, (512, 256, 128))
    nq = seq_q // tq
    scratch = [pltpu.VMEM((seq_k, 2 * HEAD_W), BF16)]
    in_specs = [
        pl.BlockSpec((tq, HEAD_W), lambda b, h, i: ((q_rb + b) * nq + i, q_cb + h)),
        pl.BlockSpec((seq_k, HEAD_W), lambda b, h, i: (kv_rb + b, k_cb + h)),
        pl.BlockSpec((seq_k, HEAD_W), lambda b, h, i: (kv_rb + b, v_cb + h)),
    ]
    args = [q_arr, k_arr, v_arr]
    if ctx is not None:
        kc_arr, kc_cb, vc_arr, vc_cb, c_rb, seq_c = ctx
        in_specs += [
            pl.BlockSpec((seq_c, HEAD_W), lambda b, h, i: (c_rb + b, kc_cb + h)),
            pl.BlockSpec((seq_c, HEAD_W), lambda b, h, i: (c_rb + b, vc_cb + h)),
        ]
        args += [kc_arr, vc_arr]
        scratch.append(pltpu.VMEM((seq_c, 2 * HEAD_W), BF16))
    in_specs += [
        pl.BlockSpec(dl.shape, lambda b, h, i: (0, 0)),
        pl.BlockSpec((1, HEAD_W), lambda b, h, i: (0, 0)),
    ]
    args += [dl, g_sub]
    return pl.pallas_call(
        functools.partial(_attn_body, two_seg=ctx is not None, lam_init=lam_init),
        grid=(n_batch, n_heads, nq),
        in_specs=in_specs,
        out_specs=pl.BlockSpec((tq, HEAD_W), lambda b, h, i: (b * nq + i, h)),
        out_shape=jax.ShapeDtypeStruct((n_batch * seq_q, n_heads * HEAD_W), BF16),
        scratch_shapes=scratch,
        compiler_params=_cparams("arbitrary", "arbitrary", "arbitrary"),
        name=name,
    )(*args)


def _branches_body(u_ref, v_ref, x_ref, b_ref, c_ref, xp_ref, cp_ref, xn_ref, cn_ref,
                   lng_ref, lnb_ref, sgw_ref, sgb_ref, cw_ref, sg_o_ref, cv_o_ref,
                   *, lat_tiles, tiles_per_seq):
    i = pl.program_id(0)
    tm, w = u_ref.shape
    u = _gelu_tanh(u_ref[...].astype(F32))
    v = _gelu_tanh(v_ref[...].astype(F32))
    mu = jnp.mean(v, axis=-1, keepdims=True)
    dv = v - mu
    v = dv * lax.rsqrt(jnp.mean(dv * dv, axis=-1, keepdims=True) + NORM_EPS)
    v = (v * lng_ref[...] + lnb_ref[...]).astype(BF16)
    gw = w // SG_GROUPS
    for c in range(tm // SG_CHUNK):
        rs = slice(c * SG_CHUNK, (c + 1) * SG_CHUNK)
        for g in range(SG_GROUPS):
            cs = slice(g * gw, (g + 1) * gw)
            mixed = jnp.dot(sgw_ref[g], v[rs, cs], preferred_element_type=F32) + sgb_ref[:, cs]
            sg_o_ref[rs, cs] = (u[rs, cs] * mixed).astype(sg_o_ref.dtype)
    is_lat = i < lat_tiles
    seq_first = jnp.logical_or(jnp.logical_not(is_lat), i % tiles_per_seq == 0)
    seq_last = jnp.logical_or(jnp.logical_not(is_lat), i % tiles_per_seq == tiles_per_seq - 1)
    z = c_ref[...].astype(F32) * x_ref[...].astype(F32)
    hp = xp_ref.shape[0]
    z_prev = (cp_ref[...].astype(F32) * xp_ref[...].astype(F32))[hp - 1:hp, :]
    z_next = (cn_ref[...].astype(F32) * xn_ref[...].astype(F32))[0:1, :]
    z_prev = jnp.where(seq_first, 0.0, z_prev)
    z_next = jnp.where(seq_last, 0.0, z_next)
    row = lax.broadcasted_iota(jnp.int32, z.shape, 0)
    z_dn = jnp.where(row == 0, z_prev, pltpu.roll(z, 1, 0))
    z_up = jnp.where(row == tm - 1, z_next, pltpu.roll(z, tm - 1, 0))
    y = cw_ref[0:1, :] * z_dn + cw_ref[1:2, :] * z + cw_ref[2:3, :] * z_up
    cv_o_ref[...] = (b_ref[...].astype(F32) * y).astype(cv_o_ref.dtype)


def _branches(y, rows, lat_rows, seq, mix_w, col0, sg_ln_g, sg_ln_b, sg_w, sg_bias_map, conv_w):
    tm = SEQ_TILE
    halo = 16
    hb = tm // halo
    n_tiles = rows // tm
    last_hblk = rows // halo - 1
    cur = lambda c: pl.BlockSpec((tm, mix_w), lambda i: (i, col0 + c))
    prev = lambda c: pl.BlockSpec((halo, mix_w), lambda i: (jnp.maximum(i * hb - 1, 0), col0 + c))
    nxt = lambda c: pl.BlockSpec((halo, mix_w), lambda i: (jnp.minimum((i + 1) * hb, last_hblk), col0 + c))
    full = lambda a: pl.BlockSpec(a.shape, lambda i: (0,) * a.ndim)
    consts = [sg_ln_g, sg_ln_b, sg_w, sg_bias_map, conv_w]
    out = jax.ShapeDtypeStruct((rows, mix_w), BF16)
    return pl.pallas_call(
        functools.partial(_branches_body, lat_tiles=lat_rows // tm, tiles_per_seq=seq // tm),
        grid=(n_tiles,),
        in_specs=[cur(0), cur(1), cur(2), cur(3), cur(4), prev(2), prev(4), nxt(2), nxt(4)]
                 + [full(a) for a in consts],
        out_specs=[pl.BlockSpec((tm, mix_w), lambda i: (i, 0))] * 2,
        out_shape=[out, out],
        compiler_params=_cparams("parallel"),
        name="sg_conv_branches",
    )(y, y, y, y, y, y, y, y, y, *consts)


def _merge_body(a_ref, s_ref, c_ref, g0_ref, g1_ref, g2_ref, wb_ref, o_ref):
    acc = None
    for g, (br, gl) in enumerate(((a_ref, g0_ref), (s_ref, g1_ref), (c_ref, g2_ref))):
        t = _sigmoid(gl[...].astype(F32)) * jnp.dot(br[...], wb_ref[g], preferred_element_type=F32)
        acc = t if acc is None else acc + t
    o_ref[...] = acc.astype(o_ref.dtype)


def _merge(attn_o, sg_o, cv_o, y, gate_cb, w_branch, layer, rows):
    n_br, mix_w, d = w_branch.shape[1:]
    tm = _pick_tile(rows, (512, 256))
    br = pl.BlockSpec((tm, mix_w), lambda i: (i, 0))
    gate = lambda g: pl.BlockSpec((tm, d), lambda i: (i, gate_cb + g))
    return pl.pallas_call(
        _merge_body,
        grid=(rows // tm,),
        in_specs=[br, br, br, gate(0), gate(1), gate(2),
                  pl.BlockSpec((None, n_br, mix_w, d), lambda i: (layer, 0, 0, 0),
                               pipeline_mode=pl.Buffered(1))],
        out_specs=pl.BlockSpec((tm, d), lambda i: (i, 0)),
        out_shape=jax.ShapeDtypeStruct((rows, d), BF16),
        compiler_params=_cparams("parallel"),
        name="branch_merge",
    )(attn_o, sg_o, cv_o, y, y, y, w_branch)


def _outproj_body(m_ref, w_ref, x_ref, g_ref, gt_ref, o_ref):
    mix = jnp.dot(m_ref[...], w_ref[...], preferred_element_type=F32)
    o_ref[...] = x_ref[...] + gt_ref[0] * _rms(mix, g_ref[...])


def _outproj(merged, w_out, layer, xa, g, mod3, gt_idx, rows, seq, n_batch):
    d = xa.shape[1]
    tm = _pick_tile(seq, (512, 256))
    tps = seq // tm
    grp = lambda i: _group_of_tile(i, tps, n_batch)
    return pl.pallas_call(
        _outproj_body,
        grid=(rows // tm,),
        in_specs=[
            pl.BlockSpec((tm, d), lambda i: (i, 0)),
            pl.BlockSpec((None, d, d), lambda i: (layer, 0, 0), pipeline_mode=pl.Buffered(1)),
            pl.BlockSpec((tm, d), lambda i: (i, 0)),
            pl.BlockSpec((1, d), lambda i: (0, 0)),
            pl.BlockSpec((1, 1, d), lambda i: (grp(i), 0, gt_idx)),
        ],
        out_specs=pl.BlockSpec((tm, d), lambda i: (i, 0)),
        out_shape=jax.ShapeDtypeStruct(xa.shape, F32),
        input_output_aliases={2: 0},
        compiler_params=_cparams("parallel"),
        name="out_proj_residual",
    )(merged, w_out, xa, g, mod3)


def _top1_mask(vals, idx, n, axis):
    m = jnp.max(vals, axis=axis, keepdims=True)
    first = jnp.min(jnp.where(vals == m, idx, n), axis=axis, keepdims=True)
    return idx == first, m, first


def _route_body(x_ref, g_ref, sc_ref, sh_ref, wr_ref, rb_ref, hp_ref, wt_ref, ix_ref, rk_ref, cnt_ref):
    h2 = _rms(x_ref[...], g_ref[...]) * (1.0 + sc_ref[0]) + sh_ref[0]
    hp_ref[...] = _pack_bf16_pairs(h2)
    n_exp = wr_ref.shape[0]
    tm = x_ref.shape[0]
    per_group = n_exp // N_GROUPS
    logits = lax.dot_general(wr_ref[...], h2, (((1,), (1,)), ((), ())),
                             precision=HIGHEST, preferred_element_type=F32)
    scores = _sigmoid(logits)
    choice = scores + rb_ref[...]
    neg = -jnp.inf
    ji = lax.broadcasted_iota(jnp.int32, (per_group, tm), 0)
    groups, gs = [], []
    for g in range(N_GROUPS):
        cg = choice[g * per_group:(g + 1) * per_group, :]
        hit, m1, _ = _top1_mask(cg, ji, per_group, 0)
        m2 = jnp.max(jnp.where(hit, neg, cg), axis=0, keepdims=True)
        groups.append(cg)
        gs.append(m1 + m2)
    kept = []
    for g in range(N_GROUPS):
        beaten_by = jnp.zeros((1, tm), jnp.int32)
        for o in range(N_GROUPS):
            if o != g:
                beats = (gs[o] >= gs[g]) if o < g else (gs[o] > gs[g])
                beaten_by = beaten_by + beats.astype(jnp.int32)
        kept.append(jnp.where(beaten_by < TOPK_GROUPS, groups[g], neg))
    cm = jnp.concatenate(kept, axis=0)
    ei = lax.broadcasted_iota(jnp.int32, cm.shape, 0)
    ws, ids, hits = [], [], []
    sel = jnp.zeros(cm.shape, F32)
    for _ in range(TOP_K):
        hit, _, first = _top1_mask(cm, ei, n_exp, 0)
        ws.append(jnp.sum(jnp.where(hit, scores, 0.0), axis=0, keepdims=True))
        ids.append(first)
        hits.append(hit)
        sel = sel + jnp.where(hit, 1.0, 0.0)
        cm = jnp.where(hit, neg, cm)
    w = jnp.concatenate(ws, axis=0)
    wt_ref[...] = w / jnp.sum(w, axis=0, keepdims=True) * ROUTED_SCALE
    ix_ref[...] = jnp.concatenate(ids, axis=0)

    @pl.when(pl.program_id(0) == 0)
    def _():
        cnt_ref[...] = jnp.zeros_like(cnt_ref)

    before = (lax.broadcasted_iota(jnp.int32, (tm, tm), 0)
              < lax.broadcasted_iota(jnp.int32, (tm, tm), 1))
    within = jnp.dot(sel.astype(BF16), jnp.where(before, 1.0, 0.0).astype(BF16),
                     preferred_element_type=F32)
    rank = within + cnt_ref[:, 0:1]
    rk_ref[...] = jnp.concatenate(
        [jnp.sum(jnp.where(h, rank, 0.0), axis=0, keepdims=True) for h in hits], axis=0).astype(jnp.int32)
    cnt_ref[...] = cnt_ref[...] + jnp.sum(sel, axis=1, keepdims=True)


def _route(xa, g, mod3, sc_idx, sh_idx, w_router_t, router_bias, rows, seq, n_batch):
    d = xa.shape[1]
    n_exp = w_router_t.shape[0]
    tm = SEQ_TILE
    tps = seq // tm
    grp = lambda i: _group_of_tile(i, tps, n_batch)
    return pl.pallas_call(
        _route_body,
        grid=(rows // tm,),
        in_specs=[
            pl.BlockSpec((tm, d), lambda i: (i, 0)),
            pl.BlockSpec((1, d), lambda i: (0, 0)),
            pl.BlockSpec((1, 1, d), lambda i: (grp(i), 0, sc_idx)),
            pl.BlockSpec((1, 1, d), lambda i: (grp(i), 0, sh_idx)),
            pl.BlockSpec((n_exp, d), lambda i: (0, 0)),
            pl.BlockSpec((n_exp, 1), lambda i: (0, 0)),
        ],
        out_specs=[
            pl.BlockSpec((tm, d // 2), lambda i: (i, 0)),
            pl.BlockSpec((TOP_K, tm), lambda i: (0, i)),
            pl.BlockSpec((TOP_K, tm), lambda i: (0, i)),
            pl.BlockSpec((TOP_K, tm), lambda i: (0, i)),
            pl.BlockSpec((n_exp, LANES), lambda i: (0, 0)),
        ],
        out_shape=[
            jax.ShapeDtypeStruct((rows, d // 2), U32),
            jax.ShapeDtypeStruct((TOP_K, rows), F32),
            jax.ShapeDtypeStruct((TOP_K, rows), jnp.int32),
            jax.ShapeDtypeStruct((TOP_K, rows), jnp.int32),
            jax.ShapeDtypeStruct((n_exp, LANES), F32),
        ],
        compiler_params=_cparams("arbitrary"),
        name="moe_norm_route",
    )(xa, g, mod3, mod3, w_router_t, router_bias.reshape(n_exp, 1))


def _dispatch_plan(idx_t, rank_t, counts, tile):
    n_exp = counts.shape[0]
    rows = idx_t.shape[1]
    tiles_e = (counts + tile - 1) // tile
    tile_end = jnp.cumsum(tiles_e)
    offs = (tile_end - tiles_e) * tile
    onehot = idx_t[None, :, :] == jnp.arange(n_exp, dtype=jnp.int32)[:, None, None]
    pos = rank_t + jnp.sum(jnp.where(onehot, offs[:, None, None], 0), axis=0)
    max_tiles = (TOP_K * rows) // tile + n_exp
    t = jnp.arange(max_tiles, dtype=jnp.int32)
    used = tile_end[-1]
    t_eff = jnp.minimum(t, used - 1)
    tile_expert = jnp.minimum(jnp.sum(t_eff[:, None] >= tile_end[None, :], axis=1), n_exp - 1)
    return pos.astype(jnp.int32), tile_expert.astype(jnp.int32), t_eff.astype(jnp.int32), \
        (t < used).astype(jnp.int32), max_tiles


def _scatter_body(pos_ref, h_ref, xs_in_ref, xs_ref, sem):
    del xs_in_ref
    tm = h_ref.shape[0]

    def row_copy(r, slot):
        return pltpu.make_async_copy(h_ref.at[pl.ds(r, 1), :], xs_ref.at[pl.ds(slot, 1), :], sem)

    def issue(r, carry):
        for k in range(TOP_K):
            row_copy(r, pos_ref[k, r]).start(priority=k % 2)
        return carry

    def drain(r, carry):
        for k in range(TOP_K):
            row_copy(r, pos_ref[k, r]).wait()
        return carry

    lax.fori_loop(0, tm, issue, 0)
    lax.fori_loop(0, tm, drain, 0)


def _scatter(pos, hp, slots):
    rows, wp = hp.shape
    tm = SEQ_TILE
    return pl.pallas_call(
        _scatter_body,
        grid=(rows // tm,),
        in_specs=[
            pl.BlockSpec((TOP_K, tm), lambda i: (0, i), memory_space=pltpu.SMEM),
            pl.BlockSpec((tm, wp), lambda i: (i, 0)),
            pl.BlockSpec(memory_space=pl.ANY),
        ],
        out_specs=pl.BlockSpec(memory_space=pl.ANY),
        out_shape=jax.ShapeDtypeStruct((slots, wp), U32),
        scratch_shapes=[pltpu.SemaphoreType.DMA(())],
        input_output_aliases={2: 0},
        compiler_params=_cparams("arbitrary"),
        name="moe_dispatch",
    )(pos, hp, jnp.zeros((slots, wp), U32))


def _ffn_rows(xw, wgu_ref, wd_ref):
    x = jnp.concatenate([_unpack_lo(xw).astype(BF16), _unpack_hi(xw).astype(BF16)], axis=1)
    gu = jnp.dot(x, wgu_ref, preferred_element_type=F32)
    f = gu.shape[1] // 2
    hdn = (_silu(gu[:, :f]) * gu[:, f:]).astype(BF16)
    return jnp.dot(hdn, wd_ref, preferred_element_type=F32)


def _experts_body(te_ref, ts_ref, tv_ref, xs_ref, wg_ref, wu_ref, wd_ref, ys_ref, wgu_bf, wd_bf):
    del ts_ref
    t = pl.program_id(0)
    valid = tv_ref[t] == 1
    f = wg_ref.shape[-1]

    @pl.when(jnp.logical_or(t == 0, te_ref[t] != te_ref[jnp.maximum(t - 1, 0)]))
    def _():
        wgu_bf[:, :f] = wg_ref[0, 0].astype(BF16)
        wgu_bf[:, f:] = wu_ref[0, 0].astype(BF16)
        wd_bf[...] = wd_ref[0, 0].astype(BF16)

    @pl.when(valid)
    def _():
        ys_ref[...] = _pack_bf16_pairs(_ffn_rows(xs_ref[...], wgu_bf[...], wd_bf[...]))

    @pl.when(jnp.logical_not(valid))
    def _():
        ys_ref[...] = jnp.zeros_like(ys_ref)


def _experts(tile_expert, tile_src, tile_valid, xs, w_gate, w_up, w_down, layer, max_tiles):
    slots, wp = xs.shape
    tm = EXPERT_TILE
    d, f = w_gate.shape[2], w_gate.shape[3]
    grid_spec = pltpu.PrefetchScalarGridSpec(
        num_scalar_prefetch=3,
        grid=(max_tiles,),
        in_specs=[
            pl.BlockSpec((tm, wp), lambda t, te, ts, tv: (ts[t], 0)),
            pl.BlockSpec((1, 1, d, f), lambda t, te, ts, tv: (layer, te[t], 0, 0)),
            pl.BlockSpec((1, 1, d, f), lambda t, te, ts, tv: (layer, te[t], 0, 0)),
            pl.BlockSpec((1, 1, f, d), lambda t, te, ts, tv: (layer, te[t], 0, 0)),
        ],
        out_specs=pl.BlockSpec((tm, wp), lambda t, te, ts, tv: (t, 0)),
        scratch_shapes=[pltpu.VMEM((d, 2 * f), BF16), pltpu.VMEM((f, d), BF16)],
    )
    return pl.pallas_call(
        _experts_body,
        grid_spec=grid_spec,
        out_shape=jax.ShapeDtypeStruct((slots, wp), U32),
        compiler_params=_cparams("arbitrary"),
        name="moe_experts",
    )(tile_expert, tile_src, tile_valid, xs, w_gate, w_up, w_down)


def _combine_body(pos_ref, wt_ref, hp_ref, ys_ref, wgu_ref, wd_ref, x_ref, g_ref, gt_ref, o_ref,
                  gbuf, sem):
    tm = hp_ref.shape[0]

    def row_copy(k, r, slot):
        return pltpu.make_async_copy(ys_ref.at[pl.ds(slot, 1), :], gbuf.at[k, pl.ds(r, 1), :], sem)

    def issue(r, carry):
        for k in range(TOP_K):
            row_copy(k, r, pos_ref[k, r]).start(priority=k % 2)
        return carry

    def drain(r, carry):
        for k in range(TOP_K):
            row_copy(k, r, pos_ref[k, r]).wait()
        return carry

    lax.fori_loop(0, tm, issue, 0)
    y = _ffn_rows(hp_ref[...], wgu_ref[...], wd_ref[...])
    lax.fori_loop(0, tm, drain, 0)
    half = y.shape[1] // 2
    lo, hi = y[:, :half], y[:, half:]
    wt = wt_ref[...]
    for k in range(TOP_K):
        wk = wt[:, k:k + 1]
        gk = gbuf[k]
        lo = lo + wk * _unpack_lo(gk)
        hi = hi + wk * _unpack_hi(gk)
    y = jnp.concatenate([lo, hi], axis=1)
    o_ref[...] = x_ref[...] + gt_ref[0] * _rms(y, g_ref[...])


def _combine(pos, wts, hp, ys, w_sgu, w_sd, xa, g, mod3, gt_idx, rows, seq, n_batch, in_place):
    d = xa.shape[1]
    wp = hp.shape[1]
    tm = SEQ_TILE
    tps = seq // tm
    grp = lambda i: _group_of_tile(i, tps, n_batch)
    return pl.pallas_call(
        _combine_body,
        grid=(rows // tm,),
        in_specs=[
            pl.BlockSpec((TOP_K, tm), lambda i: (0, i), memory_space=pltpu.SMEM),
            pl.BlockSpec((tm, TOP_K), lambda i: (i, 0)),
            pl.BlockSpec((tm, wp), lambda i: (i, 0)),
            pl.BlockSpec(memory_space=pl.ANY),
            pl.BlockSpec(w_sgu.shape, lambda i: (0, 0), pipeline_mode=pl.Buffered(1)),
            pl.BlockSpec(w_sd.shape, lambda i: (0, 0), pipeline_mode=pl.Buffered(1)),
            pl.BlockSpec((tm, d), lambda i: (i, 0)),
            pl.BlockSpec((1, d), lambda i: (0, 0)),
            pl.BlockSpec((1, 1, d), lambda i: (grp(i), 0, gt_idx)),
        ],
        out_specs=pl.BlockSpec((tm, d), lambda i: (i, 0)),
        out_shape=jax.ShapeDtypeStruct(xa.shape if in_place else (rows, d), F32),
        scratch_shapes=[pltpu.VMEM((TOP_K, tm, wp), U32), pltpu.SemaphoreType.DMA(())],
        input_output_aliases={6: 0} if in_place else {},
        compiler_params=_cparams("arbitrary"),
        name="moe_combine",
    )(pos, wts, hp, ys, w_sgu, w_sd, xa, g, mod3)


def kernel(x, c, ctx, c_ctx, w_ada, b_ada, norm_g, w_in, diff_lambda, g_subln, sg_ln_g, sg_ln_b, sg_w, sg_b, conv_w, w_branch, w_out, w_router, router_bias, w_exp_gate, w_exp_up, w_exp_down, w_sh_gate, w_sh_up, w_sh_down):
    n_batch, seq, d = x.shape
    ctx_len = ctx.shape[1]
    depth = w_ada.shape[0]
    mix_w = w_branch.shape[2]
    n_heads = mix_w // HEAD_W
    n_exp = w_router.shape[2]
    n_lat = n_batch * seq
    n_ctx = n_batch * ctx_len
    n_all = n_lat + n_ctx
    assert n_batch + 1 <= MOD_ROWS and seq % SEQ_TILE == 0 and ctx_len % SEQ_TILE == 0
    assert n_lat % seq == 0 and n_lat % ctx_len == 0 and mix_w // SG_GROUPS == LANES

    xa = _concat_rows(x.reshape(n_lat, d), ctx.reshape(n_ctx, d))
    cond = jnp.zeros((MOD_ROWS, d), F32).at[:n_batch].set(c).at[n_batch].set(c_ctx)
    mod = _ada(cond, w_ada, b_ada)
    cos, sin = _rope_tables(seq)
    q_cb, k_cb, v_cb = 0, n_heads, 2 * n_heads
    gate_cb = (8 * mix_w) // d

    w_in_b = w_in.astype(BF16)
    w_branch_b = w_branch.astype(BF16)
    w_out_b = w_out.astype(BF16)
    n_in = w_in.shape[2]

    for l in range(depth):
        last = l == depth - 1
        lam_init = 0.8 - 0.6 * math.exp(-0.3 * l)
        mod3 = mod[l].reshape(MOD_ROWS, 1, N_MOD * d)
        ng = norm_g[l]
        rows = n_lat if last else n_all

        h = _normmod(xa, ng[0:1], mod3, 1, 0, n_all, seq, n_batch)
        if last:
            y = _matmul(h, w_in_b, l, n_lat, 0, n_in, 0, BF16, "in_proj")
            yc = _matmul(h, w_in_b, l, n_ctx, n_lat, 2 * mix_w, mix_w, BF16, "in_proj_ctx_kv")
            ctx_kv = (yc, 0, yc, n_heads, 0, ctx_len)
        else:
            y = _matmul(h, w_in_b, l, n_all, 0, n_in, 0, BF16, "in_proj")
            ctx_kv = (y, k_cb, y, v_cb, n_lat // ctx_len, ctx_len)
        qk = _rope(y, cos, sin, n_lat, seq, 2 * mix_w)
        attn_o = _attention(qk, q_cb, 0, qk, k_cb, y, v_cb, 0, seq, seq, n_batch, n_heads,
                            diff_lambda[l], g_subln[l:l + 1], lam_init, ctx=ctx_kv, name="attn_latent")
        if not last:
            attn_c = _attention(y, q_cb, n_lat // ctx_len, y, k_cb, y, v_cb, n_lat // ctx_len,
                                ctx_len, ctx_len, n_batch, n_heads,
                                diff_lambda[l], g_subln[l:l + 1], lam_init, name="attn_ctx")
            attn_o = jnp.concatenate([attn_o, attn_c], axis=0)
        sg_bias_map = jnp.repeat(sg_b[l].T, mix_w // SG_GROUPS, axis=1)
        sg_o, cv_o = _branches(y, rows, n_lat, seq, mix_w, 3, sg_ln_g[l:l + 1], sg_ln_b[l:l + 1],
                               sg_w[l].astype(BF16), sg_bias_map, conv_w[l])
        merged = _merge(attn_o, sg_o, cv_o, y, gate_cb, w_branch_b, l, rows)
        xa = _outproj(merged, w_out_b, l, xa, ng[1:2], mod3, 2, rows, seq, n_batch)

        hp, wt_t, idx_t, rank_t, cnt = _route(xa, ng[2:3], mod3, 4, 3, w_router[l].T, router_bias[l],
                                              rows, seq, n_batch)
        pos, tile_expert, tile_src, tile_valid, max_tiles = _dispatch_plan(
            idx_t, rank_t, cnt[:, 0].astype(jnp.int32), EXPERT_TILE)
        xs = _scatter(pos, hp, max_tiles * EXPERT_TILE)
        ys = _experts(tile_expert, tile_src, tile_valid, xs, w_exp_gate, w_exp_up, w_exp_down, l, max_tiles)
        w_sgu = jnp.concatenate([w_sh_gate[l], w_sh_up[l]], axis=-1).astype(BF16)
        xa = _combine(pos, wt_t.T, hp, ys, w_sgu, w_sh_down[l].astype(BF16), xa, ng[3:4], mod3, 5,
                      rows, seq, n_batch, in_place=not last)

    return xa.reshape(n_batch, seq, d)
```

```python
import functools
import math

import jax
import jax.numpy as jnp
from jax import lax
from jax.experimental import pallas as pl
from jax.experimental.pallas import tpu as pltpu

GRID_W = 64
QK_DIM = 64
HEAD_W = 2 * QK_DIM
ROPE_BASE = 10000.0
SG_CHUNK = 128
SG_GROUPS = 8
TOP_K = 8
N_GROUPS = 8
TOPK_GROUPS = 4
ROUTED_SCALE = 2.5
NORM_EPS = 1e-6
N_MOD = 6

LANES = 128
SUBLANES = 8
VMEM_LIMIT = 56 * 1024 * 1024
SEQ_TILE = 256
EXPERT_TILE = 256
MOD_ROWS = 16

F32 = jnp.float32
BF16 = jnp.bfloat16
U32 = jnp.uint32
HIGHEST = lax.Precision.HIGHEST


def _cparams(*sem):
    return pltpu.CompilerParams(dimension_semantics=sem, vmem_limit_bytes=VMEM_LIMIT)


def _pick_tile(n, cands):
    for c in cands:
        if n % c == 0:
            return c
    raise ValueError(f"no tile for {n} in {cands}")


def _sigmoid(x):
    return 1.0 / (1.0 + jnp.exp(-x))


def _silu(x):
    return x * _sigmoid(x)


def _gelu_tanh(x):
    c = math.sqrt(2.0 / math.pi)
    return x * (0.5 * (1.0 + jnp.tanh(c * (x + 0.044715 * (x * x * x)))))


def _rms(x, g):
    return x * lax.rsqrt(jnp.mean(x * x, axis=-1, keepdims=True) + NORM_EPS) * g


def _pack_bf16_pairs(x):
    w = x.shape[1] // 2
    bits = pltpu.bitcast(x.astype(BF16).astype(F32), U32)
    return (bits[:, :w] >> 16) | bits[:, w:]


def _unpack_lo(w):
    return pltpu.bitcast(w << 16, F32)


def _unpack_hi(w):
    return pltpu.bitcast(w & jnp.uint32(0xFFFF0000), F32)


def _store_row_tiles(ref, packed):
    m, s, _ = ref.shape
    flat = ref.reshape(m * s, LANES)
    for j in range(s):
        flat[pl.ds(j, m, stride=s), :] = packed[:, j * LANES:(j + 1) * LANES]


def _load_row_tiles(ref):
    m, s, _ = ref.shape
    flat = ref.reshape(m * s, LANES)
    return jnp.concatenate([flat[pl.ds(j, m, stride=s), :] for j in range(s)], axis=1)


def _concat_body(a_ref, b_ref, o_ref, *, a_tiles):
    i = pl.program_id(0)

    @pl.when(i < a_tiles)
    def _():
        o_ref[...] = a_ref[...]

    @pl.when(i >= a_tiles)
    def _():
        o_ref[...] = b_ref[...]


def _concat_rows(a, b):
    d = a.shape[1]
    tm = _pick_tile(math.gcd(a.shape[0], b.shape[0]), (512, 256))
    na, nb = a.shape[0] // tm, b.shape[0] // tm
    return pl.pallas_call(
        functools.partial(_concat_body, a_tiles=na),
        grid=(na + nb,),
        in_specs=[
            pl.BlockSpec((tm, d), lambda i: (jnp.minimum(i, na - 1), 0)),
            pl.BlockSpec((tm, d), lambda i: (jnp.maximum(i - na, 0), 0)),
        ],
        out_specs=pl.BlockSpec((tm, d), lambda i: (i, 0)),
        out_shape=jax.ShapeDtypeStruct((a.shape[0] + b.shape[0], d), a.dtype),
        compiler_params=_cparams("parallel"),
        name="concat_rows",
    )(a, b)


def _ada_body(c_ref, w_ref, b_ref, o_ref):
    a = _silu(c_ref[...])
    o_ref[0] = jnp.dot(a, w_ref[0], precision=HIGHEST, preferred_element_type=F32) + b_ref[0]


def _ada(cond, w_ada, b_ada):
    depth, d, n = w_ada.shape
    tn = _pick_tile(n, (1024, 512, 256, 128))
    return pl.pallas_call(
        _ada_body,
        grid=(depth, n // tn),
        in_specs=[
            pl.BlockSpec((MOD_ROWS, d), lambda l, j: (0, 0)),
            pl.BlockSpec((1, d, tn), lambda l, j: (l, 0, j)),
            pl.BlockSpec((1, 1, tn), lambda l, j: (l, 0, j)),
        ],
        out_specs=pl.BlockSpec((1, MOD_ROWS, tn), lambda l, j: (l, 0, j)),
        out_shape=jax.ShapeDtypeStruct((depth, MOD_ROWS, n), F32),
        compiler_params=_cparams("arbitrary", "arbitrary"),
        name="ada_mod",
    )(cond, w_ada, b_ada.reshape(depth, 1, n))


def _normmod_body(x_ref, g_ref, sc_ref, sh_ref, o_ref):
    y = _rms(x_ref[...], g_ref[...])
    o_ref[...] = (y * (1.0 + sc_ref[0]) + sh_ref[0]).astype(o_ref.dtype)


def _group_of_tile(i, tiles_per_seq, n_batch):
    return jnp.minimum(i // tiles_per_seq, n_batch)


def _normmod(xa, g, mod3, sc_idx, sh_idx, rows, seq, n_batch):
    d = xa.shape[1]
    tm = _pick_tile(seq, (512, 256))
    tps = seq // tm
    grp = lambda i: _group_of_tile(i, tps, n_batch)
    return pl.pallas_call(
        _normmod_body,
        grid=(rows // tm,),
        in_specs=[
            pl.BlockSpec((tm, d), lambda i: (i, 0)),
            pl.BlockSpec((1, d), lambda i: (0, 0)),
            pl.BlockSpec((1, 1, d), lambda i: (grp(i), 0, sc_idx)),
            pl.BlockSpec((1, 1, d), lambda i: (grp(i), 0, sh_idx)),
        ],
        out_specs=pl.BlockSpec((tm, d), lambda i: (i, 0)),
        out_shape=jax.ShapeDtypeStruct((rows, d), BF16),
        compiler_params=_cparams("parallel"),
        name="norm_mod",
    )(xa, g, mod3, mod3)


def _mm_body(a_ref, w_ref, o_ref):
    o_ref[...] = jnp.dot(a_ref[...], w_ref[...], preferred_element_type=F32).astype(o_ref.dtype)


def _matmul(a, w, layer, rows, row_off, cols, col_off, out_dtype, name):
    k = a.shape[1]
    tm = _pick_tile(math.gcd(rows, row_off) if row_off else rows, (1024, 512, 256))
    tn = _pick_tile(math.gcd(cols, col_off) if col_off else cols, (1024, 512, 256))
    ro, co = row_off // tm, col_off // tn
    return pl.pallas_call(
        _mm_body,
        grid=(rows // tm, cols // tn),
        in_specs=[
            pl.BlockSpec((tm, k), lambda i, j: (i + ro, 0)),
            pl.BlockSpec((None, k, tn), lambda i, j: (layer, 0, j + co)),
        ],
        out_specs=pl.BlockSpec((tm, tn), lambda i, j: (i, j)),
        out_shape=jax.ShapeDtypeStruct((rows, cols), out_dtype),
        compiler_params=_cparams("parallel", "arbitrary"),
        name=name,
    )(a, w)


def _rope_body(y_ref, cos_ref, sin_ref, o_ref):
    cos = cos_ref[...]
    sin = sin_ref[...]
    lane = lax.broadcasted_iota(jnp.int32, cos.shape, 1)
    first_half = (lane % (QK_DIM // 2)) < (QK_DIM // 4)
    for j in range(y_ref.shape[1] // LANES):
        sl = slice(j * LANES, (j + 1) * LANES)
        x = y_ref[:, sl].astype(F32)
        partner = jnp.where(first_half,
                            pltpu.roll(x, LANES - QK_DIM // 4, 1),
                            pltpu.roll(x, QK_DIM // 4, 1))
        o_ref[:, sl] = (x * cos + partner * sin).astype(o_ref.dtype)


def _rope_tables(seq):
    n_freq = QK_DIM // 4
    inv = ROPE_BASE ** (-jnp.arange(n_freq, dtype=F32) / n_freq)
    t = jnp.arange(seq)
    row = (t // GRID_W).astype(F32)
    col = (t % GRID_W).astype(F32)
    ang = jnp.stack([row[:, None] * inv, col[:, None] * inv], axis=1)
    cos = jnp.cos(ang)
    sin = jnp.sin(ang)
    cos64 = jnp.stack([cos, cos], axis=2).reshape(seq, QK_DIM)
    sin64 = jnp.stack([-sin, sin], axis=2).reshape(seq, QK_DIM)
    reps = LANES // QK_DIM
    return jnp.tile(cos64, (1, reps)), jnp.tile(sin64, (1, reps))


def _rope(y, cos, sin, rows, seq, width):
    tm = SEQ_TILE
    tps = seq // tm
    return pl.pallas_call(
        _rope_body,
        grid=(rows // tm,),
        in_specs=[
            pl.BlockSpec((tm, width), lambda i: (i, 0)),
            pl.BlockSpec((tm, LANES), lambda i: (i % tps, 0)),
            pl.BlockSpec((tm, LANES), lambda i: (i % tps, 0)),
        ],
        out_specs=pl.BlockSpec((tm, width), lambda i: (i, 0)),
        out_shape=jax.ShapeDtypeStruct((rows, width), BF16),
        compiler_params=_cparams("parallel"),
        name="rope_qk",
    )(y, cos, sin)


def _attn_body(*refs, two_seg, lam_init):
    if two_seg:
        q_ref, k_ref, v_ref, kc_ref, vc_ref, dl_ref, g_ref, o_ref, v1_ref, vc1_ref = refs
    else:
        q_ref, k_ref, v_ref, dl_ref, g_ref, o_ref, v1_ref = refs

    @pl.when(pl.program_id(2) == 0)
    def _():
        v1_ref[:, :HEAD_W] = v_ref[...]
        v1_ref[:, HEAD_W:] = jnp.ones(v_ref.shape, BF16)
        if two_seg:
            vc1_ref[:, :HEAD_W] = vc_ref[...]
            vc1_ref[:, HEAD_W:] = jnp.ones(vc_ref.shape, BF16)

    dl = dl_ref[...]
    lam = (jnp.exp(jnp.sum(dl[0:1] * dl[1:2], axis=-1, keepdims=True))
           - jnp.exp(jnp.sum(dl[2:3] * dl[3:4], axis=-1, keepdims=True)) + lam_init)
    q = q_ref[...] * jnp.asarray(QK_DIM ** -0.5, BF16)
    lane = lax.broadcasted_iota(jnp.int32, q.shape, 1)
    zero = jnp.zeros_like(q)
    nt = (((1,), (1,)), ((), ()))

    def softmax_v(qm):
        s = lax.dot_general(qm, k_ref[...], nt, preferred_element_type=F32)
        m = jnp.max(s, axis=-1, keepdims=True)
        if two_seg:
            sc = lax.dot_general(qm, kc_ref[...], nt, preferred_element_type=F32)
            m = jnp.maximum(m, jnp.max(sc, axis=-1, keepdims=True))
        acc = jnp.dot(jnp.exp(s - m).astype(BF16), v1_ref[...], preferred_element_type=F32)
        if two_seg:
            acc = acc + jnp.dot(jnp.exp(sc - m).astype(BF16), vc1_ref[...], preferred_element_type=F32)
        return acc[:, :HEAD_W] / acc[:, HEAD_W:]

    o = softmax_v(jnp.where(lane < QK_DIM, q, zero)) - lam * softmax_v(jnp.where(lane >= QK_DIM, q, zero))
    o_ref[...] = (_rms(o, g_ref[...]) * (1.0 - lam_init)).astype(o_ref.dtype)


def _attention(q_arr, q_cb, q_rb, k_arr, k_cb, v_arr, v_cb, kv_rb, seq_q, seq_k, n_batch, n_heads,
               dl, g_sub, lam_init, ctx=None, name="attn"):
    tq = _pick_tile(seq_q, (512, 256, 128))
    nq = seq_q // tq
    scratch = [pltpu.VMEM((seq_k, 2 * HEAD_W), BF16)]
    in_specs = [
        pl.BlockSpec((tq, HEAD_W), lambda b, h, i: ((q_rb + b) * nq + i, q_cb + h)),
        pl.BlockSpec((seq_k, HEAD_W), lambda b, h, i: (kv_rb + b, k_cb + h)),
        pl.BlockSpec((seq_k, HEAD_W), lambda b, h, i: (kv_rb + b, v_cb + h)),
    ]
    args = [q_arr, k_arr, v_arr]
    if ctx is not None:
        kc_arr, kc_cb, vc_arr, vc_cb, c_rb, seq_c = ctx
        in_specs += [
            pl.BlockSpec((seq_c, HEAD_W), lambda b, h, i: (c_rb + b, kc_cb + h)),
            pl.BlockSpec((seq_c, HEAD_W), lambda b, h, i: (c_rb + b, vc_cb + h)),
        ]
        args += [kc_arr, vc_arr]
        scratch.append(pltpu.VMEM((seq_c, 2 * HEAD_W), BF16))
    in_specs += [
        pl.BlockSpec(dl.shape, lambda b, h, i: (0, 0)),
        pl.BlockSpec((1, HEAD_W), lambda b, h, i: (0, 0)),
    ]
    args += [dl, g_sub]
    return pl.pallas_call(
        functools.partial(_attn_body, two_seg=ctx is not None, lam_init=lam_init),
        grid=(n_batch, n_heads, nq),
        in_specs=in_specs,
        out_specs=pl.BlockSpec((tq, HEAD_W), lambda b, h, i: (b * nq + i, h)),
        out_shape=jax.ShapeDtypeStruct((n_batch * seq_q, n_heads * HEAD_W), BF16),
        scratch_shapes=scratch,
        compiler_params=_cparams("arbitrary", "arbitrary", "arbitrary"),
        name=name,
    )(*args)


def _branches_body(u_ref, v_ref, x_ref, b_ref, c_ref, xp_ref, cp_ref, xn_ref, cn_ref,
                   lng_ref, lnb_ref, sgw_ref, sgb_ref, cw_ref, sg_o_ref, cv_o_ref,
                   *, lat_tiles, tiles_per_seq):
    i = pl.program_id(0)
    tm, w = u_ref.shape
    u = _gelu_tanh(u_ref[...].astype(F32))
    v = _gelu_tanh(v_ref[...].astype(F32))
    mu = jnp.mean(v, axis=-1, keepdims=True)
    dv = v - mu
    v = dv * lax.rsqrt(jnp.mean(dv * dv, axis=-1, keepdims=True) + NORM_EPS)
    v = (v * lng_ref[...] + lnb_ref[...]).astype(BF16)
    gw = w // SG_GROUPS
    for c in range(tm // SG_CHUNK):
        rs = slice(c * SG_CHUNK, (c + 1) * SG_CHUNK)
        for g in range(SG_GROUPS):
            cs = slice(g * gw, (g + 1) * gw)
            mixed = jnp.dot(sgw_ref[g], v[rs, cs], preferred_element_type=F32) + sgb_ref[:, cs]
            sg_o_ref[rs, cs] = (u[rs, cs] * mixed).astype(sg_o_ref.dtype)
    is_lat = i < lat_tiles
    seq_first = jnp.logical_or(jnp.logical_not(is_lat), i % tiles_per_seq == 0)
    seq_last = jnp.logical_or(jnp.logical_not(is_lat), i % tiles_per_seq == tiles_per_seq - 1)
    z = c_ref[...].astype(F32) * x_ref[...].astype(F32)
    hp = xp_ref.shape[0]
    z_prev = (cp_ref[...].astype(F32) * xp_ref[...].astype(F32))[hp - 1:hp, :]
    z_next = (cn_ref[...].astype(F32) * xn_ref[...].astype(F32))[0:1, :]
    z_prev = jnp.where(seq_first, 0.0, z_prev)
    z_next = jnp.where(seq_last, 0.0, z_next)
    row = lax.broadcasted_iota(jnp.int32, z.shape, 0)
    z_dn = jnp.where(row == 0, z_prev, pltpu.roll(z, 1, 0))
    z_up = jnp.where(row == tm - 1, z_next, pltpu.roll(z, tm - 1, 0))
    y = cw_ref[0:1, :] * z_dn + cw_ref[1:2, :] * z + cw_ref[2:3, :] * z_up
    cv_o_ref[...] = (b_ref[...].astype(F32) * y).astype(cv_o_ref.dtype)


def _branches(y, rows, lat_rows, seq, mix_w, col0, sg_ln_g, sg_ln_b, sg_w, sg_bias_map, conv_w):
    tm = SEQ_TILE
    halo = 16
    hb = tm // halo
    n_tiles = rows // tm
    last_hblk = rows // halo - 1
    cur = lambda c: pl.BlockSpec((tm, mix_w), lambda i: (i, col0 + c))
    prev = lambda c: pl.BlockSpec((halo, mix_w), lambda i: (jnp.maximum(i * hb - 1, 0), col0 + c))
    nxt = lambda c: pl.BlockSpec((halo, mix_w), lambda i: (jnp.minimum((i + 1) * hb, last_hblk), col0 + c))
    full = lambda a: pl.BlockSpec(a.shape, lambda i: (0,) * a.ndim)
    consts = [sg_ln_g, sg_ln_b, sg_w, sg_bias_map, conv_w]
    out = jax.ShapeDtypeStruct((rows, mix_w), BF16)
    return pl.pallas_call(
        functools.partial(_branches_body, lat_tiles=lat_rows // tm, tiles_per_seq=seq // tm),
        grid=(n_tiles,),
        in_specs=[cur(0), cur(1), cur(2), cur(3), cur(4), prev(2), prev(4), nxt(2), nxt(4)]
                 + [full(a) for a in consts],
        out_specs=[pl.BlockSpec((tm, mix_w), lambda i: (i, 0))] * 2,
        out_shape=[out, out],
        compiler_params=_cparams("parallel"),
        name="sg_conv_branches",
    )(y, y, y, y, y, y, y, y, y, *consts)


def _merge_body(a_ref, s_ref, c_ref, g0_ref, g1_ref, g2_ref, wb_ref, o_ref):
    acc = None
    for g, (br, gl) in enumerate(((a_ref, g0_ref), (s_ref, g1_ref), (c_ref, g2_ref))):
        t = _sigmoid(gl[...].astype(F32)) * jnp.dot(br[...], wb_ref[g], preferred_element_type=F32)
        acc = t if acc is None else acc + t
    o_ref[...] = acc.astype(o_ref.dtype)


def _merge(attn_o, sg_o, cv_o, y, gate_cb, w_branch, layer, rows):
    n_br, mix_w, d = w_branch.shape[1:]
    tm = _pick_tile(rows, (512, 256))
    br = pl.BlockSpec((tm, mix_w), lambda i: (i, 0))
    gate = lambda g: pl.BlockSpec((tm, d), lambda i: (i, gate_cb + g))
    return pl.pallas_call(
        _merge_body,
        grid=(rows // tm,),
        in_specs=[br, br, br, gate(0), gate(1), gate(2),
                  pl.BlockSpec((None, n_br, mix_w, d), lambda i: (layer, 0, 0, 0),
                               pipeline_mode=pl.Buffered(1))],
        out_specs=pl.BlockSpec((tm, d), lambda i: (i, 0)),
        out_shape=jax.ShapeDtypeStruct((rows, d), BF16),
        compiler_params=_cparams("parallel"),
        name="branch_merge",
    )(attn_o, sg_o, cv_o, y, y, y, w_branch)


def _outproj_body(m_ref, w_ref, x_ref, g_ref, gt_ref, o_ref):
    mix = jnp.dot(m_ref[...], w_ref[...], preferred_element_type=F32)
    o_ref[...] = x_ref[...] + gt_ref[0] * _rms(mix, g_ref[...])


def _outproj(merged, w_out, layer, xa, g, mod3, gt_idx, rows, seq, n_batch):
    d = xa.shape[1]
    tm = _pick_tile(seq, (512, 256))
    tps = seq // tm
    grp = lambda i: _group_of_tile(i, tps, n_batch)
    return pl.pallas_call(
        _outproj_body,
        grid=(rows // tm,),
        in_specs=[
            pl.BlockSpec((tm, d), lambda i: (i, 0)),
            pl.BlockSpec((None, d, d), lambda i: (layer, 0, 0), pipeline_mode=pl.Buffered(1)),
            pl.BlockSpec((tm, d), lambda i: (i, 0)),
            pl.BlockSpec((1, d), lambda i: (0, 0)),
            pl.BlockSpec((1, 1, d), lambda i: (grp(i), 0, gt_idx)),
        ],
        out_specs=pl.BlockSpec((tm, d), lambda i: (i, 0)),
        out_shape=jax.ShapeDtypeStruct(xa.shape, F32),
        input_output_aliases={2: 0},
        compiler_params=_cparams("parallel"),
        name="out_proj_residual",
    )(merged, w_out, xa, g, mod3)


def _top1_mask(vals, idx, n, axis):
    m = jnp.max(vals, axis=axis, keepdims=True)
    first = jnp.min(jnp.where(vals == m, idx, n), axis=axis, keepdims=True)
    return idx == first, m, first


def _route_body(x_ref, g_ref, sc_ref, sh_ref, wr_ref, rb_ref, hp_ref, wt_ref, ix_ref, rk_ref, cnt_ref):
    h2 = _rms(x_ref[...], g_ref[...]) * (1.0 + sc_ref[0]) + sh_ref[0]
    _store_row_tiles(hp_ref, _pack_bf16_pairs(h2))
    n_exp = wr_ref.shape[0]
    tm = x_ref.shape[0]
    per_group = n_exp // N_GROUPS
    logits = lax.dot_general(wr_ref[...], h2, (((1,), (1,)), ((), ())),
                             precision=HIGHEST, preferred_element_type=F32)
    scores = _sigmoid(logits)
    choice = scores + rb_ref[...]
    neg = -jnp.inf
    ji = lax.broadcasted_iota(jnp.int32, (per_group, tm), 0)
    groups, gs = [], []
    for g in range(N_GROUPS):
        cg = choice[g * per_group:(g + 1) * per_group, :]
        hit, m1, _ = _top1_mask(cg, ji, per_group, 0)
        m2 = jnp.max(jnp.where(hit, neg, cg), axis=0, keepdims=True)
        groups.append(cg)
        gs.append(m1 + m2)
    kept = []
    for g in range(N_GROUPS):
        beaten_by = jnp.zeros((1, tm), jnp.int32)
        for o in range(N_GROUPS):
            if o != g:
                beats = (gs[o] >= gs[g]) if o < g else (gs[o] > gs[g])
                beaten_by = beaten_by + beats.astype(jnp.int32)
        kept.append(jnp.where(beaten_by < TOPK_GROUPS, groups[g], neg))
    cm = jnp.concatenate(kept, axis=0)
    ei = lax.broadcasted_iota(jnp.int32, cm.shape, 0)
    ws, ids, hits = [], [], []
    sel = jnp.zeros(cm.shape, F32)
    for _ in range(TOP_K):
        hit, _, first = _top1_mask(cm, ei, n_exp, 0)
        ws.append(jnp.sum(jnp.where(hit, scores, 0.0), axis=0, keepdims=True))
        ids.append(first)
        hits.append(hit)
        sel = sel + jnp.where(hit, 1.0, 0.0)
        cm = jnp.where(hit, neg, cm)
    w = jnp.concatenate(ws, axis=0)
    wt_ref[...] = w / jnp.sum(w, axis=0, keepdims=True) * ROUTED_SCALE
    ix_ref[...] = jnp.concatenate(ids, axis=0)

    @pl.when(pl.program_id(0) == 0)
    def _():
        cnt_ref[...] = jnp.zeros_like(cnt_ref)

    before = (lax.broadcasted_iota(jnp.int32, (tm, tm), 0)
              < lax.broadcasted_iota(jnp.int32, (tm, tm), 1))
    within = jnp.dot(sel.astype(BF16), jnp.where(before, 1.0, 0.0).astype(BF16),
                     preferred_element_type=F32)
    rank = within + cnt_ref[:, 0:1]
    rk_ref[...] = jnp.concatenate(
        [jnp.sum(jnp.where(h, rank, 0.0), axis=0, keepdims=True) for h in hits], axis=0).astype(jnp.int32)
    cnt_ref[...] = cnt_ref[...] + jnp.sum(sel, axis=1, keepdims=True)


def _route(xa, g, mod3, sc_idx, sh_idx, w_router_t, router_bias, rows, seq, n_batch):
    d = xa.shape[1]
    n_exp = w_router_t.shape[0]
    tm = SEQ_TILE
    tps = seq // tm
    grp = lambda i: _group_of_tile(i, tps, n_batch)
    return pl.pallas_call(
        _route_body,
        grid=(rows // tm,),
        in_specs=[
            pl.BlockSpec((tm, d), lambda i: (i, 0)),
            pl.BlockSpec((1, d), lambda i: (0, 0)),
            pl.BlockSpec((1, 1, d), lambda i: (grp(i), 0, sc_idx)),
            pl.BlockSpec((1, 1, d), lambda i: (grp(i), 0, sh_idx)),
            pl.BlockSpec((n_exp, d), lambda i: (0, 0)),
            pl.BlockSpec((n_exp, 1), lambda i: (0, 0)),
        ],
        out_specs=[
            pl.BlockSpec((tm, SUBLANES, LANES), lambda i: (i, 0, 0)),
            pl.BlockSpec((TOP_K, tm), lambda i: (0, i)),
            pl.BlockSpec((TOP_K, tm), lambda i: (0, i)),
            pl.BlockSpec((TOP_K, tm), lambda i: (0, i)),
            pl.BlockSpec((n_exp, LANES), lambda i: (0, 0)),
        ],
        out_shape=[
            jax.ShapeDtypeStruct((rows, SUBLANES, LANES), U32),
            jax.ShapeDtypeStruct((TOP_K, rows), F32),
            jax.ShapeDtypeStruct((TOP_K, rows), jnp.int32),
            jax.ShapeDtypeStruct((TOP_K, rows), jnp.int32),
            jax.ShapeDtypeStruct((n_exp, LANES), F32),
        ],
        compiler_params=_cparams("arbitrary"),
        name="moe_norm_route",
    )(xa, g, mod3, mod3, w_router_t, router_bias.reshape(n_exp, 1))


def _dispatch_plan(idx_t, rank_t, counts, tile):
    n_exp = counts.shape[0]
    rows = idx_t.shape[1]
    tiles_e = (counts + tile - 1) // tile
    tile_end = jnp.cumsum(tiles_e)
    offs = (tile_end - tiles_e) * tile
    onehot = idx_t[None, :, :] == jnp.arange(n_exp, dtype=jnp.int32)[:, None, None]
    pos = rank_t + jnp.sum(jnp.where(onehot, offs[:, None, None], 0), axis=0)
    max_tiles = (TOP_K * rows) // tile + n_exp
    t = jnp.arange(max_tiles, dtype=jnp.int32)
    used = tile_end[-1]
    t_eff = jnp.minimum(t, used - 1)
    tile_expert = jnp.minimum(jnp.sum(t_eff[:, None] >= tile_end[None, :], axis=1), n_exp - 1)
    return pos.astype(jnp.int32), tile_expert.astype(jnp.int32), t_eff.astype(jnp.int32), \
        (t < used).astype(jnp.int32), max_tiles


def _scatter_body(pos_ref, h_ref, xs_in_ref, xs_ref, sem):
    del xs_in_ref
    tm = h_ref.shape[0]

    def issue(r, carry):
        for k in range(TOP_K):
            pltpu.make_async_copy(h_ref.at[r], xs_ref.at[pos_ref[k, r]], sem).start(
                priority=k % 2)
        return carry

    lax.fori_loop(0, tm, issue, 0)
    all_rows = xs_ref.at[pl.ds(0, TOP_K * tm)]
    pltpu.make_async_copy(all_rows, all_rows, sem).wait()


def _scatter(pos, hp, slots):
    rows = hp.shape[0]
    row_tile = hp.shape[1:]
    tm = SEQ_TILE
    return pl.pallas_call(
        _scatter_body,
        grid=(rows // tm,),
        in_specs=[
            pl.BlockSpec((TOP_K, tm), lambda i: (0, i), memory_space=pltpu.SMEM),
            pl.BlockSpec((tm,) + row_tile, lambda i: (i, 0, 0)),
            pl.BlockSpec(memory_space=pl.ANY),
        ],
        out_specs=pl.BlockSpec(memory_space=pl.ANY),
        out_shape=jax.ShapeDtypeStruct((slots,) + row_tile, U32),
        scratch_shapes=[pltpu.SemaphoreType.DMA(())],
        input_output_aliases={2: 0},
        compiler_params=_cparams("arbitrary"),
        name="moe_dispatch",
    )(pos, hp, jnp.zeros((slots,) + row_tile, U32))


def _ffn_rows(xw, wgu_ref, wd_ref):
    x = jnp.concatenate([_unpack_lo(xw).astype(BF16), _unpack_hi(xw).astype(BF16)], axis=1)
    gu = jnp.dot(x, wgu_ref, preferred_element_type=F32)
    f = gu.shape[1] // 2
    hdn = (_silu(gu[:, :f]) * gu[:, f:]).astype(BF16)
    return jnp.dot(hdn, wd_ref, preferred_element_type=F32)


def _experts_body(te_ref, ts_ref, tv_ref, xs_ref, wg_ref, wu_ref, wd_ref, ys_ref, wgu_bf, wd_bf):
    del ts_ref
    t = pl.program_id(0)
    valid = tv_ref[t] == 1
    f = wg_ref.shape[-1]

    @pl.when(jnp.logical_or(t == 0, te_ref[t] != te_ref[jnp.maximum(t - 1, 0)]))
    def _():
        wgu_bf[:, :f] = wg_ref[0, 0].astype(BF16)
        wgu_bf[:, f:] = wu_ref[0, 0].astype(BF16)
        wd_bf[...] = wd_ref[0, 0].astype(BF16)

    @pl.when(valid)
    def _():
        y = _ffn_rows(_load_row_tiles(xs_ref), wgu_bf[...], wd_bf[...])
        _store_row_tiles(ys_ref, _pack_bf16_pairs(y))

    @pl.when(jnp.logical_not(valid))
    def _():
        ys_ref[...] = jnp.zeros_like(ys_ref)


def _experts(tile_expert, tile_src, tile_valid, xs, w_gate, w_up, w_down, layer, max_tiles):
    slots = xs.shape[0]
    row_tile = xs.shape[1:]
    tm = EXPERT_TILE
    d, f = w_gate.shape[2], w_gate.shape[3]
    grid_spec = pltpu.PrefetchScalarGridSpec(
        num_scalar_prefetch=3,
        grid=(max_tiles,),
        in_specs=[
            pl.BlockSpec((tm,) + row_tile, lambda t, te, ts, tv: (ts[t], 0, 0)),
            pl.BlockSpec((1, 1, d, f), lambda t, te, ts, tv: (layer, te[t], 0, 0)),
            pl.BlockSpec((1, 1, d, f), lambda t, te, ts, tv: (layer, te[t], 0, 0)),
            pl.BlockSpec((1, 1, f, d), lambda t, te, ts, tv: (layer, te[t], 0, 0)),
        ],
        out_specs=pl.BlockSpec((tm,) + row_tile, lambda t, te, ts, tv: (t, 0, 0)),
        scratch_shapes=[pltpu.VMEM((d, 2 * f), BF16), pltpu.VMEM((f, d), BF16)],
    )
    return pl.pallas_call(
        _experts_body,
        grid_spec=grid_spec,
        out_shape=jax.ShapeDtypeStruct((slots,) + row_tile, U32),
        compiler_params=_cparams("arbitrary"),
        name="moe_experts",
    )(tile_expert, tile_src, tile_valid, xs, w_gate, w_up, w_down)


def _combine_body(pos_ref, wt_ref, hp_ref, ys_ref, wgu_ref, wd_ref, x_ref, g_ref, gt_ref, o_ref,
                  gbuf, sem):
    tm = hp_ref.shape[0]

    def issue(r, carry):
        for k in range(TOP_K):
            pltpu.make_async_copy(ys_ref.at[pos_ref[k, r]], gbuf.at[k, r], sem).start(
                priority=k % 2)
        return carry

    lax.fori_loop(0, tm, issue, 0)
    y = _ffn_rows(_load_row_tiles(hp_ref), wgu_ref[...], wd_ref[...])
    all_rows = ys_ref.at[pl.ds(0, TOP_K * tm)]
    pltpu.make_async_copy(all_rows, all_rows, sem).wait()
    half = y.shape[1] // 2
    lo, hi = y[:, :half], y[:, half:]
    wt = wt_ref[...]
    for k in range(TOP_K):
        wk = wt[:, k:k + 1]
        gk = _load_row_tiles(gbuf.at[k])
        lo = lo + wk * _unpack_lo(gk)
        hi = hi + wk * _unpack_hi(gk)
    y = jnp.concatenate([lo, hi], axis=1)
    o_ref[...] = x_ref[...] + gt_ref[0] * _rms(y, g_ref[...])


def _combine(pos, wts, hp, ys, w_sgu, w_sd, xa, g, mod3, gt_idx, rows, seq, n_batch, in_place):
    d = xa.shape[1]
    row_tile = hp.shape[1:]
    tm = SEQ_TILE
    tps = seq // tm
    grp = lambda i: _group_of_tile(i, tps, n_batch)
    return pl.pallas_call(
        _combine_body,
        grid=(rows // tm,),
        in_specs=[
            pl.BlockSpec((TOP_K, tm), lambda i: (0, i), memory_space=pltpu.SMEM),
            pl.BlockSpec((tm, TOP_K), lambda i: (i, 0)),
            pl.BlockSpec((tm,) + row_tile, lambda i: (i, 0, 0)),
            pl.BlockSpec(memory_space=pl.ANY),
            pl.BlockSpec(w_sgu.shape, lambda i: (0, 0), pipeline_mode=pl.Buffered(1)),
            pl.BlockSpec(w_sd.shape, lambda i: (0, 0), pipeline_mode=pl.Buffered(1)),
            pl.BlockSpec((tm, d), lambda i: (i, 0)),
            pl.BlockSpec((1, d), lambda i: (0, 0)),
            pl.BlockSpec((1, 1, d), lambda i: (grp(i), 0, gt_idx)),
        ],
        out_specs=pl.BlockSpec((tm, d), lambda i: (i, 0)),
        out_shape=jax.ShapeDtypeStruct(xa.shape if in_place else (rows, d), F32),
        scratch_shapes=[pltpu.VMEM((TOP_K, tm) + row_tile, U32), pltpu.SemaphoreType.DMA(())],
        input_output_aliases={6: 0} if in_place else {},
        compiler_params=_cparams("arbitrary"),
        name="moe_combine",
    )(pos, wts, hp, ys, w_sgu, w_sd, xa, g, mod3)


def kernel(x, c, ctx, c_ctx, w_ada, b_ada, norm_g, w_in, diff_lambda, g_subln, sg_ln_g, sg_ln_b, sg_w, sg_b, conv_w, w_branch, w_out, w_router, router_bias, w_exp_gate, w_exp_up, w_exp_down, w_sh_gate, w_sh_up, w_sh_down):
    n_batch, seq, d = x.shape
    ctx_len = ctx.shape[1]
    depth = w_ada.shape[0]
    mix_w = w_branch.shape[2]
    n_heads = mix_w // HEAD_W
    n_exp = w_router.shape[2]
    n_lat = n_batch * seq
    n_ctx = n_batch * ctx_len
    n_all = n_lat + n_ctx
    assert n_batch + 1 <= MOD_ROWS and seq % SEQ_TILE == 0 and ctx_len % SEQ_TILE == 0
    assert n_lat % seq == 0 and n_lat % ctx_len == 0 and mix_w // SG_GROUPS == LANES
    assert d // 2 == SUBLANES * LANES

    xa = _concat_rows(x.reshape(n_lat, d), ctx.reshape(n_ctx, d))
    cond = jnp.zeros((MOD_ROWS, d), F32).at[:n_batch].set(c).at[n_batch].set(c_ctx)
    mod = _ada(cond, w_ada, b_ada)
    cos, sin = _rope_tables(seq)
    q_cb, k_cb, v_cb = 0, n_heads, 2 * n_heads
    gate_cb = (8 * mix_w) // d

    w_in_b = w_in.astype(BF16)
    w_branch_b = w_branch.astype(BF16)
    w_out_b = w_out.astype(BF16)
    n_in = w_in.shape[2]

    for l in range(depth):
        last = l == depth - 1
        lam_init = 0.8 - 0.6 * math.exp(-0.3 * l)
        mod3 = mod[l].reshape(MOD_ROWS, 1, N_MOD * d)
        ng = norm_g[l]
        rows = n_lat if last else n_all

        h = _normmod(xa, ng[0:1], mod3, 1, 0, n_all, seq, n_batch)
        if last:
            y = _matmul(h, w_in_b, l, n_lat, 0, n_in, 0, BF16, "in_proj")
            yc = _matmul(h, w_in_b, l, n_ctx, n_lat, 2 * mix_w, mix_w, BF16, "in_proj_ctx_kv")
            ctx_kv = (yc, 0, yc, n_heads, 0, ctx_len)
        else:
            y = _matmul(h, w_in_b, l, n_all, 0, n_in, 0, BF16, "in_proj")
            ctx_kv = (y, k_cb, y, v_cb, n_lat // ctx_len, ctx_len)
        qk = _rope(y, cos, sin, n_lat, seq, 2 * mix_w)
        attn_o = _attention(qk, q_cb, 0, qk, k_cb, y, v_cb, 0, seq, seq, n_batch, n_heads,
                            diff_lambda[l], g_subln[l:l + 1], lam_init, ctx=ctx_kv, name="attn_latent")
        if not last:
            attn_c = _attention(y, q_cb, n_lat // ctx_len, y, k_cb, y, v_cb, n_lat // ctx_len,
                                ctx_len, ctx_len, n_batch, n_heads,
                                diff_lambda[l], g_subln[l:l + 1], lam_init, name="attn_ctx")
            attn_o = jnp.concatenate([attn_o, attn_c], axis=0)
        sg_bias_map = jnp.repeat(sg_b[l].T, mix_w // SG_GROUPS, axis=1)
        sg_o, cv_o = _branches(y, rows, n_lat, seq, mix_w, 3, sg_ln_g[l:l + 1], sg_ln_b[l:l + 1],
                               sg_w[l].astype(BF16), sg_bias_map, conv_w[l])
        merged = _merge(attn_o, sg_o, cv_o, y, gate_cb, w_branch_b, l, rows)
        xa = _outproj(merged, w_out_b, l, xa, ng[1:2], mod3, 2, rows, seq, n_batch)

        hp, wt_t, idx_t, rank_t, cnt = _route(xa, ng[2:3], mod3, 4, 3, w_router[l].T, router_bias[l],
                                              rows, seq, n_batch)
        pos, tile_expert, tile_src, tile_valid, max_tiles = _dispatch_plan(
            idx_t, rank_t, cnt[:, 0].astype(jnp.int32), EXPERT_TILE)
        xs = _scatter(pos, hp, max_tiles * EXPERT_TILE)
        ys = _experts(tile_expert, tile_src, tile_valid, xs, w_exp_gate, w_exp_up, w_exp_down, l, max_tiles)
        w_sgu = jnp.concatenate([w_sh_gate[l], w_sh_up[l]], axis=-1).astype(BF16)
        xa = _combine(pos, wt_t.T, hp, ys, w_sgu, w_sh_down[l].astype(BF16), xa, ng[3:4], mod3, 5,
                      rows, seq, n_batch, in_place=not last)

    return xa.reshape(n_batch, seq, d)
```

```python
import functools
import math

import jax
import jax.numpy as jnp
from jax import lax
from jax.experimental import pallas as pl
from jax.experimental.pallas import tpu as pltpu

GRID_W = 64
QK_DIM = 64
HEAD_W = 2 * QK_DIM
ROPE_BASE = 10000.0
SG_CHUNK = 128
SG_GROUPS = 8
TOP_K = 8
N_GROUPS = 8
TOPK_GROUPS = 4
ROUTED_SCALE = 2.5
NORM_EPS = 1e-6
N_MOD = 6

LANES = 128
SUBLANES = 8
VMEM_LIMIT = 56 * 1024 * 1024
SEQ_TILE = 256
EXPERT_TILE = 512
MOD_ROWS = 16

F32 = jnp.float32
BF16 = jnp.bfloat16
U32 = jnp.uint32
HIGHEST = lax.Precision.HIGHEST


def _cparams(*sem):
    return pltpu.CompilerParams(dimension_semantics=sem, vmem_limit_bytes=VMEM_LIMIT)


def _pick_tile(n, cands):
    for c in cands:
        if n % c == 0:
            return c
    raise ValueError(f"no tile for {n} in {cands}")


def _sigmoid(x):
    return 1.0 / (1.0 + jnp.exp(-x))


def _silu(x):
    return x * _sigmoid(x)


def _gelu_tanh(x):
    c = math.sqrt(2.0 / math.pi)
    return x * (0.5 * (1.0 + jnp.tanh(c * (x + 0.044715 * (x * x * x)))))


def _rms(x, g):
    return x * lax.rsqrt(jnp.mean(x * x, axis=-1, keepdims=True) + NORM_EPS) * g


def _pack_bf16_pairs(x):
    w = x.shape[1] // 2
    bits = pltpu.bitcast(x.astype(BF16).astype(F32), U32)
    return (bits[:, :w] >> 16) | bits[:, w:]


def _unpack_lo(w):
    return pltpu.bitcast(w << 16, F32)


def _unpack_hi(w):
    return pltpu.bitcast(w & jnp.uint32(0xFFFF0000), F32)


def _store_row_tiles(ref, packed):
    m, s, _ = ref.shape
    flat = ref.reshape(m * s, LANES)
    for j in range(s):
        flat[pl.ds(j, m, stride=s), :] = packed[:, j * LANES:(j + 1) * LANES]


def _load_row_tiles(ref):
    m, s, _ = ref.shape
    flat = ref.reshape(m * s, LANES)
    return jnp.concatenate([flat[pl.ds(j, m, stride=s), :] for j in range(s)], axis=1)


def _concat_body(a_ref, b_ref, o_ref, *, a_tiles):
    i = pl.program_id(0)

    @pl.when(i < a_tiles)
    def _():
        o_ref[...] = a_ref[...]

    @pl.when(i >= a_tiles)
    def _():
        o_ref[...] = b_ref[...]


def _concat_rows(a, b):
    d = a.shape[1]
    tm = _pick_tile(math.gcd(a.shape[0], b.shape[0]), (512, 256))
    na, nb = a.shape[0] // tm, b.shape[0] // tm
    return pl.pallas_call(
        functools.partial(_concat_body, a_tiles=na),
        grid=(na + nb,),
        in_specs=[
            pl.BlockSpec((tm, d), lambda i: (jnp.minimum(i, na - 1), 0)),
            pl.BlockSpec((tm, d), lambda i: (jnp.maximum(i - na, 0), 0)),
        ],
        out_specs=pl.BlockSpec((tm, d), lambda i: (i, 0)),
        out_shape=jax.ShapeDtypeStruct((a.shape[0] + b.shape[0], d), a.dtype),
        compiler_params=_cparams("parallel"),
        name="concat_rows",
    )(a, b)


def _ada_body(c_ref, w_ref, b_ref, o_ref):
    a = _silu(c_ref[...])
    o_ref[0] = jnp.dot(a, w_ref[0], precision=HIGHEST, preferred_element_type=F32) + b_ref[0]


def _ada(cond, w_ada, b_ada):
    depth, d, n = w_ada.shape
    tn = _pick_tile(n, (1024, 512, 256, 128))
    return pl.pallas_call(
        _ada_body,
        grid=(depth, n // tn),
        in_specs=[
            pl.BlockSpec((MOD_ROWS, d), lambda l, j: (0, 0)),
            pl.BlockSpec((1, d, tn), lambda l, j: (l, 0, j)),
            pl.BlockSpec((1, 1, tn), lambda l, j: (l, 0, j)),
        ],
        out_specs=pl.BlockSpec((1, MOD_ROWS, tn), lambda l, j: (l, 0, j)),
        out_shape=jax.ShapeDtypeStruct((depth, MOD_ROWS, n), F32),
        compiler_params=_cparams("arbitrary", "arbitrary"),
        name="ada_mod",
    )(cond, w_ada, b_ada.reshape(depth, 1, n))


def _normmod_body(x_ref, g_ref, sc_ref, sh_ref, o_ref):
    y = _rms(x_ref[...], g_ref[...])
    o_ref[...] = (y * (1.0 + sc_ref[0]) + sh_ref[0]).astype(o_ref.dtype)


def _group_of_tile(i, tiles_per_seq, n_batch):
    return jnp.minimum(i // tiles_per_seq, n_batch)


def _normmod(xa, g, mod3, sc_idx, sh_idx, rows, seq, n_batch):
    d = xa.shape[1]
    tm = _pick_tile(seq, (512, 256))
    tps = seq // tm
    grp = lambda i: _group_of_tile(i, tps, n_batch)
    return pl.pallas_call(
        _normmod_body,
        grid=(rows // tm,),
        in_specs=[
            pl.BlockSpec((tm, d), lambda i: (i, 0)),
            pl.BlockSpec((1, d), lambda i: (0, 0)),
            pl.BlockSpec((1, 1, d), lambda i: (grp(i), 0, sc_idx)),
            pl.BlockSpec((1, 1, d), lambda i: (grp(i), 0, sh_idx)),
        ],
        out_specs=pl.BlockSpec((tm, d), lambda i: (i, 0)),
        out_shape=jax.ShapeDtypeStruct((rows, d), BF16),
        compiler_params=_cparams("parallel"),
        name="norm_mod",
    )(xa, g, mod3, mod3)


def _mm_body(a_ref, w_ref, o_ref):
    o_ref[...] = jnp.dot(a_ref[...], w_ref[...], preferred_element_type=F32).astype(o_ref.dtype)


def _matmul(a, w, layer, rows, row_off, cols, col_off, out_dtype, name):
    k = a.shape[1]
    tm = _pick_tile(math.gcd(rows, row_off) if row_off else rows, (1024, 512, 256))
    tn = _pick_tile(math.gcd(cols, col_off) if col_off else cols, (1024, 512, 256))
    ro, co = row_off // tm, col_off // tn
    return pl.pallas_call(
        _mm_body,
        grid=(rows // tm, cols // tn),
        in_specs=[
            pl.BlockSpec((tm, k), lambda i, j: (i + ro, 0)),
            pl.BlockSpec((None, k, tn), lambda i, j: (layer, 0, j + co)),
        ],
        out_specs=pl.BlockSpec((tm, tn), lambda i, j: (i, j)),
        out_shape=jax.ShapeDtypeStruct((rows, cols), out_dtype),
        compiler_params=_cparams("parallel", "arbitrary"),
        name=name,
    )(a, w)


def _rope_body(y_ref, cos_ref, sin_ref, o_ref):
    cos = cos_ref[...]
    sin = sin_ref[...]
    lane = lax.broadcasted_iota(jnp.int32, cos.shape, 1)
    first_half = (lane % (QK_DIM // 2)) < (QK_DIM // 4)
    for j in range(y_ref.shape[1] // LANES):
        sl = slice(j * LANES, (j + 1) * LANES)
        x = y_ref[:, sl].astype(F32)
        partner = jnp.where(first_half,
                            pltpu.roll(x, LANES - QK_DIM // 4, 1),
                            pltpu.roll(x, QK_DIM // 4, 1))
        o_ref[:, sl] = (x * cos + partner * sin).astype(o_ref.dtype)


def _rope_tables(seq):
    n_freq = QK_DIM // 4
    inv = ROPE_BASE ** (-jnp.arange(n_freq, dtype=F32) / n_freq)
    t = jnp.arange(seq)
    row = (t // GRID_W).astype(F32)
    col = (t % GRID_W).astype(F32)
    ang = jnp.stack([row[:, None] * inv, col[:, None] * inv], axis=1)
    cos = jnp.cos(ang)
    sin = jnp.sin(ang)
    cos64 = jnp.stack([cos, cos], axis=2).reshape(seq, QK_DIM)
    sin64 = jnp.stack([-sin, sin], axis=2).reshape(seq, QK_DIM)
    reps = LANES // QK_DIM
    return jnp.tile(cos64, (1, reps)), jnp.tile(sin64, (1, reps))


def _rope(y, cos, sin, rows, seq, width):
    tm = SEQ_TILE
    tps = seq // tm
    return pl.pallas_call(
        _rope_body,
        grid=(rows // tm,),
        in_specs=[
            pl.BlockSpec((tm, width), lambda i: (i, 0)),
            pl.BlockSpec((tm, LANES), lambda i: (i % tps, 0)),
            pl.BlockSpec((tm, LANES), lambda i: (i % tps, 0)),
        ],
        out_specs=pl.BlockSpec((tm, width), lambda i: (i, 0)),
        out_shape=jax.ShapeDtypeStruct((rows, width), BF16),
        compiler_params=_cparams("parallel"),
        name="rope_qk",
    )(y, cos, sin)


def _attn_body(*refs, two_seg, lam_init):
    if two_seg:
        q_ref, k_ref, v_ref, kc_ref, vc_ref, dl_ref, g_ref, o_ref, v1_ref, vc1_ref = refs
    else:
        q_ref, k_ref, v_ref, dl_ref, g_ref, o_ref, v1_ref = refs

    @pl.when(pl.program_id(2) == 0)
    def _():
        v1_ref[:, :HEAD_W] = v_ref[...]
        v1_ref[:, HEAD_W:] = jnp.ones(v_ref.shape, BF16)
        if two_seg:
            vc1_ref[:, :HEAD_W] = vc_ref[...]
            vc1_ref[:, HEAD_W:] = jnp.ones(vc_ref.shape, BF16)

    dl = dl_ref[...]
    lam = (jnp.exp(jnp.sum(dl[0:1] * dl[1:2], axis=-1, keepdims=True))
           - jnp.exp(jnp.sum(dl[2:3] * dl[3:4], axis=-1, keepdims=True)) + lam_init)
    q = q_ref[...] * jnp.asarray(QK_DIM ** -0.5, BF16)
    lane = lax.broadcasted_iota(jnp.int32, q.shape, 1)
    zero = jnp.zeros_like(q)
    nt = (((1,), (1,)), ((), ()))

    def softmax_v(qm):
        s = lax.dot_general(qm, k_ref[...], nt, preferred_element_type=F32)
        m = jnp.max(s, axis=-1, keepdims=True)
        if two_seg:
            sc = lax.dot_general(qm, kc_ref[...], nt, preferred_element_type=F32)
            m = jnp.maximum(m, jnp.max(sc, axis=-1, keepdims=True))
        acc = jnp.dot(jnp.exp(s - m).astype(BF16), v1_ref[...], preferred_element_type=F32)
        if two_seg:
            acc = acc + jnp.dot(jnp.exp(sc - m).astype(BF16), vc1_ref[...], preferred_element_type=F32)
        return acc[:, :HEAD_W] / acc[:, HEAD_W:]

    o = softmax_v(jnp.where(lane < QK_DIM, q, zero)) - lam * softmax_v(jnp.where(lane >= QK_DIM, q, zero))
    o_ref[...] = (_rms(o, g_ref[...]) * (1.0 - lam_init)).astype(o_ref.dtype)


def _attention(q_arr, q_cb, q_rb, k_arr, k_cb, v_arr, v_cb, kv_rb, seq_q, seq_k, n_batch, n_heads,
               dl, g_sub, lam_init, ctx=None, name="attn"):
    tq = _pick_tile(seq_q, (512, 256, 128))
    nq = seq_q // tq
    scratch = [pltpu.VMEM((seq_k, 2 * HEAD_W), BF16)]
    in_specs = [
        pl.BlockSpec((tq, HEAD_W), lambda b, h, i: ((q_rb + b) * nq + i, q_cb + h)),
        pl.BlockSpec((seq_k, HEAD_W), lambda b, h, i: (kv_rb + b, k_cb + h)),
        pl.BlockSpec((seq_k, HEAD_W), lambda b, h, i: (kv_rb + b, v_cb + h)),
    ]
    args = [q_arr, k_arr, v_arr]
    if ctx is not None:
        kc_arr, kc_cb, vc_arr, vc_cb, c_rb, seq_c = ctx
        in_specs += [
            pl.BlockSpec((seq_c, HEAD_W), lambda b, h, i: (c_rb + b, kc_cb + h)),
            pl.BlockSpec((seq_c, HEAD_W), lambda b, h, i: (c_rb + b, vc_cb + h)),
        ]
        args += [kc_arr, vc_arr]
        scratch.append(pltpu.VMEM((seq_c, 2 * HEAD_W), BF16))
    in_specs += [
        pl.BlockSpec(dl.shape, lambda b, h, i: (0, 0)),
        pl.BlockSpec((1, HEAD_W), lambda b, h, i: (0, 0)),
    ]
    args += [dl, g_sub]
    return pl.pallas_call(
        functools.partial(_attn_body, two_seg=ctx is not None, lam_init=lam_init),
        grid=(n_batch, n_heads, nq),
        in_specs=in_specs,
        out_specs=pl.BlockSpec((tq, HEAD_W), lambda b, h, i: (b * nq + i, h)),
        out_shape=jax.ShapeDtypeStruct((n_batch * seq_q, n_heads * HEAD_W), BF16),
        scratch_shapes=scratch,
        compiler_params=_cparams("arbitrary", "arbitrary", "arbitrary"),
        name=name,
    )(*args)


def _branches_body(u_ref, v_ref, x_ref, b_ref, c_ref, xp_ref, cp_ref, xn_ref, cn_ref,
                   lng_ref, lnb_ref, sgw_ref, sgb_ref, cw_ref, sg_o_ref, cv_o_ref,
                   *, lat_tiles, tiles_per_seq):
    i = pl.program_id(0)
    tm, w = u_ref.shape
    u = _gelu_tanh(u_ref[...].astype(F32))
    v = _gelu_tanh(v_ref[...].astype(F32))
    mu = jnp.mean(v, axis=-1, keepdims=True)
    dv = v - mu
    v = dv * lax.rsqrt(jnp.mean(dv * dv, axis=-1, keepdims=True) + NORM_EPS)
    v = (v * lng_ref[...] + lnb_ref[...]).astype(BF16)
    gw = w // SG_GROUPS
    for c in range(tm // SG_CHUNK):
        rs = slice(c * SG_CHUNK, (c + 1) * SG_CHUNK)
        for g in range(SG_GROUPS):
            cs = slice(g * gw, (g + 1) * gw)
            mixed = jnp.dot(sgw_ref[g], v[rs, cs], preferred_element_type=F32) + sgb_ref[:, cs]
            sg_o_ref[rs, cs] = (u[rs, cs] * mixed).astype(sg_o_ref.dtype)
    is_lat = i < lat_tiles
    seq_first = jnp.logical_or(jnp.logical_not(is_lat), i % tiles_per_seq == 0)
    seq_last = jnp.logical_or(jnp.logical_not(is_lat), i % tiles_per_seq == tiles_per_seq - 1)
    z = c_ref[...].astype(F32) * x_ref[...].astype(F32)
    hp = xp_ref.shape[0]
    z_prev = (cp_ref[...].astype(F32) * xp_ref[...].astype(F32))[hp - 1:hp, :]
    z_next = (cn_ref[...].astype(F32) * xn_ref[...].astype(F32))[0:1, :]
    z_prev = jnp.where(seq_first, 0.0, z_prev)
    z_next = jnp.where(seq_last, 0.0, z_next)
    row = lax.broadcasted_iota(jnp.int32, z.shape, 0)
    z_dn = jnp.where(row == 0, z_prev, pltpu.roll(z, 1, 0))
    z_up = jnp.where(row == tm - 1, z_next, pltpu.roll(z, tm - 1, 0))
    y = cw_ref[0:1, :] * z_dn + cw_ref[1:2, :] * z + cw_ref[2:3, :] * z_up
    cv_o_ref[...] = (b_ref[...].astype(F32) * y).astype(cv_o_ref.dtype)


def _branches(y, rows, lat_rows, seq, mix_w, col0, sg_ln_g, sg_ln_b, sg_w, sg_bias_map, conv_w):
    tm = SEQ_TILE
    halo = 16
    hb = tm // halo
    n_tiles = rows // tm
    last_hblk = rows // halo - 1
    cur = lambda c: pl.BlockSpec((tm, mix_w), lambda i: (i, col0 + c))
    prev = lambda c: pl.BlockSpec((halo, mix_w), lambda i: (jnp.maximum(i * hb - 1, 0), col0 + c))
    nxt = lambda c: pl.BlockSpec((halo, mix_w), lambda i: (jnp.minimum((i + 1) * hb, last_hblk), col0 + c))
    full = lambda a: pl.BlockSpec(a.shape, lambda i: (0,) * a.ndim)
    consts = [sg_ln_g, sg_ln_b, sg_w, sg_bias_map, conv_w]
    out = jax.ShapeDtypeStruct((rows, mix_w), BF16)
    return pl.pallas_call(
        functools.partial(_branches_body, lat_tiles=lat_rows // tm, tiles_per_seq=seq // tm),
        grid=(n_tiles,),
        in_specs=[cur(0), cur(1), cur(2), cur(3), cur(4), prev(2), prev(4), nxt(2), nxt(4)]
                 + [full(a) for a in consts],
        out_specs=[pl.BlockSpec((tm, mix_w), lambda i: (i, 0))] * 2,
        out_shape=[out, out],
        compiler_params=_cparams("parallel"),
        name="sg_conv_branches",
    )(y, y, y, y, y, y, y, y, y, *consts)


def _merge_body(a_ref, s_ref, c_ref, g0_ref, g1_ref, g2_ref, wb_ref, o_ref):
    acc = None
    for g, (br, gl) in enumerate(((a_ref, g0_ref), (s_ref, g1_ref), (c_ref, g2_ref))):
        t = _sigmoid(gl[...].astype(F32)) * jnp.dot(br[...], wb_ref[g], preferred_element_type=F32)
        acc = t if acc is None else acc + t
    o_ref[...] = acc.astype(o_ref.dtype)


def _merge(attn_o, sg_o, cv_o, y, gate_cb, w_branch, layer, rows):
    n_br, mix_w, d = w_branch.shape[1:]
    tm = _pick_tile(rows, (512, 256))
    br = pl.BlockSpec((tm, mix_w), lambda i: (i, 0))
    gate = lambda g: pl.BlockSpec((tm, d), lambda i: (i, gate_cb + g))
    return pl.pallas_call(
        _merge_body,
        grid=(rows // tm,),
        in_specs=[br, br, br, gate(0), gate(1), gate(2),
                  pl.BlockSpec((None, n_br, mix_w, d), lambda i: (layer, 0, 0, 0),
                               pipeline_mode=pl.Buffered(1))],
        out_specs=pl.BlockSpec((tm, d), lambda i: (i, 0)),
        out_shape=jax.ShapeDtypeStruct((rows, d), BF16),
        compiler_params=_cparams("parallel"),
        name="branch_merge",
    )(attn_o, sg_o, cv_o, y, y, y, w_branch)


def _outproj_body(m_ref, w_ref, x_ref, g_ref, gt_ref, o_ref):
    mix = jnp.dot(m_ref[...], w_ref[...], preferred_element_type=F32)
    o_ref[...] = x_ref[...] + gt_ref[0] * _rms(mix, g_ref[...])


def _outproj(merged, w_out, layer, xa, g, mod3, gt_idx, rows, seq, n_batch):
    d = xa.shape[1]
    tm = _pick_tile(seq, (512, 256))
    tps = seq // tm
    grp = lambda i: _group_of_tile(i, tps, n_batch)
    return pl.pallas_call(
        _outproj_body,
        grid=(rows // tm,),
        in_specs=[
            pl.BlockSpec((tm, d), lambda i: (i, 0)),
            pl.BlockSpec((None, d, d), lambda i: (layer, 0, 0), pipeline_mode=pl.Buffered(1)),
            pl.BlockSpec((tm, d), lambda i: (i, 0)),
            pl.BlockSpec((1, d), lambda i: (0, 0)),
            pl.BlockSpec((1, 1, d), lambda i: (grp(i), 0, gt_idx)),
        ],
        out_specs=pl.BlockSpec((tm, d), lambda i: (i, 0)),
        out_shape=jax.ShapeDtypeStruct(xa.shape, F32),
        input_output_aliases={2: 0},
        compiler_params=_cparams("parallel"),
        name="out_proj_residual",
    )(merged, w_out, xa, g, mod3)


def _top1_mask(vals, idx, n, axis):
    m = jnp.max(vals, axis=axis, keepdims=True)
    first = jnp.min(jnp.where(vals == m, idx, n), axis=axis, keepdims=True)
    return idx == first, m, first


def _route_body(x_ref, g_ref, sc_ref, sh_ref, wr_ref, rb_ref, hp_ref, wt_ref, ix_ref, rk_ref, cnt_ref):
    h2 = _rms(x_ref[...], g_ref[...]) * (1.0 + sc_ref[0]) + sh_ref[0]
    _store_row_tiles(hp_ref, _pack_bf16_pairs(h2))
    n_exp = wr_ref.shape[0]
    tm = x_ref.shape[0]
    per_group = n_exp // N_GROUPS
    logits = lax.dot_general(wr_ref[...], h2, (((1,), (1,)), ((), ())),
                             precision=HIGHEST, preferred_element_type=F32)
    scores = _sigmoid(logits)
    choice = scores + rb_ref[...]
    neg = -jnp.inf
    ji = lax.broadcasted_iota(jnp.int32, (per_group, tm), 0)
    groups, gs = [], []
    for g in range(N_GROUPS):
        cg = choice[g * per_group:(g + 1) * per_group, :]
        hit, m1, _ = _top1_mask(cg, ji, per_group, 0)
        m2 = jnp.max(jnp.where(hit, neg, cg), axis=0, keepdims=True)
        groups.append(cg)
        gs.append(m1 + m2)
    kept = []
    for g in range(N_GROUPS):
        beaten_by = jnp.zeros((1, tm), jnp.int32)
        for o in range(N_GROUPS):
            if o != g:
                beats = (gs[o] >= gs[g]) if o < g else (gs[o] > gs[g])
                beaten_by = beaten_by + beats.astype(jnp.int32)
        kept.append(jnp.where(beaten_by < TOPK_GROUPS, groups[g], neg))
    cm = jnp.concatenate(kept, axis=0)
    ei = lax.broadcasted_iota(jnp.int32, cm.shape, 0)
    ws, ids, hits = [], [], []
    sel = jnp.zeros(cm.shape, F32)
    for _ in range(TOP_K):
        hit, _, first = _top1_mask(cm, ei, n_exp, 0)
        ws.append(jnp.sum(jnp.where(hit, scores, 0.0), axis=0, keepdims=True))
        ids.append(first)
        hits.append(hit)
        sel = sel + jnp.where(hit, 1.0, 0.0)
        cm = jnp.where(hit, neg, cm)
    w = jnp.concatenate(ws, axis=0)
    wt_ref[...] = w / jnp.sum(w, axis=0, keepdims=True) * ROUTED_SCALE
    ix_ref[...] = jnp.concatenate(ids, axis=0)

    @pl.when(pl.program_id(0) == 0)
    def _():
        cnt_ref[...] = jnp.zeros_like(cnt_ref)

    before = (lax.broadcasted_iota(jnp.int32, (tm, tm), 0)
              < lax.broadcasted_iota(jnp.int32, (tm, tm), 1))
    within = jnp.dot(sel.astype(BF16), jnp.where(before, 1.0, 0.0).astype(BF16),
                     preferred_element_type=F32)
    rank = within + cnt_ref[:, 0:1]
    rk_ref[...] = jnp.concatenate(
        [jnp.sum(jnp.where(h, rank, 0.0), axis=0, keepdims=True) for h in hits], axis=0).astype(jnp.int32)
    cnt_ref[...] = cnt_ref[...] + jnp.sum(sel, axis=1, keepdims=True)


def _route(xa, g, mod3, sc_idx, sh_idx, w_router_t, router_bias, rows, seq, n_batch):
    d = xa.shape[1]
    n_exp = w_router_t.shape[0]
    tm = SEQ_TILE
    tps = seq // tm
    grp = lambda i: _group_of_tile(i, tps, n_batch)
    return pl.pallas_call(
        _route_body,
        grid=(rows // tm,),
        in_specs=[
            pl.BlockSpec((tm, d), lambda i: (i, 0)),
            pl.BlockSpec((1, d), lambda i: (0, 0)),
            pl.BlockSpec((1, 1, d), lambda i: (grp(i), 0, sc_idx)),
            pl.BlockSpec((1, 1, d), lambda i: (grp(i), 0, sh_idx)),
            pl.BlockSpec((n_exp, d), lambda i: (0, 0)),
            pl.BlockSpec((n_exp, 1), lambda i: (0, 0)),
        ],
        out_specs=[
            pl.BlockSpec((tm, SUBLANES, LANES), lambda i: (i, 0, 0)),
            pl.BlockSpec((TOP_K, tm), lambda i: (0, i)),
            pl.BlockSpec((TOP_K, tm), lambda i: (0, i)),
            pl.BlockSpec((TOP_K, tm), lambda i: (0, i)),
            pl.BlockSpec((n_exp, LANES), lambda i: (0, 0)),
        ],
        out_shape=[
            jax.ShapeDtypeStruct((rows, SUBLANES, LANES), U32),
            jax.ShapeDtypeStruct((TOP_K, rows), F32),
            jax.ShapeDtypeStruct((TOP_K, rows), jnp.int32),
            jax.ShapeDtypeStruct((TOP_K, rows), jnp.int32),
            jax.ShapeDtypeStruct((n_exp, LANES), F32),
        ],
        compiler_params=_cparams("arbitrary"),
        name="moe_norm_route",
    )(xa, g, mod3, mod3, w_router_t, router_bias.reshape(n_exp, 1))


def _dispatch_plan(idx_t, rank_t, counts, tile):
    n_exp = counts.shape[0]
    rows = idx_t.shape[1]
    tiles_e = (counts + tile - 1) // tile
    tile_end = jnp.cumsum(tiles_e)
    offs = (tile_end - tiles_e) * tile
    onehot = idx_t[None, :, :] == jnp.arange(n_exp, dtype=jnp.int32)[:, None, None]
    pos = rank_t + jnp.sum(jnp.where(onehot, offs[:, None, None], 0), axis=0)
    max_tiles = (TOP_K * rows) // tile + n_exp
    t = jnp.arange(max_tiles, dtype=jnp.int32)
    used = tile_end[-1]
    t_eff = jnp.minimum(t, used - 1)
    tile_expert = jnp.minimum(jnp.sum(t_eff[:, None] >= tile_end[None, :], axis=1), n_exp - 1)
    used_rows = used * tile
    pad = (jnp.append(offs + counts, used_rows).astype(jnp.int32),
           jnp.append(tiles_e * tile - counts, max_tiles * tile - used_rows).astype(jnp.int32))
    return pos.astype(jnp.int32), pad, tile_expert.astype(jnp.int32), t_eff.astype(jnp.int32), \
        (t < used).astype(jnp.int32), max_tiles


def _scatter_body(pad_start_ref, pad_len_ref, pos_ref, h_ref, xs_ref, zeros, sem, zsem):
    tm = h_ref.shape[0]
    n_exp = pad_start_ref.shape[0] - 1

    @pl.when(pl.program_id(0) == 0)
    def _():
        zeros[...] = jnp.zeros_like(zeros)
        half = zeros.shape[0]
        sizes = [half >> i for i in range(half.bit_length())]
        tail_start = pad_start_ref[n_exp]
        tail_halves = pad_len_ref[n_exp] // half

        def tail_copy(i):
            return pltpu.make_async_copy(zeros, xs_ref.at[pl.ds(tail_start + i * half, half)], zsem)

        lax.fori_loop(0, tail_halves, lambda i, c: (tail_copy(i).start(), c)[1], 0)
        lax.fori_loop(0, tail_halves, lambda i, c: (tail_copy(i).wait(), c)[1], 0)

        def pieces(e, wait):
            start = pad_start_ref[e]
            n = pad_len_ref[e]
            for size in sizes:
                has = (n & size) != 0
                copy = pltpu.make_async_copy(zeros.at[pl.ds(0, size)], xs_ref.at[pl.ds(start, size)], zsem)

                @pl.when(has)
                def _():
                    copy.wait() if wait else copy.start()

                start = start + jnp.where(has, size, 0)

        lax.fori_loop(0, n_exp, lambda e, c: (pieces(e, False), c)[1], 0)
        lax.fori_loop(0, n_exp, lambda e, c: (pieces(e, True), c)[1], 0)

    def issue(r, carry):
        for k in range(TOP_K):
            pltpu.make_async_copy(h_ref.at[r], xs_ref.at[pos_ref[k, r]], sem).start(
                priority=k % 2)
        return carry

    lax.fori_loop(0, tm, issue, 0)
    all_rows = xs_ref.at[pl.ds(0, TOP_K * tm)]
    pltpu.make_async_copy(all_rows, all_rows, sem).wait()


def _scatter(pad_start, pad_len, pos, hp, slots):
    rows = hp.shape[0]
    row_tile = hp.shape[1:]
    tm = SEQ_TILE
    assert EXPERT_TILE & (EXPERT_TILE - 1) == 0
    grid_spec = pltpu.PrefetchScalarGridSpec(
        num_scalar_prefetch=2,
        grid=(rows // tm,),
        in_specs=[
            pl.BlockSpec((TOP_K, tm), lambda i, ps, pn: (0, i), memory_space=pltpu.SMEM),
            pl.BlockSpec((tm,) + row_tile, lambda i, ps, pn: (i, 0, 0)),
        ],
        out_specs=pl.BlockSpec(memory_space=pl.ANY),
        scratch_shapes=[pltpu.VMEM((EXPERT_TILE // 2,) + row_tile, U32),
                        pltpu.SemaphoreType.DMA(()), pltpu.SemaphoreType.DMA(())],
    )
    return pl.pallas_call(
        _scatter_body,
        grid_spec=grid_spec,
        out_shape=jax.ShapeDtypeStruct((slots,) + row_tile, U32),
        compiler_params=_cparams("arbitrary"),
        name="moe_dispatch",
    )(pad_start, pad_len, pos, hp)


def _ffn_rows(xw, wgu_ref, wd_ref):
    x = jnp.concatenate([_unpack_lo(xw).astype(BF16), _unpack_hi(xw).astype(BF16)], axis=1)
    gu = jnp.dot(x, wgu_ref, preferred_element_type=F32)
    f = gu.shape[1] // 2
    hdn = (_silu(gu[:, :f]) * gu[:, f:]).astype(BF16)
    return jnp.dot(hdn, wd_ref, preferred_element_type=F32)


def _experts_body(te_ref, ts_ref, tv_ref, xs_ref, wg_ref, wu_ref, wd_ref, ys_ref, wgu_bf, wd_bf):
    del ts_ref
    t = pl.program_id(0)
    valid = tv_ref[t] == 1
    f = wg_ref.shape[-1]

    @pl.when(jnp.logical_or(t == 0, te_ref[t] != te_ref[jnp.maximum(t - 1, 0)]))
    def _():
        wgu_bf[:, :f] = wg_ref[0, 0].astype(BF16)
        wgu_bf[:, f:] = wu_ref[0, 0].astype(BF16)
        wd_bf[...] = wd_ref[0, 0].astype(BF16)

    @pl.when(valid)
    def _():
        y = _ffn_rows(_load_row_tiles(xs_ref), wgu_bf[...], wd_bf[...])
        _store_row_tiles(ys_ref, _pack_bf16_pairs(y))

    @pl.when(jnp.logical_not(valid))
    def _():
        ys_ref[...] = jnp.zeros_like(ys_ref)


def _experts(tile_expert, tile_src, tile_valid, xs, w_gate, w_up, w_down, layer, max_tiles):
    slots = xs.shape[0]
    row_tile = xs.shape[1:]
    tm = EXPERT_TILE
    d, f = w_gate.shape[2], w_gate.shape[3]
    grid_spec = pltpu.PrefetchScalarGridSpec(
        num_scalar_prefetch=3,
        grid=(max_tiles,),
        in_specs=[
            pl.BlockSpec((tm,) + row_tile, lambda t, te, ts, tv: (ts[t], 0, 0)),
            pl.BlockSpec((1, 1, d, f), lambda t, te, ts, tv: (layer, te[t], 0, 0)),
            pl.BlockSpec((1, 1, d, f), lambda t, te, ts, tv: (layer, te[t], 0, 0)),
            pl.BlockSpec((1, 1, f, d), lambda t, te, ts, tv: (layer, te[t], 0, 0)),
        ],
        out_specs=pl.BlockSpec((tm,) + row_tile, lambda t, te, ts, tv: (t, 0, 0)),
        scratch_shapes=[pltpu.VMEM((d, 2 * f), BF16), pltpu.VMEM((f, d), BF16)],
    )
    return pl.pallas_call(
        _experts_body,
        grid_spec=grid_spec,
        out_shape=jax.ShapeDtypeStruct((slots,) + row_tile, U32),
        compiler_params=_cparams("arbitrary"),
        name="moe_experts",
    )(tile_expert, tile_src, tile_valid, xs, w_gate, w_up, w_down)


def _combine_body(pos_ref, wt_ref, hp_ref, ys_ref, wgu_ref, wd_ref, x_ref, g_ref, gt_ref, o_ref,
                  gbuf, sem):
    tm = hp_ref.shape[0]

    def issue(r, carry):
        for k in range(TOP_K):
            pltpu.make_async_copy(ys_ref.at[pos_ref[k, r]], gbuf.at[k, r], sem).start(
                priority=k % 2)
        return carry

    lax.fori_loop(0, tm, issue, 0)
    y = _ffn_rows(_load_row_tiles(hp_ref), wgu_ref[...], wd_ref[...])
    all_rows = ys_ref.at[pl.ds(0, TOP_K * tm)]
    pltpu.make_async_copy(all_rows, all_rows, sem).wait()
    half = y.shape[1] // 2
    lo, hi = y[:, :half], y[:, half:]
    wt = wt_ref[...]
    for k in range(TOP_K):
        wk = wt[:, k:k + 1]
        gk = _load_row_tiles(gbuf.at[k])
        lo = lo + wk * _unpack_lo(gk)
        hi = hi + wk * _unpack_hi(gk)
    y = jnp.concatenate([lo, hi], axis=1)
    o_ref[...] = x_ref[...] + gt_ref[0] * _rms(y, g_ref[...])


def _combine(pos, wts, hp, ys, w_sgu, w_sd, xa, g, mod3, gt_idx, rows, seq, n_batch, in_place):
    d = xa.shape[1]
    row_tile = hp.shape[1:]
    tm = SEQ_TILE
    tps = seq // tm
    grp = lambda i: _group_of_tile(i, tps, n_batch)
    return pl.pallas_call(
        _combine_body,
        grid=(rows // tm,),
        in_specs=[
            pl.BlockSpec((TOP_K, tm), lambda i: (0, i), memory_space=pltpu.SMEM),
            pl.BlockSpec((tm, TOP_K), lambda i: (i, 0)),
            pl.BlockSpec((tm,) + row_tile, lambda i: (i, 0, 0)),
            pl.BlockSpec(memory_space=pl.ANY),
            pl.BlockSpec(w_sgu.shape, lambda i: (0, 0), pipeline_mode=pl.Buffered(1)),
            pl.BlockSpec(w_sd.shape, lambda i: (0, 0), pipeline_mode=pl.Buffered(1)),
            pl.BlockSpec((tm, d), lambda i: (i, 0)),
            pl.BlockSpec((1, d), lambda i: (0, 0)),
            pl.BlockSpec((1, 1, d), lambda i: (grp(i), 0, gt_idx)),
        ],
        out_specs=pl.BlockSpec((tm, d), lambda i: (i, 0)),
        out_shape=jax.ShapeDtypeStruct(xa.shape if in_place else (rows, d), F32),
        scratch_shapes=[pltpu.VMEM((TOP_K, tm) + row_tile, U32), pltpu.SemaphoreType.DMA(())],
        input_output_aliases={6: 0} if in_place else {},
        compiler_params=_cparams("arbitrary"),
        name="moe_combine",
    )(pos, wts, hp, ys, w_sgu, w_sd, xa, g, mod3)


def kernel(x, c, ctx, c_ctx, w_ada, b_ada, norm_g, w_in, diff_lambda, g_subln, sg_ln_g, sg_ln_b, sg_w, sg_b, conv_w, w_branch, w_out, w_router, router_bias, w_exp_gate, w_exp_up, w_exp_down, w_sh_gate, w_sh_up, w_sh_down):
    n_batch, seq, d = x.shape
    ctx_len = ctx.shape[1]
    depth = w_ada.shape[0]
    mix_w = w_branch.shape[2]
    n_heads = mix_w // HEAD_W
    n_exp = w_router.shape[2]
    n_lat = n_batch * seq
    n_ctx = n_batch * ctx_len
    n_all = n_lat + n_ctx
    assert n_batch + 1 <= MOD_ROWS and seq % SEQ_TILE == 0 and ctx_len % SEQ_TILE == 0
    assert n_lat % seq == 0 and n_lat % ctx_len == 0 and mix_w // SG_GROUPS == LANES
    assert d // 2 == SUBLANES * LANES

    xa = _concat_rows(x.reshape(n_lat, d), ctx.reshape(n_ctx, d))
    cond = jnp.zeros((MOD_ROWS, d), F32).at[:n_batch].set(c).at[n_batch].set(c_ctx)
    mod = _ada(cond, w_ada, b_ada)
    cos, sin = _rope_tables(seq)
    q_cb, k_cb, v_cb = 0, n_heads, 2 * n_heads
    gate_cb = (8 * mix_w) // d

    w_in_b = w_in.astype(BF16)
    w_branch_b = w_branch.astype(BF16)
    w_out_b = w_out.astype(BF16)
    n_in = w_in.shape[2]

    for l in range(depth):
        last = l == depth - 1
        lam_init = 0.8 - 0.6 * math.exp(-0.3 * l)
        mod3 = mod[l].reshape(MOD_ROWS, 1, N_MOD * d)
        ng = norm_g[l]
        rows = n_lat if last else n_all

        h = _normmod(xa, ng[0:1], mod3, 1, 0, n_all, seq, n_batch)
        if last:
            y = _matmul(h, w_in_b, l, n_lat, 0, n_in, 0, BF16, "in_proj")
            yc = _matmul(h, w_in_b, l, n_ctx, n_lat, 2 * mix_w, mix_w, BF16, "in_proj_ctx_kv")
            ctx_kv = (yc, 0, yc, n_heads, 0, ctx_len)
        else:
            y = _matmul(h, w_in_b, l, n_all, 0, n_in, 0, BF16, "in_proj")
            ctx_kv = (y, k_cb, y, v_cb, n_lat // ctx_len, ctx_len)
        qk = _rope(y, cos, sin, n_lat, seq, 2 * mix_w)
        attn_o = _attention(qk, q_cb, 0, qk, k_cb, y, v_cb, 0, seq, seq, n_batch, n_heads,
                            diff_lambda[l], g_subln[l:l + 1], lam_init, ctx=ctx_kv, name="attn_latent")
        if not last:
            attn_c = _attention(y, q_cb, n_lat // ctx_len, y, k_cb, y, v_cb, n_lat // ctx_len,
                                ctx_len, ctx_len, n_batch, n_heads,
                                diff_lambda[l], g_subln[l:l + 1], lam_init, name="attn_ctx")
            attn_o = jnp.concatenate([attn_o, attn_c], axis=0)
        sg_bias_map = jnp.repeat(sg_b[l].T, mix_w // SG_GROUPS, axis=1)
        sg_o, cv_o = _branches(y, rows, n_lat, seq, mix_w, 3, sg_ln_g[l:l + 1], sg_ln_b[l:l + 1],
                               sg_w[l].astype(BF16), sg_bias_map, conv_w[l])
        merged = _merge(attn_o, sg_o, cv_o, y, gate_cb, w_branch_b, l, rows)
        xa = _outproj(merged, w_out_b, l, xa, ng[1:2], mod3, 2, rows, seq, n_batch)

        hp, wt_t, idx_t, rank_t, cnt = _route(xa, ng[2:3], mod3, 4, 3, w_router[l].T, router_bias[l],
                                              rows, seq, n_batch)
        pos, pad, tile_expert, tile_src, tile_valid, max_tiles = _dispatch_plan(
            idx_t, rank_t, cnt[:, 0].astype(jnp.int32), EXPERT_TILE)
        xs = _scatter(*pad, pos, hp, max_tiles * EXPERT_TILE)
        ys = _experts(tile_expert, tile_src, tile_valid, xs, w_exp_gate, w_exp_up, w_exp_down, l, max_tiles)
        w_sgu = jnp.concatenate([w_sh_gate[l], w_sh_up[l]], axis=-1).astype(BF16)
        xa = _combine(pos, wt_t.T, hp, ys, w_sgu, w_sh_down[l].astype(BF16), xa, ng[3:4], mod3, 5,
                      rows, seq, n_batch, in_place=not last)

    return xa.reshape(n_batch, seq, d)
```

```python
import functools
import math

import jax
import jax.numpy as jnp
from jax import lax
from jax.experimental import pallas as pl
from jax.experimental.pallas import tpu as pltpu

GRID_W = 64
QK_DIM = 64
HEAD_W = 2 * QK_DIM
ROPE_BASE = 10000.0
SG_CHUNK = 128
SG_GROUPS = 8
TOP_K = 8
N_GROUPS = 8
TOPK_GROUPS = 4
ROUTED_SCALE = 2.5
NORM_EPS = 1e-6
N_MOD = 6

LANES = 128
SUBLANES = 8
VMEM_LIMIT = 56 * 1024 * 1024
SEQ_TILE = 256
EXPERT_TILE = 512
MOD_ROWS = 16
ATTN_SUB_ROWS = 128

F32 = jnp.float32
BF16 = jnp.bfloat16
U32 = jnp.uint32
HIGHEST = lax.Precision.HIGHEST


def _cparams(*sem):
    return pltpu.CompilerParams(dimension_semantics=sem, vmem_limit_bytes=VMEM_LIMIT)


def _pick_tile(n, cands):
    for c in cands:
        if n % c == 0:
            return c
    raise ValueError(f"no tile for {n} in {cands}")


def _sigmoid(x):
    return 1.0 / (1.0 + jnp.exp(-x))


def _silu(x):
    return x * _sigmoid(x)


def _gelu_tanh(x):
    c = math.sqrt(2.0 / math.pi)
    return x * (0.5 * (1.0 + jnp.tanh(c * (x + 0.044715 * (x * x * x)))))


def _rms(x, g):
    return x * lax.rsqrt(jnp.mean(x * x, axis=-1, keepdims=True) + NORM_EPS) * g


def _pack_bf16_pairs(x):
    w = x.shape[1] // 2
    bits = pltpu.bitcast(x.astype(BF16).astype(F32), U32)
    return (bits[:, :w] >> 16) | bits[:, w:]


def _unpack_lo(w):
    return pltpu.bitcast(w << 16, F32)


def _unpack_hi(w):
    return pltpu.bitcast(w & jnp.uint32(0xFFFF0000), F32)


def _store_row_tiles(ref, packed):
    m, s, _ = ref.shape
    flat = ref.reshape(m * s, LANES)
    for j in range(s):
        flat[pl.ds(j, m, stride=s), :] = packed[:, j * LANES:(j + 1) * LANES]


def _load_row_tiles(ref):
    m, s, _ = ref.shape
    flat = ref.reshape(m * s, LANES)
    return jnp.concatenate([flat[pl.ds(j, m, stride=s), :] for j in range(s)], axis=1)


def _concat_body(a_ref, b_ref, o_ref, *, a_tiles):
    i = pl.program_id(0)

    @pl.when(i < a_tiles)
    def _():
        o_ref[...] = a_ref[...]

    @pl.when(i >= a_tiles)
    def _():
        o_ref[...] = b_ref[...]


def _concat_rows(a, b):
    d = a.shape[1]
    tm = _pick_tile(math.gcd(a.shape[0], b.shape[0]), (512, 256))
    na, nb = a.shape[0] // tm, b.shape[0] // tm
    return pl.pallas_call(
        functools.partial(_concat_body, a_tiles=na),
        grid=(na + nb,),
        in_specs=[
            pl.BlockSpec((tm, d), lambda i: (jnp.minimum(i, na - 1), 0)),
            pl.BlockSpec((tm, d), lambda i: (jnp.maximum(i - na, 0), 0)),
        ],
        out_specs=pl.BlockSpec((tm, d), lambda i: (i, 0)),
        out_shape=jax.ShapeDtypeStruct((a.shape[0] + b.shape[0], d), a.dtype),
        compiler_params=_cparams("parallel"),
        name="concat_rows",
    )(a, b)


def _ada_body(c_ref, w_ref, b_ref, o_ref):
    a = _silu(c_ref[...])
    o_ref[0] = jnp.dot(a, w_ref[0], precision=HIGHEST, preferred_element_type=F32) + b_ref[0]


def _ada(cond, w_ada, b_ada):
    depth, d, n = w_ada.shape
    tn = _pick_tile(n, (1024, 512, 256, 128))
    return pl.pallas_call(
        _ada_body,
        grid=(depth, n // tn),
        in_specs=[
            pl.BlockSpec((MOD_ROWS, d), lambda l, j: (0, 0)),
            pl.BlockSpec((1, d, tn), lambda l, j: (l, 0, j)),
            pl.BlockSpec((1, 1, tn), lambda l, j: (l, 0, j)),
        ],
        out_specs=pl.BlockSpec((1, MOD_ROWS, tn), lambda l, j: (l, 0, j)),
        out_shape=jax.ShapeDtypeStruct((depth, MOD_ROWS, n), F32),
        compiler_params=_cparams("arbitrary", "arbitrary"),
        name="ada_mod",
    )(cond, w_ada, b_ada.reshape(depth, 1, n))


def _group_of_tile(i, tiles_per_seq, n_batch):
    return jnp.minimum(i // tiles_per_seq, n_batch)


def _norm_mm_body(x_ref, g_ref, sc_ref, sh_ref, w_ref, o_ref, h_ref):
    @pl.when(pl.program_id(1) == 0)
    def _():
        y = _rms(x_ref[...], g_ref[...])
        h_ref[...] = (y * (1.0 + sc_ref[0]) + sh_ref[0]).astype(h_ref.dtype)

    o_ref[...] = jnp.dot(h_ref[...], w_ref[...], preferred_element_type=F32).astype(o_ref.dtype)


def _norm_matmul(xa, g, mod3, sc_idx, sh_idx, w, layer, rows, row_off, cols, col_off, seq, n_batch, name):
    d = xa.shape[1]
    tm = _pick_tile(math.gcd(math.gcd(rows, row_off) if row_off else rows, seq), (1024, 512, 256))
    tn = _pick_tile(math.gcd(cols, col_off) if col_off else cols, (1024, 512, 256))
    ro, co = row_off // tm, col_off // tn
    tps = seq // tm
    grp = lambda i: _group_of_tile(i + ro, tps, n_batch)
    return pl.pallas_call(
        _norm_mm_body,
        grid=(rows // tm, cols // tn),
        in_specs=[
            pl.BlockSpec((tm, d), lambda i, j: (i + ro, 0)),
            pl.BlockSpec((1, d), lambda i, j: (0, 0)),
            pl.BlockSpec((1, 1, d), lambda i, j: (grp(i), 0, sc_idx)),
            pl.BlockSpec((1, 1, d), lambda i, j: (grp(i), 0, sh_idx)),
            pl.BlockSpec((None, d, tn), lambda i, j: (layer, 0, j + co)),
        ],
        out_specs=pl.BlockSpec((tm, tn), lambda i, j: (i, j)),
        out_shape=jax.ShapeDtypeStruct((rows, cols), BF16),
        scratch_shapes=[pltpu.VMEM((tm, d), BF16)],
        compiler_params=_cparams("arbitrary", "arbitrary"),
        name=name,
    )(xa, g, mod3, mod3, w)


def _rope_apply(x, cos, sin):
    lane = lax.broadcasted_iota(jnp.int32, x.shape, 1)
    first_half = (lane % (QK_DIM // 2)) < (QK_DIM // 4)
    partner = jnp.where(first_half,
                        pltpu.roll(x, LANES - QK_DIM // 4, 1),
                        pltpu.roll(x, QK_DIM // 4, 1))
    return x * cos + partner * sin


def _rope_tables(seq):
    n_freq = QK_DIM // 4
    inv = ROPE_BASE ** (-jnp.arange(n_freq, dtype=F32) / n_freq)
    t = jnp.arange(seq)
    row = (t // GRID_W).astype(F32)
    col = (t % GRID_W).astype(F32)
    ang = jnp.stack([row[:, None] * inv, col[:, None] * inv], axis=1)
    cos = jnp.cos(ang)
    sin = jnp.sin(ang)
    cos64 = jnp.stack([cos, cos], axis=2).reshape(seq, QK_DIM)
    sin64 = jnp.stack([-sin, sin], axis=2).reshape(seq, QK_DIM)
    reps = LANES // QK_DIM
    return jnp.tile(cos64, (1, reps)), jnp.tile(sin64, (1, reps))


def _attn_body(*refs, two_seg, rope, lam_init):
    refs = list(refs)
    q_ref, k_ref, v_ref = refs[:3]
    del refs[:3]
    segs = []
    if two_seg:
        segs.append((refs[0], refs[1], False))
        del refs[:2]
    segs.append((k_ref, v_ref, rope))
    if rope:
        cos_ref, sin_ref = refs[:2]
        del refs[:2]
    dl_ref, g_ref, o_ref, kt_ref, v1_ref = refs
    tq = q_ref.shape[0]

    @pl.when(pl.program_id(2) == 0)
    def _():
        row = 0
        for ks_ref, vs_ref, rotate in segs:
            n = ks_ref.shape[0]
            ks = ks_ref[...].astype(F32)
            if rotate:
                ks = _rope_apply(ks, cos_ref[...], sin_ref[...])
            kt_ref[:, row:row + n] = ks.T.astype(BF16)
            v1_ref[row:row + n, :HEAD_W] = vs_ref[...]
            v1_ref[row:row + n, HEAD_W:] = jnp.ones(vs_ref.shape, BF16)
            row += n

    dl = dl_ref[...]
    lam = (jnp.exp(jnp.sum(dl[0:1] * dl[1:2], axis=-1, keepdims=True))
           - jnp.exp(jnp.sum(dl[2:3] * dl[3:4], axis=-1, keepdims=True)) + lam_init)
    q = q_ref[...]
    if rope:
        q_rows = pl.ds(pl.multiple_of(pl.program_id(2) * tq, tq), tq)
        q = _rope_apply(q.astype(F32), cos_ref[q_rows, :], sin_ref[q_rows, :]).astype(BF16)
    q = q * jnp.asarray(QK_DIM ** -0.5, BF16)
    lane = lax.broadcasted_iota(jnp.int32, q.shape, 1)
    zero = jnp.zeros_like(q)

    def softmax_v(qm):
        s = jnp.dot(qm, kt_ref[...], preferred_element_type=F32)
        m = jnp.max(s, axis=-1, keepdims=True)
        acc = jnp.dot(jnp.exp(s - m).astype(BF16), v1_ref[...], preferred_element_type=F32)
        return acc[:, :HEAD_W] / acc[:, HEAD_W:]

    q1 = jnp.where(lane < QK_DIM, q, zero)
    q2 = jnp.where(lane >= QK_DIM, q, zero)
    sub = min(q.shape[0], ATTN_SUB_ROWS)
    for r0 in range(0, q.shape[0], sub):
        rs = slice(r0, r0 + sub)
        o = softmax_v(q1[rs]) - lam * softmax_v(q2[rs])
        o_ref[rs, :] = (_rms(o, g_ref[...]) * (1.0 - lam_init)).astype(o_ref.dtype)


def _attention(q_arr, q_cb, q_rb, k_arr, k_cb, v_arr, v_cb, kv_rb, seq_q, seq_k, n_batch, n_heads,
               dl, g_sub, lam_init, ctx=None, rope=None, name="attn"):
    tq = _pick_tile(seq_q, (1024, 512, 256, 128))
    nq = seq_q // tq
    n_keys = seq_k + (ctx[5] if ctx is not None else 0)
    scratch = [pltpu.VMEM((HEAD_W, n_keys), BF16), pltpu.VMEM((n_keys, 2 * HEAD_W), BF16)]
    in_specs = [
        pl.BlockSpec((tq, HEAD_W), lambda b, h, i: ((q_rb + b) * nq + i, q_cb + h)),
        pl.BlockSpec((seq_k, HEAD_W), lambda b, h, i: (kv_rb + b, k_cb + h)),
        pl.BlockSpec((seq_k, HEAD_W), lambda b, h, i: (kv_rb + b, v_cb + h)),
    ]
    args = [q_arr, k_arr, v_arr]
    if ctx is not None:
        kc_arr, kc_cb, vc_arr, vc_cb, c_rb, seq_c = ctx
        in_specs += [
            pl.BlockSpec((seq_c, HEAD_W), lambda b, h, i: (c_rb + b, kc_cb + h)),
            pl.BlockSpec((seq_c, HEAD_W), lambda b, h, i: (c_rb + b, vc_cb + h)),
        ]
        args += [kc_arr, vc_arr]
    if rope is not None:
        assert seq_q == seq_k
        in_specs += [pl.BlockSpec((seq_k, LANES), lambda b, h, i: (0, 0))] * 2
        args += list(rope)
    in_specs += [
        pl.BlockSpec(dl.shape, lambda b, h, i: (0, 0)),
        pl.BlockSpec((1, HEAD_W), lambda b, h, i: (0, 0)),
    ]
    args += [dl, g_sub]
    return pl.pallas_call(
        functools.partial(_attn_body, two_seg=ctx is not None, rope=rope is not None, lam_init=lam_init),
        grid=(n_batch, n_heads, nq),
        in_specs=in_specs,
        out_specs=pl.BlockSpec((tq, HEAD_W), lambda b, h, i: (b * nq + i, h)),
        out_shape=jax.ShapeDtypeStruct((n_batch * seq_q, n_heads * HEAD_W), BF16),
        scratch_shapes=scratch,
        compiler_params=_cparams("arbitrary", "arbitrary", "arbitrary"),
        name=name,
    )(*args)


def _branches_body(u_ref, v_ref, x_ref, b_ref, c_ref, xp_ref, cp_ref, xn_ref, cn_ref,
                   lng_ref, lnb_ref, sgw_ref, sgb_ref, cw_ref, sg_o_ref, cv_o_ref,
                   *, lat_tiles, tiles_per_seq):
    i = pl.program_id(0)
    tm, w = u_ref.shape
    u = _gelu_tanh(u_ref[...].astype(F32))
    v = _gelu_tanh(v_ref[...].astype(F32))
    mu = jnp.mean(v, axis=-1, keepdims=True)
    dv = v - mu
    v = dv * lax.rsqrt(jnp.mean(dv * dv, axis=-1, keepdims=True) + NORM_EPS)
    v = (v * lng_ref[...] + lnb_ref[...]).astype(BF16)
    gw = w // SG_GROUPS
    for c in range(tm // SG_CHUNK):
        rs = slice(c * SG_CHUNK, (c + 1) * SG_CHUNK)
        for g in range(SG_GROUPS):
            cs = slice(g * gw, (g + 1) * gw)
            mixed = jnp.dot(sgw_ref[g], v[rs, cs], preferred_element_type=F32) + sgb_ref[:, cs]
            sg_o_ref[rs, cs] = (u[rs, cs] * mixed).astype(sg_o_ref.dtype)
    is_lat = i < lat_tiles
    seq_first = jnp.logical_or(jnp.logical_not(is_lat), i % tiles_per_seq == 0)
    seq_last = jnp.logical_or(jnp.logical_not(is_lat), i % tiles_per_seq == tiles_per_seq - 1)
    z = c_ref[...].astype(F32) * x_ref[...].astype(F32)
    hp = xp_ref.shape[0]
    z_prev = (cp_ref[...].astype(F32) * xp_ref[...].astype(F32))[hp - 1:hp, :]
    z_next = (cn_ref[...].astype(F32) * xn_ref[...].astype(F32))[0:1, :]
    z_prev = jnp.where(seq_first, 0.0, z_prev)
    z_next = jnp.where(seq_last, 0.0, z_next)
    row = lax.broadcasted_iota(jnp.int32, z.shape, 0)
    z_dn = jnp.where(row == 0, z_prev, pltpu.roll(z, 1, 0))
    z_up = jnp.where(row == tm - 1, z_next, pltpu.roll(z, tm - 1, 0))
    y = cw_ref[0:1, :] * z_dn + cw_ref[1:2, :] * z + cw_ref[2:3, :] * z_up
    cv_o_ref[...] = (b_ref[...].astype(F32) * y).astype(cv_o_ref.dtype)


def _branches(y, rows, lat_rows, seq, mix_w, col0, sg_ln_g, sg_ln_b, sg_w, sg_bias_map, conv_w):
    tm = SEQ_TILE
    halo = 16
    hb = tm // halo
    n_tiles = rows // tm
    last_hblk = rows // halo - 1
    cur = lambda c: pl.BlockSpec((tm, mix_w), lambda i: (i, col0 + c))
    prev = lambda c: pl.BlockSpec((halo, mix_w), lambda i: (jnp.maximum(i * hb - 1, 0), col0 + c))
    nxt = lambda c: pl.BlockSpec((halo, mix_w), lambda i: (jnp.minimum((i + 1) * hb, last_hblk), col0 + c))
    full = lambda a: pl.BlockSpec(a.shape, lambda i: (0,) * a.ndim)
    consts = [sg_ln_g, sg_ln_b, sg_w, sg_bias_map, conv_w]
    out = jax.ShapeDtypeStruct((rows, mix_w), BF16)
    return pl.pallas_call(
        functools.partial(_branches_body, lat_tiles=lat_rows // tm, tiles_per_seq=seq // tm),
        grid=(n_tiles,),
        in_specs=[cur(0), cur(1), cur(2), cur(3), cur(4), prev(2), prev(4), nxt(2), nxt(4)]
                 + [full(a) for a in consts],
        out_specs=[pl.BlockSpec((tm, mix_w), lambda i: (i, 0))] * 2,
        out_shape=[out, out],
        compiler_params=_cparams("parallel"),
        name="sg_conv_branches",
    )(y, y, y, y, y, y, y, y, y, *consts)


def _merge_body(a_ref, s_ref, c_ref, g0_ref, g1_ref, g2_ref, wb_ref, o_ref):
    acc = None
    for g, (br, gl) in enumerate(((a_ref, g0_ref), (s_ref, g1_ref), (c_ref, g2_ref))):
        t = _sigmoid(gl[...].astype(F32)) * jnp.dot(br[...], wb_ref[g], preferred_element_type=F32)
        acc = t if acc is None else acc + t
    o_ref[...] = acc.astype(o_ref.dtype)


def _merge(attn_o, sg_o, cv_o, y, gate_cb, w_branch, layer, rows):
    n_br, mix_w, d = w_branch.shape[1:]
    tm = _pick_tile(rows, (512, 256))
    br = pl.BlockSpec((tm, mix_w), lambda i: (i, 0))
    gate = lambda g: pl.BlockSpec((tm, d), lambda i: (i, gate_cb + g))
    return pl.pallas_call(
        _merge_body,
        grid=(rows // tm,),
        in_specs=[br, br, br, gate(0), gate(1), gate(2),
                  pl.BlockSpec((None, n_br, mix_w, d), lambda i: (layer, 0, 0, 0),
                               pipeline_mode=pl.Buffered(1))],
        out_specs=pl.BlockSpec((tm, d), lambda i: (i, 0)),
        out_shape=jax.ShapeDtypeStruct((rows, d), BF16),
        compiler_params=_cparams("parallel"),
        name="branch_merge",
    )(attn_o, sg_o, cv_o, y, y, y, w_branch)


def _outproj_body(m_ref, w_ref, x_ref, g_ref, gt_ref, o_ref):
    mix = jnp.dot(m_ref[...], w_ref[...], preferred_element_type=F32)
    o_ref[...] = x_ref[...] + gt_ref[0] * _rms(mix, g_ref[...])


def _outproj(merged, w_out, layer, xa, g, mod3, gt_idx, rows, seq, n_batch):
    d = xa.shape[1]
    tm = _pick_tile(seq, (512, 256))
    tps = seq // tm
    grp = lambda i: _group_of_tile(i, tps, n_batch)
    return pl.pallas_call(
        _outproj_body,
        grid=(rows // tm,),
        in_specs=[
            pl.BlockSpec((tm, d), lambda i: (i, 0)),
            pl.BlockSpec((None, d, d), lambda i: (layer, 0, 0), pipeline_mode=pl.Buffered(1)),
            pl.BlockSpec((tm, d), lambda i: (i, 0)),
            pl.BlockSpec((1, d), lambda i: (0, 0)),
            pl.BlockSpec((1, 1, d), lambda i: (grp(i), 0, gt_idx)),
        ],
        out_specs=pl.BlockSpec((tm, d), lambda i: (i, 0)),
        out_shape=jax.ShapeDtypeStruct(xa.shape, F32),
        input_output_aliases={2: 0},
        compiler_params=_cparams("parallel"),
        name="out_proj_residual",
    )(merged, w_out, xa, g, mod3)


def _top1_mask(vals, idx, n, axis):
    m = jnp.max(vals, axis=axis, keepdims=True)
    first = jnp.min(jnp.where(vals == m, idx, n), axis=axis, keepdims=True)
    return idx == first, m, first


def _route_body(x_ref, g_ref, sc_ref, sh_ref, wr_ref, rb_ref, hp_ref, wt_ref, ix_ref, rk_ref, cnt_ref):
    h2 = _rms(x_ref[...], g_ref[...]) * (1.0 + sc_ref[0]) + sh_ref[0]
    _store_row_tiles(hp_ref, _pack_bf16_pairs(h2))
    n_exp = wr_ref.shape[0]
    tm = x_ref.shape[0]
    per_group = n_exp // N_GROUPS
    logits = lax.dot_general(wr_ref[...], h2, (((1,), (1,)), ((), ())),
                             precision=HIGHEST, preferred_element_type=F32)
    scores = _sigmoid(logits)
    choice = scores + rb_ref[...]
    neg = -jnp.inf
    ji = lax.broadcasted_iota(jnp.int32, (per_group, tm), 0)
    groups, gs = [], []
    for g in range(N_GROUPS):
        cg = choice[g * per_group:(g + 1) * per_group, :]
        hit, m1, _ = _top1_mask(cg, ji, per_group, 0)
        m2 = jnp.max(jnp.where(hit, neg, cg), axis=0, keepdims=True)
        groups.append(cg)
        gs.append(m1 + m2)
    kept = []
    for g in range(N_GROUPS):
        beaten_by = jnp.zeros((1, tm), jnp.int32)
        for o in range(N_GROUPS):
            if o != g:
                beats = (gs[o] >= gs[g]) if o < g else (gs[o] > gs[g])
                beaten_by = beaten_by + beats.astype(jnp.int32)
        kept.append(jnp.where(beaten_by < TOPK_GROUPS, groups[g], neg))
    cm = jnp.concatenate(kept, axis=0)
    ei = lax.broadcasted_iota(jnp.int32, cm.shape, 0)
    ws, ids, hits = [], [], []
    sel = jnp.zeros(cm.shape, F32)
    for _ in range(TOP_K):
        hit, _, first = _top1_mask(cm, ei, n_exp, 0)
        ws.append(jnp.sum(jnp.where(hit, scores, 0.0), axis=0, keepdims=True))
        ids.append(first)
        hits.append(hit)
        sel = sel + jnp.where(hit, 1.0, 0.0)
        cm = jnp.where(hit, neg, cm)
    w = jnp.concatenate(ws, axis=0)
    wt_ref[...] = w / jnp.sum(w, axis=0, keepdims=True) * ROUTED_SCALE
    ix_ref[...] = jnp.concatenate(ids, axis=0)

    @pl.when(pl.program_id(0) == 0)
    def _():
        cnt_ref[...] = jnp.zeros_like(cnt_ref)

    before = (lax.broadcasted_iota(jnp.int32, (tm, tm), 0)
              < lax.broadcasted_iota(jnp.int32, (tm, tm), 1))
    within = jnp.dot(sel.astype(BF16), jnp.where(before, 1.0, 0.0).astype(BF16),
                     preferred_element_type=F32)
    rank = within + cnt_ref[:, 0:1]
    rk_ref[...] = jnp.concatenate(
        [jnp.sum(jnp.where(h, rank, 0.0), axis=0, keepdims=True) for h in hits], axis=0).astype(jnp.int32)
    cnt_ref[...] = cnt_ref[...] + jnp.sum(sel, axis=1, keepdims=True)


def _route(xa, g, mod3, sc_idx, sh_idx, w_router_t, router_bias, rows, seq, n_batch):
    d = xa.shape[1]
    n_exp = w_router_t.shape[0]
    tm = SEQ_TILE
    tps = seq // tm
    grp = lambda i: _group_of_tile(i, tps, n_batch)
    return pl.pallas_call(
        _route_body,
        grid=(rows // tm,),
        in_specs=[
            pl.BlockSpec((tm, d), lambda i: (i, 0)),
            pl.BlockSpec((1, d), lambda i: (0, 0)),
            pl.BlockSpec((1, 1, d), lambda i: (grp(i), 0, sc_idx)),
            pl.BlockSpec((1, 1, d), lambda i: (grp(i), 0, sh_idx)),
            pl.BlockSpec((n_exp, d), lambda i: (0, 0)),
            pl.BlockSpec((n_exp, 1), lambda i: (0, 0)),
        ],
        out_specs=[
            pl.BlockSpec((tm, SUBLANES, LANES), lambda i: (i, 0, 0)),
            pl.BlockSpec((TOP_K, tm), lambda i: (0, i)),
            pl.BlockSpec((TOP_K, tm), lambda i: (0, i)),
            pl.BlockSpec((TOP_K, tm), lambda i: (0, i)),
            pl.BlockSpec((n_exp, LANES), lambda i: (0, 0)),
        ],
        out_shape=[
            jax.ShapeDtypeStruct((rows, SUBLANES, LANES), U32),
            jax.ShapeDtypeStruct((TOP_K, rows), F32),
            jax.ShapeDtypeStruct((TOP_K, rows), jnp.int32),
            jax.ShapeDtypeStruct((TOP_K, rows), jnp.int32),
            jax.ShapeDtypeStruct((n_exp, LANES), F32),
        ],
        compiler_params=_cparams("arbitrary"),
        name="moe_norm_route",
    )(xa, g, mod3, mod3, w_router_t, router_bias.reshape(n_exp, 1))


def _dispatch_plan(idx_t, rank_t, counts, tile):
    n_exp = counts.shape[0]
    rows = idx_t.shape[1]
    tiles_e = (counts + tile - 1) // tile
    tile_end = jnp.cumsum(tiles_e)
    offs = (tile_end - tiles_e) * tile
    onehot = idx_t[None, :, :] == jnp.arange(n_exp, dtype=jnp.int32)[:, None, None]
    pos = rank_t + jnp.sum(jnp.where(onehot, offs[:, None, None], 0), axis=0)
    max_tiles = (TOP_K * rows) // tile + n_exp
    t = jnp.arange(max_tiles, dtype=jnp.int32)
    used = tile_end[-1]
    t_eff = jnp.minimum(t, used - 1)
    tile_expert = jnp.minimum(jnp.sum(t_eff[:, None] >= tile_end[None, :], axis=1), n_exp - 1)
    used_rows = used * tile
    pad = (jnp.append(offs + counts, used_rows).astype(jnp.int32),
           jnp.append(tiles_e * tile - counts, max_tiles * tile - used_rows).astype(jnp.int32))
    return pos.astype(jnp.int32), pad, tile_expert.astype(jnp.int32), t_eff.astype(jnp.int32), \
        (t < used).astype(jnp.int32), max_tiles


def _scatter_body(pad_start_ref, pad_len_ref, pos_ref, h_ref, xs_ref, zeros, sem, zsem):
    tm = h_ref.shape[0]
    n_exp = pad_start_ref.shape[0] - 1

    @pl.when(pl.program_id(0) == 0)
    def _():
        zeros[...] = jnp.zeros_like(zeros)
        half = zeros.shape[0]
        sizes = [half >> i for i in range(half.bit_length())]
        tail_start = pad_start_ref[n_exp]
        tail_halves = pad_len_ref[n_exp] // half

        def tail_copy(i):
            return pltpu.make_async_copy(zeros, xs_ref.at[pl.ds(tail_start + i * half, half)], zsem)

        lax.fori_loop(0, tail_halves, lambda i, c: (tail_copy(i).start(), c)[1], 0)
        lax.fori_loop(0, tail_halves, lambda i, c: (tail_copy(i).wait(), c)[1], 0)

        def pieces(e, wait):
            start = pad_start_ref[e]
            n = pad_len_ref[e]
            for size in sizes:
                has = (n & size) != 0
                copy = pltpu.make_async_copy(zeros.at[pl.ds(0, size)], xs_ref.at[pl.ds(start, size)], zsem)

                @pl.when(has)
                def _():
                    copy.wait() if wait else copy.start()

                start = start + jnp.where(has, size, 0)

        lax.fori_loop(0, n_exp, lambda e, c: (pieces(e, False), c)[1], 0)
        lax.fori_loop(0, n_exp, lambda e, c: (pieces(e, True), c)[1], 0)

    def issue(r, carry):
        for k in range(TOP_K):
            pltpu.make_async_copy(h_ref.at[r], xs_ref.at[pos_ref[k, r]], sem).start(
                priority=k % 2)
        return carry

    lax.fori_loop(0, tm, issue, 0)
    all_rows = xs_ref.at[pl.ds(0, TOP_K * tm)]
    pltpu.make_async_copy(all_rows, all_rows, sem).wait()


def _scatter(pad_start, pad_len, pos, hp, slots):
    rows = hp.shape[0]
    row_tile = hp.shape[1:]
    tm = SEQ_TILE
    assert EXPERT_TILE & (EXPERT_TILE - 1) == 0
    grid_spec = pltpu.PrefetchScalarGridSpec(
        num_scalar_prefetch=2,
        grid=(rows // tm,),
        in_specs=[
            pl.BlockSpec((TOP_K, tm), lambda i, ps, pn: (0, i), memory_space=pltpu.SMEM),
            pl.BlockSpec((tm,) + row_tile, lambda i, ps, pn: (i, 0, 0)),
        ],
        out_specs=pl.BlockSpec(memory_space=pl.ANY),
        scratch_shapes=[pltpu.VMEM((EXPERT_TILE // 2,) + row_tile, U32),
                        pltpu.SemaphoreType.DMA(()), pltpu.SemaphoreType.DMA(())],
    )
    return pl.pallas_call(
        _scatter_body,
        grid_spec=grid_spec,
        out_shape=jax.ShapeDtypeStruct((slots,) + row_tile, U32),
        compiler_params=_cparams("arbitrary"),
        name="moe_dispatch",
    )(pad_start, pad_len, pos, hp)


def _ffn_rows(xw, wgu_ref, wd_ref):
    x = jnp.concatenate([_unpack_lo(xw).astype(BF16), _unpack_hi(xw).astype(BF16)], axis=1)
    gu = jnp.dot(x, wgu_ref, preferred_element_type=F32)
    f = gu.shape[1] // 2
    hdn = (_silu(gu[:, :f]) * gu[:, f:]).astype(BF16)
    return jnp.dot(hdn, wd_ref, preferred_element_type=F32)


def _experts_body(te_ref, ts_ref, tv_ref, xs_ref, wg_ref, wu_ref, wd_ref, ys_ref, wgu_bf, wd_bf):
    del ts_ref
    t = pl.program_id(0)
    valid = tv_ref[t] == 1
    f = wg_ref.shape[-1]

    @pl.when(jnp.logical_or(t == 0, te_ref[t] != te_ref[jnp.maximum(t - 1, 0)]))
    def _():
        wgu_bf[:, :f] = wg_ref[0, 0].astype(BF16)
        wgu_bf[:, f:] = wu_ref[0, 0].astype(BF16)
        wd_bf[...] = wd_ref[0, 0].astype(BF16)

    @pl.when(valid)
    def _():
        y = _ffn_rows(_load_row_tiles(xs_ref), wgu_bf[...], wd_bf[...])
        _store_row_tiles(ys_ref, _pack_bf16_pairs(y))

    @pl.when(jnp.logical_not(valid))
    def _():
        ys_ref[...] = jnp.zeros_like(ys_ref)


def _experts(tile_expert, tile_src, tile_valid, xs, w_gate, w_up, w_down, layer, max_tiles):
    slots = xs.shape[0]
    row_tile = xs.shape[1:]
    tm = EXPERT_TILE
    d, f = w_gate.shape[2], w_gate.shape[3]
    grid_spec = pltpu.PrefetchScalarGridSpec(
        num_scalar_prefetch=3,
        grid=(max_tiles,),
        in_specs=[
            pl.BlockSpec((tm,) + row_tile, lambda t, te, ts, tv: (ts[t], 0, 0)),
            pl.BlockSpec((1, 1, d, f), lambda t, te, ts, tv: (layer, te[t], 0, 0)),
            pl.BlockSpec((1, 1, d, f), lambda t, te, ts, tv: (layer, te[t], 0, 0)),
            pl.BlockSpec((1, 1, f, d), lambda t, te, ts, tv: (layer, te[t], 0, 0)),
        ],
        out_specs=pl.BlockSpec((tm,) + row_tile, lambda t, te, ts, tv: (t, 0, 0)),
        scratch_shapes=[pltpu.VMEM((d, 2 * f), BF16), pltpu.VMEM((f, d), BF16)],
    )
    return pl.pallas_call(
        _experts_body,
        grid_spec=grid_spec,
        out_shape=jax.ShapeDtypeStruct((slots,) + row_tile, U32),
        compiler_params=_cparams("arbitrary"),
        name="moe_experts",
    )(tile_expert, tile_src, tile_valid, xs, w_gate, w_up, w_down)


def _combine_body(pos_ref, wt_ref, hp_ref, ys_ref, wgu_ref, wd_ref, x_ref, g_ref, gt_ref, o_ref,
                  gbuf, sem):
    tm = hp_ref.shape[0]

    def issue(r, carry):
        for k in range(TOP_K):
            pltpu.make_async_copy(ys_ref.at[pos_ref[k, r]], gbuf.at[k, r], sem).start(
                priority=k % 2)
        return carry

    lax.fori_loop(0, tm, issue, 0)
    y = _ffn_rows(_load_row_tiles(hp_ref), wgu_ref[...], wd_ref[...])
    all_rows = ys_ref.at[pl.ds(0, TOP_K * tm)]
    pltpu.make_async_copy(all_rows, all_rows, sem).wait()
    half = y.shape[1] // 2
    lo, hi = y[:, :half], y[:, half:]
    wt = wt_ref[...]
    for k in range(TOP_K):
        wk = wt[:, k:k + 1]
        gk = _load_row_tiles(gbuf.at[k])
        lo = lo + wk * _unpack_lo(gk)
        hi = hi + wk * _unpack_hi(gk)
    y = jnp.concatenate([lo, hi], axis=1)
    o_ref[...] = x_ref[...] + gt_ref[0] * _rms(y, g_ref[...])


def _combine(pos, wts, hp, ys, w_sgu, w_sd, xa, g, mod3, gt_idx, rows, seq, n_batch, in_place):
    d = xa.shape[1]
    row_tile = hp.shape[1:]
    tm = SEQ_TILE
    tps = seq // tm
    grp = lambda i: _group_of_tile(i, tps, n_batch)
    return pl.pallas_call(
        _combine_body,
        grid=(rows // tm,),
        in_specs=[
            pl.BlockSpec((TOP_K, tm), lambda i: (0, i), memory_space=pltpu.SMEM),
            pl.BlockSpec((tm, TOP_K), lambda i: (i, 0)),
            pl.BlockSpec((tm,) + row_tile, lambda i: (i, 0, 0)),
            pl.BlockSpec(memory_space=pl.ANY),
            pl.BlockSpec(w_sgu.shape, lambda i: (0, 0), pipeline_mode=pl.Buffered(1)),
            pl.BlockSpec(w_sd.shape, lambda i: (0, 0), pipeline_mode=pl.Buffered(1)),
            pl.BlockSpec((tm, d), lambda i: (i, 0)),
            pl.BlockSpec((1, d), lambda i: (0, 0)),
            pl.BlockSpec((1, 1, d), lambda i: (grp(i), 0, gt_idx)),
        ],
        out_specs=pl.BlockSpec((tm, d), lambda i: (i, 0)),
        out_shape=jax.ShapeDtypeStruct(xa.shape if in_place else (rows, d), F32),
        scratch_shapes=[pltpu.VMEM((TOP_K, tm) + row_tile, U32), pltpu.SemaphoreType.DMA(())],
        input_output_aliases={6: 0} if in_place else {},
        compiler_params=_cparams("arbitrary"),
        name="moe_combine",
    )(pos, wts, hp, ys, w_sgu, w_sd, xa, g, mod3)


def kernel(x, c, ctx, c_ctx, w_ada, b_ada, norm_g, w_in, diff_lambda, g_subln, sg_ln_g, sg_ln_b, sg_w, sg_b, conv_w, w_branch, w_out, w_router, router_bias, w_exp_gate, w_exp_up, w_exp_down, w_sh_gate, w_sh_up, w_sh_down):
    n_batch, seq, d = x.shape
    ctx_len = ctx.shape[1]
    depth = w_ada.shape[0]
    mix_w = w_branch.shape[2]
    n_heads = mix_w // HEAD_W
    n_exp = w_router.shape[2]
    n_lat = n_batch * seq
    n_ctx = n_batch * ctx_len
    n_all = n_lat + n_ctx
    assert n_batch + 1 <= MOD_ROWS and seq % SEQ_TILE == 0 and ctx_len % SEQ_TILE == 0
    assert n_lat % seq == 0 and n_lat % ctx_len == 0 and mix_w // SG_GROUPS == LANES
    assert d // 2 == SUBLANES * LANES

    xa = _concat_rows(x.reshape(n_lat, d), ctx.reshape(n_ctx, d))
    cond = jnp.zeros((MOD_ROWS, d), F32).at[:n_batch].set(c).at[n_batch].set(c_ctx)
    mod = _ada(cond, w_ada, b_ada)
    cos, sin = _rope_tables(seq)
    q_cb, k_cb, v_cb = 0, n_heads, 2 * n_heads
    gate_cb = (8 * mix_w) // d

    w_in_b = w_in.astype(BF16)
    w_branch_b = w_branch.astype(BF16)
    w_out_b = w_out.astype(BF16)
    n_in = w_in.shape[2]

    for l in range(depth):
        last = l == depth - 1
        lam_init = 0.8 - 0.6 * math.exp(-0.3 * l)
        mod3 = mod[l].reshape(MOD_ROWS, 1, N_MOD * d)
        ng = norm_g[l]
        rows = n_lat if last else n_all

        in_proj = functools.partial(_norm_matmul, xa, ng[0:1], mod3, 1, 0, w_in_b, l,
                                    seq=seq, n_batch=n_batch)
        if last:
            y = in_proj(n_lat, 0, n_in, 0, name="in_proj")
            yc = in_proj(n_ctx, n_lat, 2 * mix_w, mix_w, name="in_proj_ctx_kv")
            ctx_kv = (yc, 0, yc, n_heads, 0, ctx_len)
        else:
            y = in_proj(n_all, 0, n_in, 0, name="in_proj")
            ctx_kv = (y, k_cb, y, v_cb, n_lat // ctx_len, ctx_len)
        attn_o = _attention(y, q_cb, 0, y, k_cb, y, v_cb, 0, seq, seq, n_batch, n_heads,
                            diff_lambda[l], g_subln[l:l + 1], lam_init, ctx=ctx_kv, rope=(cos, sin),
                            name="attn_latent")
        if not last:
            attn_c = _attention(y, q_cb, n_lat // ctx_len, y, k_cb, y, v_cb, n_lat // ctx_len,
                                ctx_len, ctx_len, n_batch, n_heads,
                                diff_lambda[l], g_subln[l:l + 1], lam_init, name="attn_ctx")
            attn_o = jnp.concatenate([attn_o, attn_c], axis=0)
        sg_bias_map = jnp.repeat(sg_b[l].T, mix_w // SG_GROUPS, axis=1)
        sg_o, cv_o = _branches(y, rows, n_lat, seq, mix_w, 3, sg_ln_g[l:l + 1], sg_ln_b[l:l + 1],
                               sg_w[l].astype(BF16), sg_bias_map, conv_w[l])
        merged = _merge(attn_o, sg_o, cv_o, y, gate_cb, w_branch_b, l, rows)
        xa = _outproj(merged, w_out_b, l, xa, ng[1:2], mod3, 2, rows, seq, n_batch)

        hp, wt_t, idx_t, rank_t, cnt = _route(xa, ng[2:3], mod3, 4, 3, w_router[l].T, router_bias[l],
                                              rows, seq, n_batch)
        pos, pad, tile_expert, tile_src, tile_valid, max_tiles = _dispatch_plan(
            idx_t, rank_t, cnt[:, 0].astype(jnp.int32), EXPERT_TILE)
        xs = _scatter(*pad, pos, hp, max_tiles * EXPERT_TILE)
        ys = _experts(tile_expert, tile_src, tile_valid, xs, w_exp_gate, w_exp_up, w_exp_down, l, max_tiles)
        w_sgu = jnp.concatenate([w_sh_gate[l], w_sh_up[l]], axis=-1).astype(BF16)
        xa = _combine(pos, wt_t.T, hp, ys, w_sgu, w_sh_down[l].astype(BF16), xa, ng[3:4], mod3, 5,
                      rows, seq, n_batch, in_place=not last)

    return xa.reshape(n_batch, seq, d)
```

```python
import functools
import math

import jax
import jax.numpy as jnp
from jax import lax
from jax.experimental import pallas as pl
from jax.experimental.pallas import tpu as pltpu

GRID_W = 64
QK_DIM = 64
HEAD_W = 2 * QK_DIM
ROPE_BASE = 10000.0
SG_CHUNK = 128
SG_GROUPS = 8
TOP_K = 8
N_GROUPS = 8
TOPK_GROUPS = 4
ROUTED_SCALE = 2.5
NORM_EPS = 1e-6
N_MOD = 6

LANES = 128
SUBLANES = 8
VMEM_LIMIT = 56 * 1024 * 1024
SEQ_TILE = 256
EXPERT_TILE = 512
MOD_ROWS = 16
ATTN_SUB_ROWS = 128

F32 = jnp.float32
BF16 = jnp.bfloat16
U32 = jnp.uint32
HIGHEST = lax.Precision.HIGHEST


def _cparams(*sem):
    return pltpu.CompilerParams(dimension_semantics=sem, vmem_limit_bytes=VMEM_LIMIT)


def _pick_tile(n, cands):
    for c in cands:
        if n % c == 0:
            return c
    raise ValueError(f"no tile for {n} in {cands}")


def _sigmoid(x):
    return 1.0 / (1.0 + jnp.exp(-x))


def _silu(x):
    return x * _sigmoid(x)


def _gelu_tanh(x):
    c = math.sqrt(2.0 / math.pi)
    return x * (0.5 * (1.0 + jnp.tanh(c * (x + 0.044715 * (x * x * x)))))


def _rms(x, g):
    return x * lax.rsqrt(jnp.mean(x * x, axis=-1, keepdims=True) + NORM_EPS) * g


def _pack_bf16_pairs(x):
    w = x.shape[1] // 2
    bits = pltpu.bitcast(x.astype(BF16).astype(F32), U32)
    return (bits[:, :w] >> 16) | bits[:, w:]


def _unpack_lo(w):
    return pltpu.bitcast(w << 16, F32)


def _unpack_hi(w):
    return pltpu.bitcast(w & jnp.uint32(0xFFFF0000), F32)


def _store_row_tiles(ref, packed):
    m, s, _ = ref.shape
    flat = ref.reshape(m * s, LANES)
    for j in range(s):
        flat[pl.ds(j, m, stride=s), :] = packed[:, j * LANES:(j + 1) * LANES]


def _load_row_tiles(ref):
    m, s, _ = ref.shape
    flat = ref.reshape(m * s, LANES)
    return jnp.concatenate([flat[pl.ds(j, m, stride=s), :] for j in range(s)], axis=1)


def _concat_body(a_ref, b_ref, o_ref, *, a_tiles):
    i = pl.program_id(0)

    @pl.when(i < a_tiles)
    def _():
        o_ref[...] = a_ref[...]

    @pl.when(i >= a_tiles)
    def _():
        o_ref[...] = b_ref[...]


def _concat_rows(a, b):
    d = a.shape[1]
    tm = _pick_tile(math.gcd(a.shape[0], b.shape[0]), (512, 256))
    na, nb = a.shape[0] // tm, b.shape[0] // tm
    return pl.pallas_call(
        functools.partial(_concat_body, a_tiles=na),
        grid=(na + nb,),
        in_specs=[
            pl.BlockSpec((tm, d), lambda i: (jnp.minimum(i, na - 1), 0)),
            pl.BlockSpec((tm, d), lambda i: (jnp.maximum(i - na, 0), 0)),
        ],
        out_specs=pl.BlockSpec((tm, d), lambda i: (i, 0)),
        out_shape=jax.ShapeDtypeStruct((a.shape[0] + b.shape[0], d), a.dtype),
        compiler_params=_cparams("parallel"),
        name="concat_rows",
    )(a, b)


def _ada_body(c_ref, w_ref, b_ref, o_ref):
    a = _silu(c_ref[...])
    o_ref[0] = jnp.dot(a, w_ref[0], precision=HIGHEST, preferred_element_type=F32) + b_ref[0]


def _ada(cond, w_ada, b_ada):
    depth, d, n = w_ada.shape
    tn = _pick_tile(n, (1024, 512, 256, 128))
    return pl.pallas_call(
        _ada_body,
        grid=(depth, n // tn),
        in_specs=[
            pl.BlockSpec((MOD_ROWS, d), lambda l, j: (0, 0)),
            pl.BlockSpec((1, d, tn), lambda l, j: (l, 0, j)),
            pl.BlockSpec((1, 1, tn), lambda l, j: (l, 0, j)),
        ],
        out_specs=pl.BlockSpec((1, MOD_ROWS, tn), lambda l, j: (l, 0, j)),
        out_shape=jax.ShapeDtypeStruct((depth, MOD_ROWS, n), F32),
        compiler_params=_cparams("arbitrary", "arbitrary"),
        name="ada_mod",
    )(cond, w_ada, b_ada.reshape(depth, 1, n))


def _group_of_tile(i, tiles_per_seq, n_batch):
    return jnp.minimum(i // tiles_per_seq, n_batch)


def _stream_specs(stream, tm, tile_of):
    if not isinstance(stream, tuple):
        return [pl.BlockSpec((tm, stream.shape[1]), lambda *idx: (tile_of(*idx), 0))], None
    a, b = stream
    split = a.shape[0] // tm
    assert a.shape[0] % tm == 0 and b.shape[0] % tm == 0
    return [pl.BlockSpec((tm, a.shape[1]), lambda *idx: (jnp.minimum(tile_of(*idx), split - 1), 0)),
            pl.BlockSpec((tm, b.shape[1]), lambda *idx: (jnp.maximum(tile_of(*idx) - split, 0), 0),
                         pipeline_mode=pl.Buffered(1))], split


def _stream_tile(refs, tile, split):
    if split is None:
        return refs[0][...]
    return jnp.where(tile < split, refs[0][...], refs[1][...])


def _stream_args(stream):
    return list(stream) if isinstance(stream, tuple) else [stream]


def _norm_mm_body(*refs, split, tile0):
    n_src = 1 if split is None else 2
    g_ref, sc_ref, sh_ref, w_ref, o_ref, h_ref = refs[n_src:]

    @pl.when(pl.program_id(1) == 0)
    def _():
        y = _rms(_stream_tile(refs[:n_src], pl.program_id(0) + tile0, split), g_ref[...])
        h_ref[...] = (y * (1.0 + sc_ref[0]) + sh_ref[0]).astype(h_ref.dtype)

    o_ref[...] = jnp.dot(h_ref[...], w_ref[...], preferred_element_type=F32).astype(o_ref.dtype)


def _norm_matmul(stream, g, mod3, sc_idx, sh_idx, w, layer, rows, row_off, cols, col_off, seq, n_batch, name):
    d = w.shape[1]
    tm = _pick_tile(math.gcd(math.gcd(rows, row_off) if row_off else rows, seq), (1024, 512, 256))
    tn = _pick_tile(math.gcd(cols, col_off) if col_off else cols, (1024, 512, 256))
    ro, co = row_off // tm, col_off // tn
    tps = seq // tm
    grp = lambda i: _group_of_tile(i + ro, tps, n_batch)
    x_specs, split = _stream_specs(stream, tm, lambda i, j: i + ro)
    return pl.pallas_call(
        functools.partial(_norm_mm_body, split=split, tile0=ro),
        grid=(rows // tm, cols // tn),
        in_specs=x_specs + [
            pl.BlockSpec((1, d), lambda i, j: (0, 0)),
            pl.BlockSpec((1, 1, d), lambda i, j: (grp(i), 0, sc_idx)),
            pl.BlockSpec((1, 1, d), lambda i, j: (grp(i), 0, sh_idx)),
            pl.BlockSpec((None, d, tn), lambda i, j: (layer, 0, j + co)),
        ],
        out_specs=pl.BlockSpec((tm, tn), lambda i, j: (i, j)),
        out_shape=jax.ShapeDtypeStruct((rows, cols), BF16),
        scratch_shapes=[pltpu.VMEM((tm, d), BF16)],
        compiler_params=_cparams("arbitrary", "arbitrary"),
        name=name,
    )(*_stream_args(stream), g, mod3, mod3, w)


def _rope_apply(x, cos, sin):
    lane = lax.broadcasted_iota(jnp.int32, x.shape, 1)
    first_half = (lane % (QK_DIM // 2)) < (QK_DIM // 4)
    partner = jnp.where(first_half,
                        pltpu.roll(x, LANES - QK_DIM // 4, 1),
                        pltpu.roll(x, QK_DIM // 4, 1))
    return x * cos + partner * sin


def _rope_tables(seq):
    n_freq = QK_DIM // 4
    inv = ROPE_BASE ** (-jnp.arange(n_freq, dtype=F32) / n_freq)
    t = jnp.arange(seq)
    row = (t // GRID_W).astype(F32)
    col = (t % GRID_W).astype(F32)
    ang = jnp.stack([row[:, None] * inv, col[:, None] * inv], axis=1)
    cos = jnp.cos(ang)
    sin = jnp.sin(ang)
    cos64 = jnp.stack([cos, cos], axis=2).reshape(seq, QK_DIM)
    sin64 = jnp.stack([-sin, sin], axis=2).reshape(seq, QK_DIM)
    reps = LANES // QK_DIM
    return jnp.tile(cos64, (1, reps)), jnp.tile(sin64, (1, reps))


def _attn_body(*refs, two_seg, rope, lam_init):
    refs = list(refs)
    q_ref, k_ref, v_ref = refs[:3]
    del refs[:3]
    segs = []
    if two_seg:
        segs.append((refs[0], refs[1], False))
        del refs[:2]
    segs.append((k_ref, v_ref, rope))
    if rope:
        cos_ref, sin_ref = refs[:2]
        del refs[:2]
    dl_ref, g_ref, o_ref, kt_ref, v1_ref = refs
    tq = q_ref.shape[0]

    @pl.when(pl.program_id(2) == 0)
    def _():
        row = 0
        for ks_ref, vs_ref, rotate in segs:
            n = ks_ref.shape[0]
            ks = ks_ref[...].astype(F32)
            if rotate:
                ks = _rope_apply(ks, cos_ref[...], sin_ref[...])
            kt_ref[:, row:row + n] = ks.T.astype(BF16)
            v1_ref[row:row + n, :HEAD_W] = vs_ref[...]
            v1_ref[row:row + n, HEAD_W:] = jnp.ones(vs_ref.shape, BF16)
            row += n

    dl = dl_ref[...]
    lam = (jnp.exp(jnp.sum(dl[0:1] * dl[1:2], axis=-1, keepdims=True))
           - jnp.exp(jnp.sum(dl[2:3] * dl[3:4], axis=-1, keepdims=True)) + lam_init)
    q = q_ref[...]
    if rope:
        q_rows = pl.ds(pl.multiple_of(pl.program_id(2) * tq, tq), tq)
        q = _rope_apply(q.astype(F32), cos_ref[q_rows, :], sin_ref[q_rows, :]).astype(BF16)
    q = q * jnp.asarray(QK_DIM ** -0.5, BF16)
    lane = lax.broadcasted_iota(jnp.int32, q.shape, 1)
    zero = jnp.zeros_like(q)

    def softmax_v(qm):
        s = jnp.dot(qm, kt_ref[...], preferred_element_type=F32)
        m = jnp.max(s, axis=-1, keepdims=True)
        acc = jnp.dot(jnp.exp(s - m).astype(BF16), v1_ref[...], preferred_element_type=F32)
        return acc[:, :HEAD_W] / acc[:, HEAD_W:]

    q1 = jnp.where(lane < QK_DIM, q, zero)
    q2 = jnp.where(lane >= QK_DIM, q, zero)
    sub = min(q.shape[0], ATTN_SUB_ROWS)
    for r0 in range(0, q.shape[0], sub):
        rs = slice(r0, r0 + sub)
        o = softmax_v(q1[rs]) - lam * softmax_v(q2[rs])
        o_ref[rs, :] = (_rms(o, g_ref[...]) * (1.0 - lam_init)).astype(o_ref.dtype)


def _attention(q_arr, q_cb, q_rb, k_arr, k_cb, v_arr, v_cb, kv_rb, seq_q, seq_k, n_batch, n_heads,
               dl, g_sub, lam_init, ctx=None, rope=None, name="attn"):
    tq = _pick_tile(seq_q, (1024, 512, 256, 128))
    nq = seq_q // tq
    n_keys = seq_k + (ctx[5] if ctx is not None else 0)
    scratch = [pltpu.VMEM((HEAD_W, n_keys), BF16), pltpu.VMEM((n_keys, 2 * HEAD_W), BF16)]
    in_specs = [
        pl.BlockSpec((tq, HEAD_W), lambda b, h, i: ((q_rb + b) * nq + i, q_cb + h)),
        pl.BlockSpec((seq_k, HEAD_W), lambda b, h, i: (kv_rb + b, k_cb + h)),
        pl.BlockSpec((seq_k, HEAD_W), lambda b, h, i: (kv_rb + b, v_cb + h)),
    ]
    args = [q_arr, k_arr, v_arr]
    if ctx is not None:
        kc_arr, kc_cb, vc_arr, vc_cb, c_rb, seq_c = ctx
        in_specs += [
            pl.BlockSpec((seq_c, HEAD_W), lambda b, h, i: (c_rb + b, kc_cb + h)),
            pl.BlockSpec((seq_c, HEAD_W), lambda b, h, i: (c_rb + b, vc_cb + h)),
        ]
        args += [kc_arr, vc_arr]
    if rope is not None:
        assert seq_q == seq_k
        in_specs += [pl.BlockSpec((seq_k, LANES), lambda b, h, i: (0, 0))] * 2
        args += list(rope)
    in_specs += [
        pl.BlockSpec(dl.shape, lambda b, h, i: (0, 0)),
        pl.BlockSpec((1, HEAD_W), lambda b, h, i: (0, 0)),
    ]
    args += [dl, g_sub]
    return pl.pallas_call(
        functools.partial(_attn_body, two_seg=ctx is not None, rope=rope is not None, lam_init=lam_init),
        grid=(n_batch, n_heads, nq),
        in_specs=in_specs,
        out_specs=pl.BlockSpec((tq, HEAD_W), lambda b, h, i: (b * nq + i, h)),
        out_shape=jax.ShapeDtypeStruct((n_batch * seq_q, n_heads * HEAD_W), BF16),
        scratch_shapes=scratch,
        compiler_params=_cparams("arbitrary", "arbitrary", "arbitrary"),
        name=name,
    )(*args)


def _branches_body(u_ref, v_ref, x_ref, b_ref, c_ref, xp_ref, cp_ref, xn_ref, cn_ref,
                   lng_ref, lnb_ref, sgw_ref, sgb_ref, cw_ref, sg_o_ref, cv_o_ref,
                   *, lat_tiles, tiles_per_seq):
    i = pl.program_id(0)
    tm, w = u_ref.shape
    u = _gelu_tanh(u_ref[...].astype(F32))
    v = _gelu_tanh(v_ref[...].astype(F32))
    mu = jnp.mean(v, axis=-1, keepdims=True)
    dv = v - mu
    v = dv * lax.rsqrt(jnp.mean(dv * dv, axis=-1, keepdims=True) + NORM_EPS)
    v = (v * lng_ref[...] + lnb_ref[...]).astype(BF16)
    gw = w // SG_GROUPS
    for c in range(tm // SG_CHUNK):
        rs = slice(c * SG_CHUNK, (c + 1) * SG_CHUNK)
        for g in range(SG_GROUPS):
            cs = slice(g * gw, (g + 1) * gw)
            mixed = jnp.dot(sgw_ref[g], v[rs, cs], preferred_element_type=F32) + sgb_ref[:, cs]
            sg_o_ref[rs, cs] = (u[rs, cs] * mixed).astype(sg_o_ref.dtype)
    is_lat = i < lat_tiles
    seq_first = jnp.logical_or(jnp.logical_not(is_lat), i % tiles_per_seq == 0)
    seq_last = jnp.logical_or(jnp.logical_not(is_lat), i % tiles_per_seq == tiles_per_seq - 1)
    z = c_ref[...].astype(F32) * x_ref[...].astype(F32)
    hp = xp_ref.shape[0]
    z_prev = (cp_ref[...].astype(F32) * xp_ref[...].astype(F32))[hp - 1:hp, :]
    z_next = (cn_ref[...].astype(F32) * xn_ref[...].astype(F32))[0:1, :]
    z_prev = jnp.where(seq_first, 0.0, z_prev)
    z_next = jnp.where(seq_last, 0.0, z_next)
    row = lax.broadcasted_iota(jnp.int32, z.shape, 0)
    z_dn = jnp.where(row == 0, z_prev, pltpu.roll(z, 1, 0))
    z_up = jnp.where(row == tm - 1, z_next, pltpu.roll(z, tm - 1, 0))
    y = cw_ref[0:1, :] * z_dn + cw_ref[1:2, :] * z + cw_ref[2:3, :] * z_up
    cv_o_ref[...] = (b_ref[...].astype(F32) * y).astype(cv_o_ref.dtype)


def _branches(y, rows, lat_rows, seq, mix_w, col0, sg_ln_g, sg_ln_b, sg_w, sg_bias_map, conv_w):
    tm = SEQ_TILE
    halo = 16
    hb = tm // halo
    n_tiles = rows // tm
    last_hblk = rows // halo - 1
    cur = lambda c: pl.BlockSpec((tm, mix_w), lambda i: (i, col0 + c))
    prev = lambda c: pl.BlockSpec((halo, mix_w), lambda i: (jnp.maximum(i * hb - 1, 0), col0 + c))
    nxt = lambda c: pl.BlockSpec((halo, mix_w), lambda i: (jnp.minimum((i + 1) * hb, last_hblk), col0 + c))
    full = lambda a: pl.BlockSpec(a.shape, lambda i: (0,) * a.ndim)
    consts = [sg_ln_g, sg_ln_b, sg_w, sg_bias_map, conv_w]
    out = jax.ShapeDtypeStruct((rows, mix_w), BF16)
    return pl.pallas_call(
        functools.partial(_branches_body, lat_tiles=lat_rows // tm, tiles_per_seq=seq // tm),
        grid=(n_tiles,),
        in_specs=[cur(0), cur(1), cur(2), cur(3), cur(4), prev(2), prev(4), nxt(2), nxt(4)]
                 + [full(a) for a in consts],
        out_specs=[pl.BlockSpec((tm, mix_w), lambda i: (i, 0))] * 2,
        out_shape=[out, out],
        compiler_params=_cparams("parallel"),
        name="sg_conv_branches",
    )(y, y, y, y, y, y, y, y, y, *consts)


def _merge_body(a_ref, s_ref, c_ref, g0_ref, g1_ref, g2_ref, wb_ref, o_ref):
    acc = None
    for g, (br, gl) in enumerate(((a_ref, g0_ref), (s_ref, g1_ref), (c_ref, g2_ref))):
        t = _sigmoid(gl[...].astype(F32)) * jnp.dot(br[...], wb_ref[g], preferred_element_type=F32)
        acc = t if acc is None else acc + t
    o_ref[...] = acc.astype(o_ref.dtype)


def _merge(attn_o, sg_o, cv_o, y, gate_cb, w_branch, layer, rows):
    n_br, mix_w, d = w_branch.shape[1:]
    tm = _pick_tile(rows, (512, 256))
    br = pl.BlockSpec((tm, mix_w), lambda i: (i, 0))
    gate = lambda g: pl.BlockSpec((tm, d), lambda i: (i, gate_cb + g))
    return pl.pallas_call(
        _merge_body,
        grid=(rows // tm,),
        in_specs=[br, br, br, gate(0), gate(1), gate(2),
                  pl.BlockSpec((None, n_br, mix_w, d), lambda i: (layer, 0, 0, 0),
                               pipeline_mode=pl.Buffered(1))],
        out_specs=pl.BlockSpec((tm, d), lambda i: (i, 0)),
        out_shape=jax.ShapeDtypeStruct((rows, d), BF16),
        compiler_params=_cparams("parallel"),
        name="branch_merge",
    )(attn_o, sg_o, cv_o, y, y, y, w_branch)


def _outproj_body(m_ref, w_ref, g_ref, gt_ref, *refs, split):
    x_refs, o_ref = refs[:-1], refs[-1]
    mix = jnp.dot(m_ref[...], w_ref[...], preferred_element_type=F32)
    x = _stream_tile(x_refs, pl.program_id(0), split)
    o_ref[...] = x + gt_ref[0] * _rms(mix, g_ref[...])


def _outproj(merged, w_out, layer, stream, g, mod3, gt_idx, rows, seq, n_batch):
    d = w_out.shape[1]
    tm = _pick_tile(seq, (512, 256))
    tps = seq // tm
    grp = lambda i: _group_of_tile(i, tps, n_batch)
    x_specs, split = _stream_specs(stream, tm, lambda i: i)
    if split is None:
        out_rows, alias = stream.shape[0], {4: 0}
    else:
        out_rows, alias = rows, {}
        assert rows == stream[0].shape[0] + stream[1].shape[0]
    return pl.pallas_call(
        functools.partial(_outproj_body, split=split),
        grid=(rows // tm,),
        in_specs=[
            pl.BlockSpec((tm, d), lambda i: (i, 0)),
            pl.BlockSpec((None, d, d), lambda i: (layer, 0, 0), pipeline_mode=pl.Buffered(1)),
            pl.BlockSpec((1, d), lambda i: (0, 0)),
            pl.BlockSpec((1, 1, d), lambda i: (grp(i), 0, gt_idx)),
        ] + x_specs,
        out_specs=pl.BlockSpec((tm, d), lambda i: (i, 0)),
        out_shape=jax.ShapeDtypeStruct((out_rows, d), F32),
        input_output_aliases=alias,
        compiler_params=_cparams("parallel"),
        name="out_proj_residual",
    )(merged, w_out, g, mod3, *_stream_args(stream))


def _top1_mask(vals, idx, n, axis):
    m = jnp.max(vals, axis=axis, keepdims=True)
    first = jnp.min(jnp.where(vals == m, idx, n), axis=axis, keepdims=True)
    return idx == first, m, first


def _route_body(x_ref, g_ref, sc_ref, sh_ref, wr_ref, rb_ref, hp_ref, wt_ref, ix_ref, rk_ref, cnt_ref):
    h2 = _rms(x_ref[...], g_ref[...]) * (1.0 + sc_ref[0]) + sh_ref[0]
    _store_row_tiles(hp_ref, _pack_bf16_pairs(h2))
    n_exp = wr_ref.shape[0]
    tm = x_ref.shape[0]
    per_group = n_exp // N_GROUPS
    logits = lax.dot_general(wr_ref[...], h2, (((1,), (1,)), ((), ())),
                             precision=HIGHEST, preferred_element_type=F32)
    scores = _sigmoid(logits)
    choice = scores + rb_ref[...]
    neg = -jnp.inf
    ji = lax.broadcasted_iota(jnp.int32, (per_group, tm), 0)
    groups, gs = [], []
    for g in range(N_GROUPS):
        cg = choice[g * per_group:(g + 1) * per_group, :]
        hit, m1, _ = _top1_mask(cg, ji, per_group, 0)
        m2 = jnp.max(jnp.where(hit, neg, cg), axis=0, keepdims=True)
        groups.append(cg)
        gs.append(m1 + m2)
    kept = []
    for g in range(N_GROUPS):
        beaten_by = jnp.zeros((1, tm), jnp.int32)
        for o in range(N_GROUPS):
            if o != g:
                beats = (gs[o] >= gs[g]) if o < g else (gs[o] > gs[g])
                beaten_by = beaten_by + beats.astype(jnp.int32)
        kept.append(jnp.where(beaten_by < TOPK_GROUPS, groups[g], neg))
    cm = jnp.concatenate(kept, axis=0)
    ei = lax.broadcasted_iota(jnp.int32, cm.shape, 0)
    ws, ids, hits = [], [], []
    sel = jnp.zeros(cm.shape, F32)
    for _ in range(TOP_K):
        hit, _, first = _top1_mask(cm, ei, n_exp, 0)
        ws.append(jnp.sum(jnp.where(hit, scores, 0.0), axis=0, keepdims=True))
        ids.append(first)
        hits.append(hit)
        sel = sel + jnp.where(hit, 1.0, 0.0)
        cm = jnp.where(hit, neg, cm)
    w = jnp.concatenate(ws, axis=0)
    wt_ref[...] = w / jnp.sum(w, axis=0, keepdims=True) * ROUTED_SCALE
    ix_ref[...] = jnp.concatenate(ids, axis=0)

    @pl.when(pl.program_id(0) == 0)
    def _():
        cnt_ref[...] = jnp.zeros_like(cnt_ref)

    before = (lax.broadcasted_iota(jnp.int32, (tm, tm), 0)
              < lax.broadcasted_iota(jnp.int32, (tm, tm), 1))
    within = jnp.dot(sel.astype(BF16), jnp.where(before, 1.0, 0.0).astype(BF16),
                     preferred_element_type=F32)
    rank = within + cnt_ref[:, 0:1]
    rk_ref[...] = jnp.concatenate(
        [jnp.sum(jnp.where(h, rank, 0.0), axis=0, keepdims=True) for h in hits], axis=0).astype(jnp.int32)
    cnt_ref[...] = cnt_ref[...] + jnp.sum(sel, axis=1, keepdims=True)


def _route(xa, g, mod3, sc_idx, sh_idx, w_router_t, router_bias, rows, seq, n_batch):
    d = xa.shape[1]
    n_exp = w_router_t.shape[0]
    tm = SEQ_TILE
    tps = seq // tm
    grp = lambda i: _group_of_tile(i, tps, n_batch)
    return pl.pallas_call(
        _route_body,
        grid=(rows // tm,),
        in_specs=[
            pl.BlockSpec((tm, d), lambda i: (i, 0)),
            pl.BlockSpec((1, d), lambda i: (0, 0)),
            pl.BlockSpec((1, 1, d), lambda i: (grp(i), 0, sc_idx)),
            pl.BlockSpec((1, 1, d), lambda i: (grp(i), 0, sh_idx)),
            pl.BlockSpec((n_exp, d), lambda i: (0, 0)),
            pl.BlockSpec((n_exp, 1), lambda i: (0, 0)),
        ],
        out_specs=[
            pl.BlockSpec((tm, SUBLANES, LANES), lambda i: (i, 0, 0)),
            pl.BlockSpec((TOP_K, tm), lambda i: (0, i)),
            pl.BlockSpec((TOP_K, tm), lambda i: (0, i)),
            pl.BlockSpec((TOP_K, tm), lambda i: (0, i)),
            pl.BlockSpec((n_exp, LANES), lambda i: (0, 0)),
        ],
        out_shape=[
            jax.ShapeDtypeStruct((rows, SUBLANES, LANES), U32),
            jax.ShapeDtypeStruct((TOP_K, rows), F32),
            jax.ShapeDtypeStruct((TOP_K, rows), jnp.int32),
            jax.ShapeDtypeStruct((TOP_K, rows), jnp.int32),
            jax.ShapeDtypeStruct((n_exp, LANES), F32),
        ],
        compiler_params=_cparams("arbitrary"),
        name="moe_norm_route",
    )(xa, g, mod3, mod3, w_router_t, router_bias.reshape(n_exp, 1))


def _dispatch_plan(idx_t, rank_t, counts, tile):
    n_exp = counts.shape[0]
    rows = idx_t.shape[1]
    tiles_e = (counts + tile - 1) // tile
    tile_end = jnp.cumsum(tiles_e)
    offs = (tile_end - tiles_e) * tile
    onehot = idx_t[None, :, :] == jnp.arange(n_exp, dtype=jnp.int32)[:, None, None]
    pos = rank_t + jnp.sum(jnp.where(onehot, offs[:, None, None], 0), axis=0)
    max_tiles = (TOP_K * rows) // tile + n_exp
    t = jnp.arange(max_tiles, dtype=jnp.int32)
    used = tile_end[-1]
    t_eff = jnp.minimum(t, used - 1)
    tile_expert = jnp.minimum(jnp.sum(t_eff[:, None] >= tile_end[None, :], axis=1), n_exp - 1)
    used_rows = used * tile
    pad = (jnp.append(offs + counts, used_rows).astype(jnp.int32),
           jnp.append(tiles_e * tile - counts, max_tiles * tile - used_rows).astype(jnp.int32))
    own = tile_expert[:, None] == jnp.arange(n_exp, dtype=jnp.int32)[None, :]
    last_row = jnp.sum(jnp.where(own, (offs + counts)[None, :], 0), axis=1)
    tile_rows = jnp.where(t < used, jnp.clip(last_row - t * tile, 0, tile), 0)
    return pos.astype(jnp.int32), pad, tile_expert.astype(jnp.int32), t_eff.astype(jnp.int32), \
        tile_rows.astype(jnp.int32), max_tiles


def _scatter_body(pad_start_ref, pad_len_ref, pos_ref, h_ref, xs_ref, zeros, sem, zsem):
    tm = h_ref.shape[0]
    n_exp = pad_start_ref.shape[0] - 1

    @pl.when(pl.program_id(0) == 0)
    def _():
        zeros[...] = jnp.zeros_like(zeros)
        half = zeros.shape[0]
        sizes = [half >> i for i in range(half.bit_length())]
        tail_start = pad_start_ref[n_exp]
        tail_halves = pad_len_ref[n_exp] // half

        def tail_copy(i):
            return pltpu.make_async_copy(zeros, xs_ref.at[pl.ds(tail_start + i * half, half)], zsem)

        lax.fori_loop(0, tail_halves, lambda i, c: (tail_copy(i).start(), c)[1], 0)
        lax.fori_loop(0, tail_halves, lambda i, c: (tail_copy(i).wait(), c)[1], 0)

        def pieces(e, wait):
            start = pad_start_ref[e]
            n = pad_len_ref[e]
            for size in sizes:
                has = (n & size) != 0
                copy = pltpu.make_async_copy(zeros.at[pl.ds(0, size)], xs_ref.at[pl.ds(start, size)], zsem)

                @pl.when(has)
                def _():
                    copy.wait() if wait else copy.start()

                start = start + jnp.where(has, size, 0)

        lax.fori_loop(0, n_exp, lambda e, c: (pieces(e, False), c)[1], 0)
        lax.fori_loop(0, n_exp, lambda e, c: (pieces(e, True), c)[1], 0)

    def issue(r, carry):
        for k in range(TOP_K):
            pltpu.make_async_copy(h_ref.at[r], xs_ref.at[pos_ref[k, r]], sem).start(
                priority=k % 2)
        return carry

    lax.fori_loop(0, tm, issue, 0)
    all_rows = xs_ref.at[pl.ds(0, TOP_K * tm)]
    pltpu.make_async_copy(all_rows, all_rows, sem).wait()


def _scatter(pad_start, pad_len, pos, hp, slots):
    rows = hp.shape[0]
    row_tile = hp.shape[1:]
    tm = SEQ_TILE
    assert EXPERT_TILE & (EXPERT_TILE - 1) == 0
    grid_spec = pltpu.PrefetchScalarGridSpec(
        num_scalar_prefetch=2,
        grid=(rows // tm,),
        in_specs=[
            pl.BlockSpec((TOP_K, tm), lambda i, ps, pn: (0, i), memory_space=pltpu.SMEM),
            pl.BlockSpec((tm,) + row_tile, lambda i, ps, pn: (i, 0, 0)),
        ],
        out_specs=pl.BlockSpec(memory_space=pl.ANY),
        scratch_shapes=[pltpu.VMEM((EXPERT_TILE // 2,) + row_tile, U32),
                        pltpu.SemaphoreType.DMA(()), pltpu.SemaphoreType.DMA(())],
    )
    return pl.pallas_call(
        _scatter_body,
        grid_spec=grid_spec,
        out_shape=jax.ShapeDtypeStruct((slots,) + row_tile, U32),
        compiler_params=_cparams("arbitrary"),
        name="moe_dispatch",
    )(pad_start, pad_len, pos, hp)


def _ffn_rows(xw, wgu_ref, wd_ref):
    x = jnp.concatenate([_unpack_lo(xw).astype(BF16), _unpack_hi(xw).astype(BF16)], axis=1)
    gu = jnp.dot(x, wgu_ref, preferred_element_type=F32)
    f = gu.shape[1] // 2
    hdn = (_silu(gu[:, :f]) * gu[:, f:]).astype(BF16)
    return jnp.dot(hdn, wd_ref, preferred_element_type=F32)


def _experts_body(te_ref, ts_ref, tr_ref, xs_ref, wg_ref, wu_ref, wd_ref, ys_ref, wgu_bf, wd_bf):
    del ts_ref
    t = pl.program_id(0)
    n_real = tr_ref[t]
    tm = xs_ref.shape[0]
    half = tm // 2
    f = wg_ref.shape[-1]

    @pl.when(jnp.logical_or(t == 0, te_ref[t] != te_ref[jnp.maximum(t - 1, 0)]))
    def _():
        wgu_bf[:, :f] = wg_ref[0, 0].astype(BF16)
        wgu_bf[:, f:] = wu_ref[0, 0].astype(BF16)
        wd_bf[...] = wd_ref[0, 0].astype(BF16)

    def ffn(rows):
        part = pl.ds(0, rows)
        y = _ffn_rows(_load_row_tiles(xs_ref.at[part]), wgu_bf[...], wd_bf[...])
        _store_row_tiles(ys_ref.at[part], _pack_bf16_pairs(y))

    @pl.when(n_real > half)
    def _():
        ffn(tm)

    @pl.when(jnp.logical_and(n_real > 0, n_real <= half))
    def _():
        ffn(half)
        ys_ref[pl.ds(half, half)] = jnp.zeros((half,) + ys_ref.shape[1:], ys_ref.dtype)

    @pl.when(n_real == 0)
    def _():
        ys_ref[...] = jnp.zeros_like(ys_ref)


def _experts(tile_expert, tile_src, tile_rows, xs, w_gate, w_up, w_down, layer, max_tiles):
    slots = xs.shape[0]
    row_tile = xs.shape[1:]
    tm = EXPERT_TILE
    d, f = w_gate.shape[2], w_gate.shape[3]
    grid_spec = pltpu.PrefetchScalarGridSpec(
        num_scalar_prefetch=3,
        grid=(max_tiles,),
        in_specs=[
            pl.BlockSpec((tm,) + row_tile, lambda t, te, ts, tv: (ts[t], 0, 0)),
            pl.BlockSpec((1, 1, d, f), lambda t, te, ts, tv: (layer, te[t], 0, 0)),
            pl.BlockSpec((1, 1, d, f), lambda t, te, ts, tv: (layer, te[t], 0, 0)),
            pl.BlockSpec((1, 1, f, d), lambda t, te, ts, tv: (layer, te[t], 0, 0)),
        ],
        out_specs=pl.BlockSpec((tm,) + row_tile, lambda t, te, ts, tv: (t, 0, 0)),
        scratch_shapes=[pltpu.VMEM((d, 2 * f), BF16), pltpu.VMEM((f, d), BF16)],
    )
    return pl.pallas_call(
        _experts_body,
        grid_spec=grid_spec,
        out_shape=jax.ShapeDtypeStruct((slots,) + row_tile, U32),
        compiler_params=_cparams("arbitrary"),
        name="moe_experts",
    )(tile_expert, tile_src, tile_rows, xs, w_gate, w_up, w_down)


def _combine_body(pos_ref, wt_ref, hp_ref, ys_ref, wgu_ref, wd_ref, x_ref, g_ref, gt_ref, o_ref,
                  gbuf, sem):
    tm = hp_ref.shape[0]

    def issue(r, carry):
        for k in range(TOP_K):
            pltpu.make_async_copy(ys_ref.at[pos_ref[k, r]], gbuf.at[k, r], sem).start(
                priority=k % 2)
        return carry

    lax.fori_loop(0, tm, issue, 0)
    y = _ffn_rows(_load_row_tiles(hp_ref), wgu_ref[...], wd_ref[...])
    all_rows = ys_ref.at[pl.ds(0, TOP_K * tm)]
    pltpu.make_async_copy(all_rows, all_rows, sem).wait()
    half = y.shape[1] // 2
    lo, hi = y[:, :half], y[:, half:]
    wt = wt_ref[...]
    for k in range(TOP_K):
        wk = wt[:, k:k + 1]
        gk = _load_row_tiles(gbuf.at[k])
        lo = lo + wk * _unpack_lo(gk)
        hi = hi + wk * _unpack_hi(gk)
    y = jnp.concatenate([lo, hi], axis=1)
    o_ref[...] = x_ref[...] + gt_ref[0] * _rms(y, g_ref[...])


def _combine(pos, wts, hp, ys, w_sgu, w_sd, xa, g, mod3, gt_idx, rows, seq, n_batch, in_place):
    d = xa.shape[1]
    row_tile = hp.shape[1:]
    tm = SEQ_TILE
    tps = seq // tm
    grp = lambda i: _group_of_tile(i, tps, n_batch)
    return pl.pallas_call(
        _combine_body,
        grid=(rows // tm,),
        in_specs=[
            pl.BlockSpec((TOP_K, tm), lambda i: (0, i), memory_space=pltpu.SMEM),
            pl.BlockSpec((tm, TOP_K), lambda i: (i, 0)),
            pl.BlockSpec((tm,) + row_tile, lambda i: (i, 0, 0)),
            pl.BlockSpec(memory_space=pl.ANY),
            pl.BlockSpec(w_sgu.shape, lambda i: (0, 0), pipeline_mode=pl.Buffered(1)),
            pl.BlockSpec(w_sd.shape, lambda i: (0, 0), pipeline_mode=pl.Buffered(1)),
            pl.BlockSpec((tm, d), lambda i: (i, 0)),
            pl.BlockSpec((1, d), lambda i: (0, 0)),
            pl.BlockSpec((1, 1, d), lambda i: (grp(i), 0, gt_idx)),
        ],
        out_specs=pl.BlockSpec((tm, d), lambda i: (i, 0)),
        out_shape=jax.ShapeDtypeStruct(xa.shape if in_place else (rows, d), F32),
        scratch_shapes=[pltpu.VMEM((TOP_K, tm) + row_tile, U32), pltpu.SemaphoreType.DMA(())],
        input_output_aliases={6: 0} if in_place else {},
        compiler_params=_cparams("arbitrary"),
        name="moe_combine",
    )(pos, wts, hp, ys, w_sgu, w_sd, xa, g, mod3)


def kernel(x, c, ctx, c_ctx, w_ada, b_ada, norm_g, w_in, diff_lambda, g_subln, sg_ln_g, sg_ln_b, sg_w, sg_b, conv_w, w_branch, w_out, w_router, router_bias, w_exp_gate, w_exp_up, w_exp_down, w_sh_gate, w_sh_up, w_sh_down):
    n_batch, seq, d = x.shape
    ctx_len = ctx.shape[1]
    depth = w_ada.shape[0]
    mix_w = w_branch.shape[2]
    n_heads = mix_w // HEAD_W
    n_exp = w_router.shape[2]
    n_lat = n_batch * seq
    n_ctx = n_batch * ctx_len
    n_all = n_lat + n_ctx
    assert n_batch + 1 <= MOD_ROWS and seq % SEQ_TILE == 0 and ctx_len % SEQ_TILE == 0
    assert n_lat % seq == 0 and n_lat % ctx_len == 0 and mix_w // SG_GROUPS == LANES
    assert d // 2 == SUBLANES * LANES

    xa = (x.reshape(n_lat, d), ctx.reshape(n_ctx, d))
    if depth == 1:
        xa = _concat_rows(*xa)
    cond = jnp.zeros((MOD_ROWS, d), F32).at[:n_batch].set(c).at[n_batch].set(c_ctx)
    mod = _ada(cond, w_ada, b_ada)
    cos, sin = _rope_tables(seq)
    q_cb, k_cb, v_cb = 0, n_heads, 2 * n_heads
    gate_cb = (8 * mix_w) // d

    w_in_b = w_in.astype(BF16)
    w_branch_b = w_branch.astype(BF16)
    w_out_b = w_out.astype(BF16)
    n_in = w_in.shape[2]

    for l in range(depth):
        last = l == depth - 1
        lam_init = 0.8 - 0.6 * math.exp(-0.3 * l)
        mod3 = mod[l].reshape(MOD_ROWS, 1, N_MOD * d)
        ng = norm_g[l]
        rows = n_lat if last else n_all

        in_proj = functools.partial(_norm_matmul, xa, ng[0:1], mod3, 1, 0, w_in_b, l,
                                    seq=seq, n_batch=n_batch)
        if last:
            y = in_proj(n_lat, 0, n_in, 0, name="in_proj")
            yc = in_proj(n_ctx, n_lat, 2 * mix_w, mix_w, name="in_proj_ctx_kv")
            ctx_kv = (yc, 0, yc, n_heads, 0, ctx_len)
        else:
            y = in_proj(n_all, 0, n_in, 0, name="in_proj")
            ctx_kv = (y, k_cb, y, v_cb, n_lat // ctx_len, ctx_len)
        attn_o = _attention(y, q_cb, 0, y, k_cb, y, v_cb, 0, seq, seq, n_batch, n_heads,
                            diff_lambda[l], g_subln[l:l + 1], lam_init, ctx=ctx_kv, rope=(cos, sin),
                            name="attn_latent")
        if not last:
            attn_c = _attention(y, q_cb, n_lat // ctx_len, y, k_cb, y, v_cb, n_lat // ctx_len,
                                ctx_len, ctx_len, n_batch, n_heads,
                                diff_lambda[l], g_subln[l:l + 1], lam_init, name="attn_ctx")
            attn_o = jnp.concatenate([attn_o, attn_c], axis=0)
        sg_bias_map = jnp.repeat(sg_b[l].T, mix_w // SG_GROUPS, axis=1)
        sg_o, cv_o = _branches(y, rows, n_lat, seq, mix_w, 3, sg_ln_g[l:l + 1], sg_ln_b[l:l + 1],
                               sg_w[l].astype(BF16), sg_bias_map, conv_w[l])
        merged = _merge(attn_o, sg_o, cv_o, y, gate_cb, w_branch_b, l, rows)
        xa = _outproj(merged, w_out_b, l, xa, ng[1:2], mod3, 2, rows, seq, n_batch)

        hp, wt_t, idx_t, rank_t, cnt = _route(xa, ng[2:3], mod3, 4, 3, w_router[l].T, router_bias[l],
                                              rows, seq, n_batch)
        pos, pad, tile_expert, tile_src, tile_rows, max_tiles = _dispatch_plan(
            idx_t, rank_t, cnt[:, 0].astype(jnp.int32), EXPERT_TILE)
        xs = _scatter(*pad, pos, hp, max_tiles * EXPERT_TILE)
        ys = _experts(tile_expert, tile_src, tile_rows, xs, w_exp_gate, w_exp_up, w_exp_down, l, max_tiles)
        w_sgu = jnp.concatenate([w_sh_gate[l], w_sh_up[l]], axis=-1).astype(BF16)
        xa = _combine(pos, wt_t.T, hp, ys, w_sgu, w_sh_down[l].astype(BF16), xa, ng[3:4], mod3, 5,
                      rows, seq, n_batch, in_place=not last)

    return xa.reshape(n_batch, seq, d)
```

```python
import functools
import math

import jax
import jax.numpy as jnp
from jax import lax
from jax.experimental import pallas as pl
from jax.experimental.pallas import tpu as pltpu

GRID_W = 64
QK_DIM = 64
HEAD_W = 2 * QK_DIM
ROPE_BASE = 10000.0
SG_CHUNK = 128
SG_GROUPS = 8
TOP_K = 8
N_GROUPS = 8
TOPK_GROUPS = 4
ROUTED_SCALE = 2.5
NORM_EPS = 1e-6
N_MOD = 6

LANES = 128
SUBLANES = 8
VMEM_LIMIT = 56 * 1024 * 1024
SEQ_TILE = 256
EXPERT_TILE = 512
MOD_ROWS = 16
ATTN_SUB_ROWS = 128

F32 = jnp.float32
BF16 = jnp.bfloat16
U32 = jnp.uint32
HIGHEST = lax.Precision.HIGHEST


def _cparams(*sem):
    return pltpu.CompilerParams(dimension_semantics=sem, vmem_limit_bytes=VMEM_LIMIT)


def _pick_tile(n, cands):
    for c in cands:
        if n % c == 0:
            return c
    raise ValueError(f"no tile for {n} in {cands}")


def _sigmoid(x):
    return 1.0 / (1.0 + jnp.exp(-x))


def _silu(x):
    return x * _sigmoid(x)


def _gelu_tanh(x):
    c = math.sqrt(2.0 / math.pi)
    return x * (0.5 * (1.0 + jnp.tanh(c * (x + 0.044715 * (x * x * x)))))


def _rms(x, g):
    return x * lax.rsqrt(jnp.mean(x * x, axis=-1, keepdims=True) + NORM_EPS) * g


def _pack_bf16_pairs(x):
    w = x.shape[1] // 2
    bits = pltpu.bitcast(x.astype(BF16).astype(F32), U32)
    return (bits[:, :w] >> 16) | bits[:, w:]


def _unpack_lo(w):
    return pltpu.bitcast(w << 16, F32)


def _unpack_hi(w):
    return pltpu.bitcast(w & jnp.uint32(0xFFFF0000), F32)


def _store_row_tiles(ref, packed):
    m, s, _ = ref.shape
    flat = ref.reshape(m * s, LANES)
    for j in range(s):
        flat[pl.ds(j, m, stride=s), :] = packed[:, j * LANES:(j + 1) * LANES]


def _load_row_tiles(ref):
    m, s, _ = ref.shape
    flat = ref.reshape(m * s, LANES)
    return jnp.concatenate([flat[pl.ds(j, m, stride=s), :] for j in range(s)], axis=1)


def _concat_body(a_ref, b_ref, o_ref, *, a_tiles):
    i = pl.program_id(0)

    @pl.when(i < a_tiles)
    def _():
        o_ref[...] = a_ref[...]

    @pl.when(i >= a_tiles)
    def _():
        o_ref[...] = b_ref[...]


def _concat_rows(a, b):
    d = a.shape[1]
    tm = _pick_tile(math.gcd(a.shape[0], b.shape[0]), (512, 256))
    na, nb = a.shape[0] // tm, b.shape[0] // tm
    return pl.pallas_call(
        functools.partial(_concat_body, a_tiles=na),
        grid=(na + nb,),
        in_specs=[
            pl.BlockSpec((tm, d), lambda i: (jnp.minimum(i, na - 1), 0)),
            pl.BlockSpec((tm, d), lambda i: (jnp.maximum(i - na, 0), 0)),
        ],
        out_specs=pl.BlockSpec((tm, d), lambda i: (i, 0)),
        out_shape=jax.ShapeDtypeStruct((a.shape[0] + b.shape[0], d), a.dtype),
        compiler_params=_cparams("parallel"),
        name="concat_rows",
    )(a, b)


def _ada_body(c_ref, w_ref, b_ref, o_ref):
    a = _silu(c_ref[...])
    o_ref[0] = jnp.dot(a, w_ref[0], precision=HIGHEST, preferred_element_type=F32) + b_ref[0]


def _ada(cond, w_ada, b_ada):
    depth, d, n = w_ada.shape
    tn = _pick_tile(n, (1024, 512, 256, 128))
    return pl.pallas_call(
        _ada_body,
        grid=(depth, n // tn),
        in_specs=[
            pl.BlockSpec((MOD_ROWS, d), lambda l, j: (0, 0)),
            pl.BlockSpec((1, d, tn), lambda l, j: (l, 0, j)),
            pl.BlockSpec((1, 1, tn), lambda l, j: (l, 0, j)),
        ],
        out_specs=pl.BlockSpec((1, MOD_ROWS, tn), lambda l, j: (l, 0, j)),
        out_shape=jax.ShapeDtypeStruct((depth, MOD_ROWS, n), F32),
        compiler_params=_cparams("arbitrary", "arbitrary"),
        name="ada_mod",
    )(cond, w_ada, b_ada.reshape(depth, 1, n))


def _group_of_tile(i, tiles_per_seq, n_batch):
    return jnp.minimum(i // tiles_per_seq, n_batch)


def _stream_specs(stream, tm, tile_of):
    if not isinstance(stream, tuple):
        return [pl.BlockSpec((tm, stream.shape[1]), lambda *idx: (tile_of(*idx), 0))], None
    a, b = stream
    split = a.shape[0] // tm
    assert a.shape[0] % tm == 0 and b.shape[0] % tm == 0
    return [pl.BlockSpec((tm, a.shape[1]), lambda *idx: (jnp.minimum(tile_of(*idx), split - 1), 0)),
            pl.BlockSpec((tm, b.shape[1]), lambda *idx: (jnp.maximum(tile_of(*idx) - split, 0), 0),
                         pipeline_mode=pl.Buffered(1))], split


def _stream_tile(refs, tile, split):
    if split is None:
        return refs[0][...]
    return jnp.where(tile < split, refs[0][...], refs[1][...])


def _stream_args(stream):
    return list(stream) if isinstance(stream, tuple) else [stream]


def _norm_mm_body(*refs, split, tile0):
    n_src = 1 if split is None else 2
    g_ref, sc_ref, sh_ref, w_ref, o_ref, h_ref = refs[n_src:]

    @pl.when(pl.program_id(1) == 0)
    def _():
        y = _rms(_stream_tile(refs[:n_src], pl.program_id(0) + tile0, split), g_ref[...])
        h_ref[...] = (y * (1.0 + sc_ref[0]) + sh_ref[0]).astype(h_ref.dtype)

    o_ref[...] = jnp.dot(h_ref[...], w_ref[...], preferred_element_type=F32).astype(o_ref.dtype)


def _norm_matmul(stream, g, mod3, sc_idx, sh_idx, w, layer, rows, row_off, cols, col_off, seq, n_batch, name):
    d = w.shape[1]
    tm = _pick_tile(math.gcd(math.gcd(rows, row_off) if row_off else rows, seq), (1024, 512, 256))
    tn = _pick_tile(math.gcd(cols, col_off) if col_off else cols, (1024, 512, 256))
    ro, co = row_off // tm, col_off // tn
    tps = seq // tm
    grp = lambda i: _group_of_tile(i + ro, tps, n_batch)
    x_specs, split = _stream_specs(stream, tm, lambda i, j: i + ro)
    return pl.pallas_call(
        functools.partial(_norm_mm_body, split=split, tile0=ro),
        grid=(rows // tm, cols // tn),
        in_specs=x_specs + [
            pl.BlockSpec((1, d), lambda i, j: (0, 0)),
            pl.BlockSpec((1, 1, d), lambda i, j: (grp(i), 0, sc_idx)),
            pl.BlockSpec((1, 1, d), lambda i, j: (grp(i), 0, sh_idx)),
            pl.BlockSpec((None, d, tn), lambda i, j: (layer, 0, j + co)),
        ],
        out_specs=pl.BlockSpec((tm, tn), lambda i, j: (i, j)),
        out_shape=jax.ShapeDtypeStruct((rows, cols), BF16),
        scratch_shapes=[pltpu.VMEM((tm, d), BF16)],
        compiler_params=_cparams("arbitrary", "arbitrary"),
        name=name,
    )(*_stream_args(stream), g, mod3, mod3, w)


def _rope_apply(x, cos, sin):
    lane = lax.broadcasted_iota(jnp.int32, x.shape, 1)
    first_half = (lane % (QK_DIM // 2)) < (QK_DIM // 4)
    partner = jnp.where(first_half,
                        pltpu.roll(x, LANES - QK_DIM // 4, 1),
                        pltpu.roll(x, QK_DIM // 4, 1))
    return x * cos + partner * sin


def _rope_tables(seq):
    n_freq = QK_DIM // 4
    inv = ROPE_BASE ** (-jnp.arange(n_freq, dtype=F32) / n_freq)
    t = jnp.arange(seq)
    row = (t // GRID_W).astype(F32)
    col = (t % GRID_W).astype(F32)
    ang = jnp.stack([row[:, None] * inv, col[:, None] * inv], axis=1)
    cos = jnp.cos(ang)
    sin = jnp.sin(ang)
    cos64 = jnp.stack([cos, cos], axis=2).reshape(seq, QK_DIM)
    sin64 = jnp.stack([-sin, sin], axis=2).reshape(seq, QK_DIM)
    reps = LANES // QK_DIM
    return jnp.tile(cos64, (1, reps)), jnp.tile(sin64, (1, reps))


def _attn_body(*refs, two_seg, rope, lam_init):
    refs = list(refs)
    q_ref, k_ref, v_ref = refs[:3]
    del refs[:3]
    segs = []
    if two_seg:
        segs.append((refs[0], refs[1], False))
        del refs[:2]
    segs.append((k_ref, v_ref, rope))
    if rope:
        cos_ref, sin_ref = refs[:2]
        del refs[:2]
    dl_ref, g_ref, o_ref, kt_ref, v1_ref = refs
    tq = q_ref.shape[0]

    @pl.when(pl.program_id(2) == 0)
    def _():
        row = 0
        for ks_ref, vs_ref, rotate in segs:
            n = ks_ref.shape[0]
            ks = ks_ref[...].astype(F32)
            if rotate:
                ks = _rope_apply(ks, cos_ref[...], sin_ref[...])
            kt_ref[:, row:row + n] = ks.T.astype(BF16)
            v1_ref[row:row + n, :HEAD_W] = vs_ref[...]
            v1_ref[row:row + n, HEAD_W:] = jnp.ones(vs_ref.shape, BF16)
            row += n

    dl = dl_ref[...]
    lam = (jnp.exp(jnp.sum(dl[0:1] * dl[1:2], axis=-1, keepdims=True))
           - jnp.exp(jnp.sum(dl[2:3] * dl[3:4], axis=-1, keepdims=True)) + lam_init)
    q = q_ref[...]
    if rope:
        q_rows = pl.ds(pl.multiple_of(pl.program_id(2) * tq, tq), tq)
        q = _rope_apply(q.astype(F32), cos_ref[q_rows, :], sin_ref[q_rows, :]).astype(BF16)
    q = q * jnp.asarray(QK_DIM ** -0.5, BF16)
    lane = lax.broadcasted_iota(jnp.int32, q.shape, 1)
    zero = jnp.zeros_like(q)

    def softmax_v(qm):
        s = jnp.dot(qm, kt_ref[...], preferred_element_type=F32)
        m = jnp.max(s, axis=-1, keepdims=True)
        acc = jnp.dot(jnp.exp(s - m).astype(BF16), v1_ref[...], preferred_element_type=F32)
        return acc[:, :HEAD_W] / acc[:, HEAD_W:]

    q1 = jnp.where(lane < QK_DIM, q, zero)
    q2 = jnp.where(lane >= QK_DIM, q, zero)
    sub = min(q.shape[0], ATTN_SUB_ROWS)
    for r0 in range(0, q.shape[0], sub):
        rs = slice(r0, r0 + sub)
        o = softmax_v(q1[rs]) - lam * softmax_v(q2[rs])
        o_ref[rs, :] = (_rms(o, g_ref[...]) * (1.0 - lam_init)).astype(o_ref.dtype)


def _attention(q_arr, q_cb, q_rb, k_arr, k_cb, v_arr, v_cb, kv_rb, seq_q, seq_k, n_batch, n_heads,
               dl, g_sub, lam_init, ctx=None, rope=None, name="attn"):
    tq = _pick_tile(seq_q, (1024, 512, 256, 128))
    nq = seq_q // tq
    n_keys = seq_k + (ctx[5] if ctx is not None else 0)
    scratch = [pltpu.VMEM((HEAD_W, n_keys), BF16), pltpu.VMEM((n_keys, 2 * HEAD_W), BF16)]
    in_specs = [
        pl.BlockSpec((tq, HEAD_W), lambda b, h, i: ((q_rb + b) * nq + i, q_cb + h)),
        pl.BlockSpec((seq_k, HEAD_W), lambda b, h, i: (kv_rb + b, k_cb + h)),
        pl.BlockSpec((seq_k, HEAD_W), lambda b, h, i: (kv_rb + b, v_cb + h)),
    ]
    args = [q_arr, k_arr, v_arr]
    if ctx is not None:
        kc_arr, kc_cb, vc_arr, vc_cb, c_rb, seq_c = ctx
        in_specs += [
            pl.BlockSpec((seq_c, HEAD_W), lambda b, h, i: (c_rb + b, kc_cb + h)),
            pl.BlockSpec((seq_c, HEAD_W), lambda b, h, i: (c_rb + b, vc_cb + h)),
        ]
        args += [kc_arr, vc_arr]
    if rope is not None:
        assert seq_q == seq_k
        in_specs += [pl.BlockSpec((seq_k, LANES), lambda b, h, i: (0, 0))] * 2
        args += list(rope)
    in_specs += [
        pl.BlockSpec(dl.shape, lambda b, h, i: (0, 0)),
        pl.BlockSpec((1, HEAD_W), lambda b, h, i: (0, 0)),
    ]
    args += [dl, g_sub]
    return pl.pallas_call(
        functools.partial(_attn_body, two_seg=ctx is not None, rope=rope is not None, lam_init=lam_init),
        grid=(n_batch, n_heads, nq),
        in_specs=in_specs,
        out_specs=pl.BlockSpec((tq, HEAD_W), lambda b, h, i: (b * nq + i, h)),
        out_shape=jax.ShapeDtypeStruct((n_batch * seq_q, n_heads * HEAD_W), BF16),
        scratch_shapes=scratch,
        compiler_params=_cparams("arbitrary", "arbitrary", "arbitrary"),
        name=name,
    )(*args)


def _branches_body(u_ref, v_ref, x_ref, b_ref, c_ref, xp_ref, cp_ref, xn_ref, cn_ref,
                   lng_ref, lnb_ref, sgw_ref, sgb_ref, cw_ref, sg_o_ref, cv_o_ref,
                   *, lat_tiles, tiles_per_seq):
    i = pl.program_id(0)
    tm, w = u_ref.shape
    u = _gelu_tanh(u_ref[...].astype(F32))
    v = _gelu_tanh(v_ref[...].astype(F32))
    mu = jnp.mean(v, axis=-1, keepdims=True)
    dv = v - mu
    v = dv * lax.rsqrt(jnp.mean(dv * dv, axis=-1, keepdims=True) + NORM_EPS)
    v = (v * lng_ref[...] + lnb_ref[...]).astype(BF16)
    gw = w // SG_GROUPS
    for c in range(tm // SG_CHUNK):
        rs = slice(c * SG_CHUNK, (c + 1) * SG_CHUNK)
        for g in range(SG_GROUPS):
            cs = slice(g * gw, (g + 1) * gw)
            mixed = jnp.dot(sgw_ref[g], v[rs, cs], preferred_element_type=F32) + sgb_ref[:, cs]
            sg_o_ref[rs, cs] = (u[rs, cs] * mixed).astype(sg_o_ref.dtype)
    is_lat = i < lat_tiles
    seq_first = jnp.logical_or(jnp.logical_not(is_lat), i % tiles_per_seq == 0)
    seq_last = jnp.logical_or(jnp.logical_not(is_lat), i % tiles_per_seq == tiles_per_seq - 1)
    z = c_ref[...].astype(F32) * x_ref[...].astype(F32)
    hp = xp_ref.shape[0]
    z_prev = (cp_ref[...].astype(F32) * xp_ref[...].astype(F32))[hp - 1:hp, :]
    z_next = (cn_ref[...].astype(F32) * xn_ref[...].astype(F32))[0:1, :]
    z_prev = jnp.where(seq_first, 0.0, z_prev)
    z_next = jnp.where(seq_last, 0.0, z_next)
    row = lax.broadcasted_iota(jnp.int32, z.shape, 0)
    z_dn = jnp.where(row == 0, z_prev, pltpu.roll(z, 1, 0))
    z_up = jnp.where(row == tm - 1, z_next, pltpu.roll(z, tm - 1, 0))
    y = cw_ref[0:1, :] * z_dn + cw_ref[1:2, :] * z + cw_ref[2:3, :] * z_up
    cv_o_ref[...] = (b_ref[...].astype(F32) * y).astype(cv_o_ref.dtype)


def _branches(y, rows, lat_rows, seq, mix_w, col0, sg_ln_g, sg_ln_b, sg_w, sg_bias_map, conv_w):
    tm = SEQ_TILE
    halo = 16
    hb = tm // halo
    n_tiles = rows // tm
    last_hblk = rows // halo - 1
    cur = lambda c: pl.BlockSpec((tm, mix_w), lambda i: (i, col0 + c))
    prev = lambda c: pl.BlockSpec((halo, mix_w), lambda i: (jnp.maximum(i * hb - 1, 0), col0 + c))
    nxt = lambda c: pl.BlockSpec((halo, mix_w), lambda i: (jnp.minimum((i + 1) * hb, last_hblk), col0 + c))
    full = lambda a: pl.BlockSpec(a.shape, lambda i: (0,) * a.ndim)
    consts = [sg_ln_g, sg_ln_b, sg_w, sg_bias_map, conv_w]
    out = jax.ShapeDtypeStruct((rows, mix_w), BF16)
    return pl.pallas_call(
        functools.partial(_branches_body, lat_tiles=lat_rows // tm, tiles_per_seq=seq // tm),
        grid=(n_tiles,),
        in_specs=[cur(0), cur(1), cur(2), cur(3), cur(4), prev(2), prev(4), nxt(2), nxt(4)]
                 + [full(a) for a in consts],
        out_specs=[pl.BlockSpec((tm, mix_w), lambda i: (i, 0))] * 2,
        out_shape=[out, out],
        compiler_params=_cparams("parallel"),
        name="sg_conv_branches",
    )(y, y, y, y, y, y, y, y, y, *consts)


def _merge_body(a_ref, s_ref, c_ref, g0_ref, g1_ref, g2_ref, wb_ref, o_ref):
    acc = None
    for g, (br, gl) in enumerate(((a_ref, g0_ref), (s_ref, g1_ref), (c_ref, g2_ref))):
        t = _sigmoid(gl[...].astype(F32)) * jnp.dot(br[...], wb_ref[g], preferred_element_type=F32)
        acc = t if acc is None else acc + t
    o_ref[...] = acc.astype(o_ref.dtype)


def _merge(attn_o, sg_o, cv_o, y, gate_cb, w_branch, layer, rows):
    n_br, mix_w, d = w_branch.shape[1:]
    tm = _pick_tile(rows, (512, 256))
    br = pl.BlockSpec((tm, mix_w), lambda i: (i, 0))
    gate = lambda g: pl.BlockSpec((tm, d), lambda i: (i, gate_cb + g))
    return pl.pallas_call(
        _merge_body,
        grid=(rows // tm,),
        in_specs=[br, br, br, gate(0), gate(1), gate(2),
                  pl.BlockSpec((None, n_br, mix_w, d), lambda i: (layer, 0, 0, 0),
                               pipeline_mode=pl.Buffered(1))],
        out_specs=pl.BlockSpec((tm, d), lambda i: (i, 0)),
        out_shape=jax.ShapeDtypeStruct((rows, d), BF16),
        compiler_params=_cparams("parallel"),
        name="branch_merge",
    )(attn_o, sg_o, cv_o, y, y, y, w_branch)


def _outproj_body(m_ref, w_ref, g_ref, gt_ref, *refs, split):
    x_refs, o_ref = refs[:-1], refs[-1]
    mix = jnp.dot(m_ref[...], w_ref[...], preferred_element_type=F32)
    x = _stream_tile(x_refs, pl.program_id(0), split)
    o_ref[...] = x + gt_ref[0] * _rms(mix, g_ref[...])


def _outproj(merged, w_out, layer, stream, g, mod3, gt_idx, rows, seq, n_batch):
    d = w_out.shape[1]
    tm = _pick_tile(seq, (512, 256))
    tps = seq // tm
    grp = lambda i: _group_of_tile(i, tps, n_batch)
    x_specs, split = _stream_specs(stream, tm, lambda i: i)
    if split is None:
        out_rows, alias = stream.shape[0], {4: 0}
    else:
        out_rows, alias = rows, {}
        assert rows == stream[0].shape[0] + stream[1].shape[0]
    return pl.pallas_call(
        functools.partial(_outproj_body, split=split),
        grid=(rows // tm,),
        in_specs=[
            pl.BlockSpec((tm, d), lambda i: (i, 0)),
            pl.BlockSpec((None, d, d), lambda i: (layer, 0, 0), pipeline_mode=pl.Buffered(1)),
            pl.BlockSpec((1, d), lambda i: (0, 0)),
            pl.BlockSpec((1, 1, d), lambda i: (grp(i), 0, gt_idx)),
        ] + x_specs,
        out_specs=pl.BlockSpec((tm, d), lambda i: (i, 0)),
        out_shape=jax.ShapeDtypeStruct((out_rows, d), F32),
        input_output_aliases=alias,
        compiler_params=_cparams("parallel"),
        name="out_proj_residual",
    )(merged, w_out, g, mod3, *_stream_args(stream))


def _top1_mask(vals, idx, n, axis):
    m = jnp.max(vals, axis=axis, keepdims=True)
    first = jnp.min(jnp.where(vals == m, idx, n), axis=axis, keepdims=True)
    return idx == first, m, first


def _route_body(x_ref, g_ref, sc_ref, sh_ref, wr_ref, rb_ref, hp_ref, wt_ref, ix_ref, rk_ref, cnt_ref):
    h2 = _rms(x_ref[...], g_ref[...]) * (1.0 + sc_ref[0]) + sh_ref[0]
    _store_row_tiles(hp_ref, _pack_bf16_pairs(h2))
    n_exp = wr_ref.shape[0]
    tm = x_ref.shape[0]
    per_group = n_exp // N_GROUPS
    logits = lax.dot_general(wr_ref[...], h2, (((1,), (1,)), ((), ())),
                             precision=HIGHEST, preferred_element_type=F32)
    scores = _sigmoid(logits)
    choice = scores + rb_ref[...]
    neg = -jnp.inf
    ji = lax.broadcasted_iota(jnp.int32, (per_group, tm), 0)
    groups, gs = [], []
    for g in range(N_GROUPS):
        cg = choice[g * per_group:(g + 1) * per_group, :]
        hit, m1, _ = _top1_mask(cg, ji, per_group, 0)
        m2 = jnp.max(jnp.where(hit, neg, cg), axis=0, keepdims=True)
        groups.append(cg)
        gs.append(m1 + m2)
    kept = []
    for g in range(N_GROUPS):
        beaten_by = jnp.zeros((1, tm), jnp.int32)
        for o in range(N_GROUPS):
            if o != g:
                beats = (gs[o] >= gs[g]) if o < g else (gs[o] > gs[g])
                beaten_by = beaten_by + beats.astype(jnp.int32)
        kept.append(jnp.where(beaten_by < TOPK_GROUPS, groups[g], neg))
    cm = jnp.concatenate(kept, axis=0)
    ei = lax.broadcasted_iota(jnp.int32, cm.shape, 0)
    ws, ids, hits = [], [], []
    sel = jnp.zeros(cm.shape, F32)
    for _ in range(TOP_K):
        hit, _, first = _top1_mask(cm, ei, n_exp, 0)
        ws.append(jnp.sum(jnp.where(hit, scores, 0.0), axis=0, keepdims=True))
        ids.append(first)
        hits.append(hit)
        sel = sel + jnp.where(hit, 1.0, 0.0)
        cm = jnp.where(hit, neg, cm)
    w = jnp.concatenate(ws, axis=0)
    wt_ref[...] = w / jnp.sum(w, axis=0, keepdims=True) * ROUTED_SCALE
    ix_ref[...] = jnp.concatenate(ids, axis=0)

    @pl.when(pl.program_id(0) == 0)
    def _():
        cnt_ref[...] = jnp.zeros_like(cnt_ref)

    before = (lax.broadcasted_iota(jnp.int32, (tm, tm), 0)
              < lax.broadcasted_iota(jnp.int32, (tm, tm), 1))
    within = jnp.dot(sel.astype(BF16), jnp.where(before, 1.0, 0.0).astype(BF16),
                     preferred_element_type=F32)
    rank = within + cnt_ref[:, 0:1]
    rk_ref[...] = jnp.concatenate(
        [jnp.sum(jnp.where(h, rank, 0.0), axis=0, keepdims=True) for h in hits], axis=0).astype(jnp.int32)
    cnt_ref[...] = cnt_ref[...] + jnp.sum(sel, axis=1, keepdims=True)


def _route(xa, g, mod3, sc_idx, sh_idx, w_router_t, router_bias, rows, seq, n_batch):
    d = xa.shape[1]
    n_exp = w_router_t.shape[0]
    tm = SEQ_TILE
    tps = seq // tm
    grp = lambda i: _group_of_tile(i, tps, n_batch)
    return pl.pallas_call(
        _route_body,
        grid=(rows // tm,),
        in_specs=[
            pl.BlockSpec((tm, d), lambda i: (i, 0)),
            pl.BlockSpec((1, d), lambda i: (0, 0)),
            pl.BlockSpec((1, 1, d), lambda i: (grp(i), 0, sc_idx)),
            pl.BlockSpec((1, 1, d), lambda i: (grp(i), 0, sh_idx)),
            pl.BlockSpec((n_exp, d), lambda i: (0, 0)),
            pl.BlockSpec((n_exp, 1), lambda i: (0, 0)),
        ],
        out_specs=[
            pl.BlockSpec((tm, SUBLANES, LANES), lambda i: (i, 0, 0)),
            pl.BlockSpec((TOP_K, tm), lambda i: (0, i)),
            pl.BlockSpec((TOP_K, tm), lambda i: (0, i)),
            pl.BlockSpec((TOP_K, tm), lambda i: (0, i)),
            pl.BlockSpec((n_exp, LANES), lambda i: (0, 0)),
        ],
        out_shape=[
            jax.ShapeDtypeStruct((rows, SUBLANES, LANES), U32),
            jax.ShapeDtypeStruct((TOP_K, rows), F32),
            jax.ShapeDtypeStruct((TOP_K, rows), jnp.int32),
            jax.ShapeDtypeStruct((TOP_K, rows), jnp.int32),
            jax.ShapeDtypeStruct((n_exp, LANES), F32),
        ],
        compiler_params=_cparams("arbitrary"),
        name="moe_norm_route",
    )(xa, g, mod3, mod3, w_router_t, router_bias.reshape(n_exp, 1))


def _dispatch_plan(idx_t, rank_t, counts, tile):
    n_exp = counts.shape[0]
    rows = idx_t.shape[1]
    tiles_e = (counts + tile - 1) // tile
    tile_end = jnp.cumsum(tiles_e)
    offs = (tile_end - tiles_e) * tile
    onehot = idx_t[None, :, :] == jnp.arange(n_exp, dtype=jnp.int32)[:, None, None]
    pos = rank_t + jnp.sum(jnp.where(onehot, offs[:, None, None], 0), axis=0)
    max_tiles = (TOP_K * rows) // tile + n_exp
    t = jnp.arange(max_tiles, dtype=jnp.int32)
    used = tile_end[-1]
    t_eff = jnp.minimum(t, used - 1)
    tile_expert = jnp.minimum(jnp.sum(t_eff[:, None] >= tile_end[None, :], axis=1), n_exp - 1)
    used_rows = used * tile
    pad = (jnp.append(offs + counts, used_rows).astype(jnp.int32),
           jnp.append(tiles_e * tile - counts, max_tiles * tile - used_rows).astype(jnp.int32))
    own = tile_expert[:, None] == jnp.arange(n_exp, dtype=jnp.int32)[None, :]
    last_row = jnp.sum(jnp.where(own, (offs + counts)[None, :], 0), axis=1)
    tile_rows = jnp.where(t < used, jnp.clip(last_row - t * tile, 0, tile), 0)
    first = jnp.logical_or(t == 0, tile_expert != jnp.roll(tile_expert, 1))
    slot = (jnp.cumsum(first.astype(jnp.int32)) - 1) % 2
    after = jnp.sum(jnp.where(own, tile_end[None, :], 0), axis=1)
    nxt = jnp.where(after < used, jnp.take(tile_expert, jnp.minimum(after, max_tiles - 1)), -1)
    plan = tuple(a.astype(jnp.int32) for a in (tile_expert, t_eff, tile_rows, first, nxt, slot))
    return pos.astype(jnp.int32), pad, plan, max_tiles


def _scatter_body(pad_start_ref, pad_len_ref, pos_ref, h_ref, xs_ref, zeros, sem, zsem):
    tm = h_ref.shape[0]
    n_exp = pad_start_ref.shape[0] - 1

    @pl.when(pl.program_id(0) == 0)
    def _():
        zeros[...] = jnp.zeros_like(zeros)
        half = zeros.shape[0]
        sizes = [half >> i for i in range(half.bit_length())]
        tail_start = pad_start_ref[n_exp]
        tail_halves = pad_len_ref[n_exp] // half

        def tail_copy(i):
            return pltpu.make_async_copy(zeros, xs_ref.at[pl.ds(tail_start + i * half, half)], zsem)

        lax.fori_loop(0, tail_halves, lambda i, c: (tail_copy(i).start(), c)[1], 0)
        lax.fori_loop(0, tail_halves, lambda i, c: (tail_copy(i).wait(), c)[1], 0)

        def pieces(e, wait):
            start = pad_start_ref[e]
            n = pad_len_ref[e]
            for size in sizes:
                has = (n & size) != 0
                copy = pltpu.make_async_copy(zeros.at[pl.ds(0, size)], xs_ref.at[pl.ds(start, size)], zsem)

                @pl.when(has)
                def _():
                    copy.wait() if wait else copy.start()

                start = start + jnp.where(has, size, 0)

        lax.fori_loop(0, n_exp, lambda e, c: (pieces(e, False), c)[1], 0)
        lax.fori_loop(0, n_exp, lambda e, c: (pieces(e, True), c)[1], 0)

    def issue(r, carry):
        for k in range(TOP_K):
            pltpu.make_async_copy(h_ref.at[r], xs_ref.at[pos_ref[k, r]], sem).start(
                priority=k % 2)
        return carry

    lax.fori_loop(0, tm, issue, 0)
    all_rows = xs_ref.at[pl.ds(0, TOP_K * tm)]
    pltpu.make_async_copy(all_rows, all_rows, sem).wait()


def _scatter(pad_start, pad_len, pos, hp, slots):
    rows = hp.shape[0]
    row_tile = hp.shape[1:]
    tm = SEQ_TILE
    assert EXPERT_TILE & (EXPERT_TILE - 1) == 0
    grid_spec = pltpu.PrefetchScalarGridSpec(
        num_scalar_prefetch=2,
        grid=(rows // tm,),
        in_specs=[
            pl.BlockSpec((TOP_K, tm), lambda i, ps, pn: (0, i), memory_space=pltpu.SMEM),
            pl.BlockSpec((tm,) + row_tile, lambda i, ps, pn: (i, 0, 0)),
        ],
        out_specs=pl.BlockSpec(memory_space=pl.ANY),
        scratch_shapes=[pltpu.VMEM((EXPERT_TILE // 2,) + row_tile, U32),
                        pltpu.SemaphoreType.DMA(()), pltpu.SemaphoreType.DMA(())],
    )
    return pl.pallas_call(
        _scatter_body,
        grid_spec=grid_spec,
        out_shape=jax.ShapeDtypeStruct((slots,) + row_tile, U32),
        compiler_params=_cparams("arbitrary"),
        name="moe_dispatch",
    )(pad_start, pad_len, pos, hp)


def _ffn_rows(xw, wgu_ref, wd_ref):
    x = jnp.concatenate([_unpack_lo(xw).astype(BF16), _unpack_hi(xw).astype(BF16)], axis=1)
    gu = jnp.dot(x, wgu_ref, preferred_element_type=F32)
    f = gu.shape[1] // 2
    hdn = (_silu(gu[:, :f]) * gu[:, f:]).astype(BF16)
    return jnp.dot(hdn, wd_ref, preferred_element_type=F32)


def _experts_body(te_ref, ts_ref, tr_ref, first_ref, next_ref, slot_ref,
                  xs_ref, wg_hbm, wu_hbm, wd_hbm, ys_ref,
                  wg_buf, wu_buf, wd_buf, wgu_bf, wd_bf, sems, *, layer):
    del ts_ref
    t = pl.program_id(0)
    n_real = tr_ref[t]
    tm = xs_ref.shape[0]
    half = tm // 2
    f = wg_buf.shape[-1]

    def weight_copies(e, s):
        return [pltpu.make_async_copy(src.at[layer, e], dst.at[s], sems.at[s])
                for src, dst in ((wg_hbm, wg_buf), (wu_hbm, wu_buf), (wd_hbm, wd_buf))]

    @pl.when(t == 0)
    def _():
        for c in weight_copies(te_ref[0], slot_ref[0]):
            c.start()

    @pl.when(first_ref[t] == 1)
    def _():
        s = slot_ref[t]
        for c in weight_copies(te_ref[t], s):
            c.wait()
        wgu_bf[:, :f] = wg_buf[s].astype(BF16)
        wgu_bf[:, f:] = wu_buf[s].astype(BF16)
        wd_bf[...] = wd_buf[s].astype(BF16)

        @pl.when(next_ref[t] >= 0)
        def _():
            for c in weight_copies(next_ref[t], 1 - s):
                c.start()

    def ffn(rows):
        part = pl.ds(0, rows)
        y = _ffn_rows(_load_row_tiles(xs_ref.at[part]), wgu_bf[...], wd_bf[...])
        _store_row_tiles(ys_ref.at[part], _pack_bf16_pairs(y))

    @pl.when(n_real > half)
    def _():
        ffn(tm)

    @pl.when(jnp.logical_and(n_real > 0, n_real <= half))
    def _():
        ffn(half)
        ys_ref[pl.ds(half, half)] = jnp.zeros((half,) + ys_ref.shape[1:], ys_ref.dtype)

    @pl.when(n_real == 0)
    def _():
        ys_ref[...] = jnp.zeros_like(ys_ref)


def _experts(tile_plan, xs, w_gate, w_up, w_down, layer, max_tiles):
    slots = xs.shape[0]
    row_tile = xs.shape[1:]
    tm = EXPERT_TILE
    d, f = w_gate.shape[2], w_gate.shape[3]
    grid_spec = pltpu.PrefetchScalarGridSpec(
        num_scalar_prefetch=len(tile_plan),
        grid=(max_tiles,),
        in_specs=[
            pl.BlockSpec((tm,) + row_tile, lambda t, te, ts, *_: (ts[t], 0, 0)),
            pl.BlockSpec(memory_space=pl.ANY),
            pl.BlockSpec(memory_space=pl.ANY),
            pl.BlockSpec(memory_space=pl.ANY),
        ],
        out_specs=pl.BlockSpec((tm,) + row_tile, lambda t, *_: (t, 0, 0)),
        scratch_shapes=[pltpu.VMEM((2, d, f), F32), pltpu.VMEM((2, d, f), F32), pltpu.VMEM((2, f, d), F32),
                        pltpu.VMEM((d, 2 * f), BF16), pltpu.VMEM((f, d), BF16),
                        pltpu.SemaphoreType.DMA((2,))],
    )
    return pl.pallas_call(
        functools.partial(_experts_body, layer=layer),
        grid_spec=grid_spec,
        out_shape=jax.ShapeDtypeStruct((slots,) + row_tile, U32),
        compiler_params=_cparams("arbitrary"),
        name="moe_experts",
    )(*tile_plan, xs, w_gate, w_up, w_down)


def _combine_body(pos_ref, wt_ref, hp_ref, ys_ref, wgu_ref, wd_ref, x_ref, g_ref, gt_ref, o_ref,
                  gbuf, sem):
    tm = hp_ref.shape[0]

    def issue(r, carry):
        for k in range(TOP_K):
            pltpu.make_async_copy(ys_ref.at[pos_ref[k, r]], gbuf.at[k, r], sem).start(
                priority=k % 2)
        return carry

    lax.fori_loop(0, tm, issue, 0)
    y = _ffn_rows(_load_row_tiles(hp_ref), wgu_ref[...], wd_ref[...])
    all_rows = ys_ref.at[pl.ds(0, TOP_K * tm)]
    pltpu.make_async_copy(all_rows, all_rows, sem).wait()
    half = y.shape[1] // 2
    lo, hi = y[:, :half], y[:, half:]
    wt = wt_ref[...]
    for k in range(TOP_K):
        wk = wt[:, k:k + 1]
        gk = _load_row_tiles(gbuf.at[k])
        lo = lo + wk * _unpack_lo(gk)
        hi = hi + wk * _unpack_hi(gk)
    y = jnp.concatenate([lo, hi], axis=1)
    o_ref[...] = x_ref[...] + gt_ref[0] * _rms(y, g_ref[...])


def _combine(pos, wts, hp, ys, w_sgu, w_sd, xa, g, mod3, gt_idx, rows, seq, n_batch, in_place):
    d = xa.shape[1]
    row_tile = hp.shape[1:]
    tm = SEQ_TILE
    tps = seq // tm
    grp = lambda i: _group_of_tile(i, tps, n_batch)
    return pl.pallas_call(
        _combine_body,
        grid=(rows // tm,),
        in_specs=[
            pl.BlockSpec((TOP_K, tm), lambda i: (0, i), memory_space=pltpu.SMEM),
            pl.BlockSpec((tm, TOP_K), lambda i: (i, 0)),
            pl.BlockSpec((tm,) + row_tile, lambda i: (i, 0, 0)),
            pl.BlockSpec(memory_space=pl.ANY),
            pl.BlockSpec(w_sgu.shape, lambda i: (0, 0), pipeline_mode=pl.Buffered(1)),
            pl.BlockSpec(w_sd.shape, lambda i: (0, 0), pipeline_mode=pl.Buffered(1)),
            pl.BlockSpec((tm, d), lambda i: (i, 0)),
            pl.BlockSpec((1, d), lambda i: (0, 0)),
            pl.BlockSpec((1, 1, d), lambda i: (grp(i), 0, gt_idx)),
        ],
        out_specs=pl.BlockSpec((tm, d), lambda i: (i, 0)),
        out_shape=jax.ShapeDtypeStruct(xa.shape if in_place else (rows, d), F32),
        scratch_shapes=[pltpu.VMEM((TOP_K, tm) + row_tile, U32), pltpu.SemaphoreType.DMA(())],
        input_output_aliases={6: 0} if in_place else {},
        compiler_params=_cparams("arbitrary"),
        name="moe_combine",
    )(pos, wts, hp, ys, w_sgu, w_sd, xa, g, mod3)


def kernel(x, c, ctx, c_ctx, w_ada, b_ada, norm_g, w_in, diff_lambda, g_subln, sg_ln_g, sg_ln_b, sg_w, sg_b, conv_w, w_branch, w_out, w_router, router_bias, w_exp_gate, w_exp_up, w_exp_down, w_sh_gate, w_sh_up, w_sh_down):
    n_batch, seq, d = x.shape
    ctx_len = ctx.shape[1]
    depth = w_ada.shape[0]
    mix_w = w_branch.shape[2]
    n_heads = mix_w // HEAD_W
    n_exp = w_router.shape[2]
    n_lat = n_batch * seq
    n_ctx = n_batch * ctx_len
    n_all = n_lat + n_ctx
    assert n_batch + 1 <= MOD_ROWS and seq % SEQ_TILE == 0 and ctx_len % SEQ_TILE == 0
    assert n_lat % seq == 0 and n_lat % ctx_len == 0 and mix_w // SG_GROUPS == LANES
    assert d // 2 == SUBLANES * LANES

    xa = (x.reshape(n_lat, d), ctx.reshape(n_ctx, d))
    if depth == 1:
        xa = _concat_rows(*xa)
    cond = jnp.zeros((MOD_ROWS, d), F32).at[:n_batch].set(c).at[n_batch].set(c_ctx)
    mod = _ada(cond, w_ada, b_ada)
    cos, sin = _rope_tables(seq)
    q_cb, k_cb, v_cb = 0, n_heads, 2 * n_heads
    gate_cb = (8 * mix_w) // d

    w_in_b = w_in.astype(BF16)
    w_branch_b = w_branch.astype(BF16)
    w_out_b = w_out.astype(BF16)
    n_in = w_in.shape[2]

    for l in range(depth):
        last = l == depth - 1
        lam_init = 0.8 - 0.6 * math.exp(-0.3 * l)
        mod3 = mod[l].reshape(MOD_ROWS, 1, N_MOD * d)
        ng = norm_g[l]
        rows = n_lat if last else n_all

        in_proj = functools.partial(_norm_matmul, xa, ng[0:1], mod3, 1, 0, w_in_b, l,
                                    seq=seq, n_batch=n_batch)
        if last:
            y = in_proj(n_lat, 0, n_in, 0, name="in_proj")
            yc = in_proj(n_ctx, n_lat, 2 * mix_w, mix_w, name="in_proj_ctx_kv")
            ctx_kv = (yc, 0, yc, n_heads, 0, ctx_len)
        else:
            y = in_proj(n_all, 0, n_in, 0, name="in_proj")
            ctx_kv = (y, k_cb, y, v_cb, n_lat // ctx_len, ctx_len)
        attn_o = _attention(y, q_cb, 0, y, k_cb, y, v_cb, 0, seq, seq, n_batch, n_heads,
                            diff_lambda[l], g_subln[l:l + 1], lam_init, ctx=ctx_kv, rope=(cos, sin),
                            name="attn_latent")
        if not last:
            attn_c = _attention(y, q_cb, n_lat // ctx_len, y, k_cb, y, v_cb, n_lat // ctx_len,
                                ctx_len, ctx_len, n_batch, n_heads,
                                diff_lambda[l], g_subln[l:l + 1], lam_init, name="attn_ctx")
            attn_o = jnp.concatenate([attn_o, attn_c], axis=0)
        sg_bias_map = jnp.repeat(sg_b[l].T, mix_w // SG_GROUPS, axis=1)
        sg_o, cv_o = _branches(y, rows, n_lat, seq, mix_w, 3, sg_ln_g[l:l + 1], sg_ln_b[l:l + 1],
                               sg_w[l].astype(BF16), sg_bias_map, conv_w[l])
        merged = _merge(attn_o, sg_o, cv_o, y, gate_cb, w_branch_b, l, rows)
        xa = _outproj(merged, w_out_b, l, xa, ng[1:2], mod3, 2, rows, seq, n_batch)

        hp, wt_t, idx_t, rank_t, cnt = _route(xa, ng[2:3], mod3, 4, 3, w_router[l].T, router_bias[l],
                                              rows, seq, n_batch)
        pos, pad, tile_plan, max_tiles = _dispatch_plan(
            idx_t, rank_t, cnt[:, 0].astype(jnp.int32), EXPERT_TILE)
        xs = _scatter(*pad, pos, hp, max_tiles * EXPERT_TILE)
        ys = _experts(tile_plan, xs, w_exp_gate, w_exp_up, w_exp_down, l, max_tiles)
        w_sgu = jnp.concatenate([w_sh_gate[l], w_sh_up[l]], axis=-1).astype(BF16)
        xa = _combine(pos, wt_t.T, hp, ys, w_sgu, w_sh_down[l].astype(BF16), xa, ng[3:4], mod3, 5,
                      rows, seq, n_batch, in_place=not last)

    return xa.reshape(n_batch, seq, d)
```

```python
import functools
import math

import jax
import jax.numpy as jnp
from jax import lax
from jax.experimental import pallas as pl
from jax.experimental.pallas import tpu as pltpu

GRID_W = 64
QK_DIM = 64
HEAD_W = 2 * QK_DIM
ROPE_BASE = 10000.0
SG_CHUNK = 128
SG_GROUPS = 8
TOP_K = 8
N_GROUPS = 8
TOPK_GROUPS = 4
ROUTED_SCALE = 2.5
NORM_EPS = 1e-6
N_MOD = 6

LANES = 128
SUBLANES = 8
VMEM_LIMIT = 56 * 1024 * 1024
SEQ_TILE = 256
EXPERT_TILE = 512
MOD_ROWS = 16
ATTN_SUB_ROWS = 128

F32 = jnp.float32
BF16 = jnp.bfloat16
U32 = jnp.uint32
HIGHEST = lax.Precision.HIGHEST


def _cparams(*sem):
    return pltpu.CompilerParams(dimension_semantics=sem, vmem_limit_bytes=VMEM_LIMIT)


def _pick_tile(n, cands):
    for c in cands:
        if n % c == 0:
            return c
    raise ValueError(f"no tile for {n} in {cands}")


def _sigmoid(x):
    return 1.0 / (1.0 + jnp.exp(-x))


def _silu(x):
    return x * _sigmoid(x)


def _gelu_tanh(x):
    c = math.sqrt(2.0 / math.pi)
    return x * (0.5 * (1.0 + jnp.tanh(c * (x + 0.044715 * (x * x * x)))))


def _rms(x, g):
    return x * lax.rsqrt(jnp.mean(x * x, axis=-1, keepdims=True) + NORM_EPS) * g


def _pack_bf16_pairs(x):
    w = x.shape[1] // 2
    bits = pltpu.bitcast(x.astype(BF16).astype(F32), U32)
    return (bits[:, :w] >> 16) | bits[:, w:]


def _unpack_lo(w):
    return pltpu.bitcast(w << 16, F32)


def _unpack_hi(w):
    return pltpu.bitcast(w & jnp.uint32(0xFFFF0000), F32)


def _store_row_tiles(ref, packed):
    m, s, _ = ref.shape
    flat = ref.reshape(m * s, LANES)
    for j in range(s):
        flat[pl.ds(j, m, stride=s), :] = packed[:, j * LANES:(j + 1) * LANES]


def _load_row_tiles(ref):
    m, s, _ = ref.shape
    flat = ref.reshape(m * s, LANES)
    return jnp.concatenate([flat[pl.ds(j, m, stride=s), :] for j in range(s)], axis=1)


def _concat_body(a_ref, b_ref, o_ref, *, a_tiles):
    i = pl.program_id(0)

    @pl.when(i < a_tiles)
    def _():
        o_ref[...] = a_ref[...]

    @pl.when(i >= a_tiles)
    def _():
        o_ref[...] = b_ref[...]


def _concat_rows(a, b):
    d = a.shape[1]
    tm = _pick_tile(math.gcd(a.shape[0], b.shape[0]), (512, 256))
    na, nb = a.shape[0] // tm, b.shape[0] // tm
    return pl.pallas_call(
        functools.partial(_concat_body, a_tiles=na),
        grid=(na + nb,),
        in_specs=[
            pl.BlockSpec((tm, d), lambda i: (jnp.minimum(i, na - 1), 0)),
            pl.BlockSpec((tm, d), lambda i: (jnp.maximum(i - na, 0), 0)),
        ],
        out_specs=pl.BlockSpec((tm, d), lambda i: (i, 0)),
        out_shape=jax.ShapeDtypeStruct((a.shape[0] + b.shape[0], d), a.dtype),
        compiler_params=_cparams("parallel"),
        name="concat_rows",
    )(a, b)


def _ada_body(c_ref, w_ref, b_ref, o_ref):
    a = _silu(c_ref[...])
    o_ref[0] = jnp.dot(a, w_ref[0], precision=HIGHEST, preferred_element_type=F32) + b_ref[0]


def _ada(cond, w_ada, b_ada):
    depth, d, n = w_ada.shape
    tn = _pick_tile(n, (1024, 512, 256, 128))
    return pl.pallas_call(
        _ada_body,
        grid=(depth, n // tn),
        in_specs=[
            pl.BlockSpec((MOD_ROWS, d), lambda l, j: (0, 0)),
            pl.BlockSpec((1, d, tn), lambda l, j: (l, 0, j)),
            pl.BlockSpec((1, 1, tn), lambda l, j: (l, 0, j)),
        ],
        out_specs=pl.BlockSpec((1, MOD_ROWS, tn), lambda l, j: (l, 0, j)),
        out_shape=jax.ShapeDtypeStruct((depth, MOD_ROWS, n), F32),
        compiler_params=_cparams("arbitrary", "arbitrary"),
        name="ada_mod",
    )(cond, w_ada, b_ada.reshape(depth, 1, n))


def _group_of_tile(i, tiles_per_seq, n_batch):
    return jnp.minimum(i // tiles_per_seq, n_batch)


def _stream_specs(stream, tm, tile_of):
    if not isinstance(stream, tuple):
        return [pl.BlockSpec((tm, stream.shape[1]), lambda *idx: (tile_of(*idx), 0))], None
    a, b = stream
    split = a.shape[0] // tm
    assert a.shape[0] % tm == 0 and b.shape[0] % tm == 0
    return [pl.BlockSpec((tm, a.shape[1]), lambda *idx: (jnp.minimum(tile_of(*idx), split - 1), 0)),
            pl.BlockSpec((tm, b.shape[1]), lambda *idx: (jnp.maximum(tile_of(*idx) - split, 0), 0),
                         pipeline_mode=pl.Buffered(1))], split


def _stream_tile(refs, tile, split):
    if split is None:
        return refs[0][...]
    return jnp.where(tile < split, refs[0][...], refs[1][...])


def _stream_args(stream):
    return list(stream) if isinstance(stream, tuple) else [stream]


def _norm_mm_body(*refs, split, tile0):
    n_src = 1 if split is None else 2
    g_ref, sc_ref, sh_ref, w_ref, o_ref, h_ref = refs[n_src:]

    @pl.when(pl.program_id(1) == 0)
    def _():
        y = _rms(_stream_tile(refs[:n_src], pl.program_id(0) + tile0, split), g_ref[...])
        h_ref[...] = (y * (1.0 + sc_ref[0]) + sh_ref[0]).astype(h_ref.dtype)

    o_ref[...] = jnp.dot(h_ref[...], w_ref[...], preferred_element_type=F32).astype(o_ref.dtype)


def _norm_matmul(stream, g, mod3, sc_idx, sh_idx, w, layer, rows, row_off, cols, col_off, seq, n_batch, name):
    d = w.shape[1]
    tm = _pick_tile(math.gcd(math.gcd(rows, row_off) if row_off else rows, seq), (1024, 512, 256))
    tn = _pick_tile(math.gcd(cols, col_off) if col_off else cols, (1024, 512, 256))
    ro, co = row_off // tm, col_off // tn
    tps = seq // tm
    grp = lambda i: _group_of_tile(i + ro, tps, n_batch)
    x_specs, split = _stream_specs(stream, tm, lambda i, j: i + ro)
    return pl.pallas_call(
        functools.partial(_norm_mm_body, split=split, tile0=ro),
        grid=(rows // tm, cols // tn),
        in_specs=x_specs + [
            pl.BlockSpec((1, d), lambda i, j: (0, 0)),
            pl.BlockSpec((1, 1, d), lambda i, j: (grp(i), 0, sc_idx)),
            pl.BlockSpec((1, 1, d), lambda i, j: (grp(i), 0, sh_idx)),
            pl.BlockSpec((None, d, tn), lambda i, j: (layer, 0, j + co)),
        ],
        out_specs=pl.BlockSpec((tm, tn), lambda i, j: (i, j)),
        out_shape=jax.ShapeDtypeStruct((rows, cols), BF16),
        scratch_shapes=[pltpu.VMEM((tm, d), BF16)],
        compiler_params=_cparams("arbitrary", "arbitrary"),
        name=name,
    )(*_stream_args(stream), g, mod3, mod3, w)


def _rope_apply(x, cos, sin):
    lane = lax.broadcasted_iota(jnp.int32, x.shape, 1)
    first_half = (lane % (QK_DIM // 2)) < (QK_DIM // 4)
    partner = jnp.where(first_half,
                        pltpu.roll(x, LANES - QK_DIM // 4, 1),
                        pltpu.roll(x, QK_DIM // 4, 1))
    return x * cos + partner * sin


def _rope_tables(seq):
    n_freq = QK_DIM // 4
    inv = ROPE_BASE ** (-jnp.arange(n_freq, dtype=F32) / n_freq)
    t = jnp.arange(seq)
    row = (t // GRID_W).astype(F32)
    col = (t % GRID_W).astype(F32)
    ang = jnp.stack([row[:, None] * inv, col[:, None] * inv], axis=1)
    cos = jnp.cos(ang)
    sin = jnp.sin(ang)
    cos64 = jnp.stack([cos, cos], axis=2).reshape(seq, QK_DIM)
    sin64 = jnp.stack([-sin, sin], axis=2).reshape(seq, QK_DIM)
    reps = LANES // QK_DIM
    return jnp.tile(cos64, (1, reps)), jnp.tile(sin64, (1, reps))


def _attn_body(*refs, two_seg, rope, lam_init):
    refs = list(refs)
    q_ref, k_ref, v_ref = refs[:3]
    del refs[:3]
    segs = []
    if two_seg:
        segs.append((refs[0], refs[1], False))
        del refs[:2]
    segs.append((k_ref, v_ref, rope))
    if rope:
        cos_ref, sin_ref = refs[:2]
        del refs[:2]
    dl_ref, g_ref, o_ref, kt_ref, v1_ref = refs
    tq = q_ref.shape[0]

    @pl.when(pl.program_id(2) == 0)
    def _():
        row = 0
        for ks_ref, vs_ref, rotate in segs:
            n = ks_ref.shape[0]
            ks = ks_ref[...].astype(F32)
            if rotate:
                ks = _rope_apply(ks, cos_ref[...], sin_ref[...])
            kt_ref[:, row:row + n] = ks.T.astype(BF16)
            v1_ref[row:row + n, :HEAD_W] = vs_ref[...]
            v1_ref[row:row + n, HEAD_W:] = jnp.ones(vs_ref.shape, BF16)
            row += n

    dl = dl_ref[...]
    lam = (jnp.exp(jnp.sum(dl[0:1] * dl[1:2], axis=-1, keepdims=True))
           - jnp.exp(jnp.sum(dl[2:3] * dl[3:4], axis=-1, keepdims=True)) + lam_init)
    q = q_ref[...]
    if rope:
        q_rows = pl.ds(pl.multiple_of(pl.program_id(2) * tq, tq), tq)
        q = _rope_apply(q.astype(F32), cos_ref[q_rows, :], sin_ref[q_rows, :]).astype(BF16)
    q = q * jnp.asarray(QK_DIM ** -0.5, BF16)
    lane = lax.broadcasted_iota(jnp.int32, q.shape, 1)
    zero = jnp.zeros_like(q)

    def softmax_v(qm):
        s = jnp.dot(qm, kt_ref[...], preferred_element_type=F32)
        m = jnp.max(s, axis=-1, keepdims=True)
        acc = jnp.dot(jnp.exp(s - m).astype(BF16), v1_ref[...], preferred_element_type=F32)
        return acc[:, :HEAD_W] / acc[:, HEAD_W:]

    q1 = jnp.where(lane < QK_DIM, q, zero)
    q2 = jnp.where(lane >= QK_DIM, q, zero)
    sub = min(q.shape[0], ATTN_SUB_ROWS)
    for r0 in range(0, q.shape[0], sub):
        rs = slice(r0, r0 + sub)
        o = softmax_v(q1[rs]) - lam * softmax_v(q2[rs])
        o_ref[rs, :] = (_rms(o, g_ref[...]) * (1.0 - lam_init)).astype(o_ref.dtype)


def _attention(q_arr, q_cb, q_rb, k_arr, k_cb, v_arr, v_cb, kv_rb, seq_q, seq_k, n_batch, n_heads,
               dl, g_sub, lam_init, ctx=None, rope=None, name="attn"):
    tq = _pick_tile(seq_q, (1024, 512, 256, 128))
    nq = seq_q // tq
    n_keys = seq_k + (ctx[5] if ctx is not None else 0)
    scratch = [pltpu.VMEM((HEAD_W, n_keys), BF16), pltpu.VMEM((n_keys, 2 * HEAD_W), BF16)]
    in_specs = [
        pl.BlockSpec((tq, HEAD_W), lambda b, h, i: ((q_rb + b) * nq + i, q_cb + h)),
        pl.BlockSpec((seq_k, HEAD_W), lambda b, h, i: (kv_rb + b, k_cb + h)),
        pl.BlockSpec((seq_k, HEAD_W), lambda b, h, i: (kv_rb + b, v_cb + h)),
    ]
    args = [q_arr, k_arr, v_arr]
    if ctx is not None:
        kc_arr, kc_cb, vc_arr, vc_cb, c_rb, seq_c = ctx
        in_specs += [
            pl.BlockSpec((seq_c, HEAD_W), lambda b, h, i: (c_rb + b, kc_cb + h)),
            pl.BlockSpec((seq_c, HEAD_W), lambda b, h, i: (c_rb + b, vc_cb + h)),
        ]
        args += [kc_arr, vc_arr]
    if rope is not None:
        assert seq_q == seq_k
        in_specs += [pl.BlockSpec((seq_k, LANES), lambda b, h, i: (0, 0))] * 2
        args += list(rope)
    in_specs += [
        pl.BlockSpec(dl.shape, lambda b, h, i: (0, 0)),
        pl.BlockSpec((1, HEAD_W), lambda b, h, i: (0, 0)),
    ]
    args += [dl, g_sub]
    return pl.pallas_call(
        functools.partial(_attn_body, two_seg=ctx is not None, rope=rope is not None, lam_init=lam_init),
        grid=(n_batch, n_heads, nq),
        in_specs=in_specs,
        out_specs=pl.BlockSpec((tq, HEAD_W), lambda b, h, i: (b * nq + i, h)),
        out_shape=jax.ShapeDtypeStruct((n_batch * seq_q, n_heads * HEAD_W), BF16),
        scratch_shapes=scratch,
        compiler_params=_cparams("arbitrary", "arbitrary", "arbitrary"),
        name=name,
    )(*args)


def _branches_body(u_ref, v_ref, x_ref, b_ref, c_ref, xp_ref, cp_ref, xn_ref, cn_ref,
                   lng_ref, lnb_ref, sgw_ref, sgb_ref, cw_ref, sg_o_ref, cv_o_ref,
                   *, lat_tiles, tiles_per_seq):
    i = pl.program_id(0)
    tm, w = u_ref.shape
    u = _gelu_tanh(u_ref[...].astype(F32))
    v = _gelu_tanh(v_ref[...].astype(F32))
    mu = jnp.mean(v, axis=-1, keepdims=True)
    dv = v - mu
    v = dv * lax.rsqrt(jnp.mean(dv * dv, axis=-1, keepdims=True) + NORM_EPS)
    v = (v * lng_ref[...] + lnb_ref[...]).astype(BF16)
    gw = w // SG_GROUPS
    for c in range(tm // SG_CHUNK):
        rs = slice(c * SG_CHUNK, (c + 1) * SG_CHUNK)
        for g in range(SG_GROUPS):
            cs = slice(g * gw, (g + 1) * gw)
            mixed = jnp.dot(sgw_ref[g], v[rs, cs], preferred_element_type=F32) + sgb_ref[:, cs]
            sg_o_ref[rs, cs] = (u[rs, cs] * mixed).astype(sg_o_ref.dtype)
    is_lat = i < lat_tiles
    seq_first = jnp.logical_or(jnp.logical_not(is_lat), i % tiles_per_seq == 0)
    seq_last = jnp.logical_or(jnp.logical_not(is_lat), i % tiles_per_seq == tiles_per_seq - 1)
    z = c_ref[...].astype(F32) * x_ref[...].astype(F32)
    hp = xp_ref.shape[0]
    z_prev = (cp_ref[...].astype(F32) * xp_ref[...].astype(F32))[hp - 1:hp, :]
    z_next = (cn_ref[...].astype(F32) * xn_ref[...].astype(F32))[0:1, :]
    z_prev = jnp.where(seq_first, 0.0, z_prev)
    z_next = jnp.where(seq_last, 0.0, z_next)
    row = lax.broadcasted_iota(jnp.int32, z.shape, 0)
    z_dn = jnp.where(row == 0, z_prev, pltpu.roll(z, 1, 0))
    z_up = jnp.where(row == tm - 1, z_next, pltpu.roll(z, tm - 1, 0))
    y = cw_ref[0:1, :] * z_dn + cw_ref[1:2, :] * z + cw_ref[2:3, :] * z_up
    cv_o_ref[...] = (b_ref[...].astype(F32) * y).astype(cv_o_ref.dtype)


def _branches(y, rows, lat_rows, seq, mix_w, col0, sg_ln_g, sg_ln_b, sg_w, sg_bias_map, conv_w):
    tm = SEQ_TILE
    halo = 16
    hb = tm // halo
    n_tiles = rows // tm
    last_hblk = rows // halo - 1
    cur = lambda c: pl.BlockSpec((tm, mix_w), lambda i: (i, col0 + c))
    prev = lambda c: pl.BlockSpec((halo, mix_w), lambda i: (jnp.maximum(i * hb - 1, 0), col0 + c))
    nxt = lambda c: pl.BlockSpec((halo, mix_w), lambda i: (jnp.minimum((i + 1) * hb, last_hblk), col0 + c))
    full = lambda a: pl.BlockSpec(a.shape, lambda i: (0,) * a.ndim)
    consts = [sg_ln_g, sg_ln_b, sg_w, sg_bias_map, conv_w]
    out = jax.ShapeDtypeStruct((rows, mix_w), BF16)
    return pl.pallas_call(
        functools.partial(_branches_body, lat_tiles=lat_rows // tm, tiles_per_seq=seq // tm),
        grid=(n_tiles,),
        in_specs=[cur(0), cur(1), cur(2), cur(3), cur(4), prev(2), prev(4), nxt(2), nxt(4)]
                 + [full(a) for a in consts],
        out_specs=[pl.BlockSpec((tm, mix_w), lambda i: (i, 0))] * 2,
        out_shape=[out, out],
        compiler_params=_cparams("parallel"),
        name="sg_conv_branches",
    )(y, y, y, y, y, y, y, y, y, *consts)


def _merge_body(a_ref, s_ref, c_ref, g0_ref, g1_ref, g2_ref, wb_ref, o_ref):
    acc = None
    for g, (br, gl) in enumerate(((a_ref, g0_ref), (s_ref, g1_ref), (c_ref, g2_ref))):
        t = _sigmoid(gl[...].astype(F32)) * jnp.dot(br[...], wb_ref[g], preferred_element_type=F32)
        acc = t if acc is None else acc + t
    o_ref[...] = acc.astype(o_ref.dtype)


def _merge(attn_o, sg_o, cv_o, y, gate_cb, w_branch, layer, rows):
    n_br, mix_w, d = w_branch.shape[1:]
    tm = _pick_tile(rows, (512, 256))
    br = pl.BlockSpec((tm, mix_w), lambda i: (i, 0))
    gate = lambda g: pl.BlockSpec((tm, d), lambda i: (i, gate_cb + g))
    return pl.pallas_call(
        _merge_body,
        grid=(rows // tm,),
        in_specs=[br, br, br, gate(0), gate(1), gate(2),
                  pl.BlockSpec((None, n_br, mix_w, d), lambda i: (layer, 0, 0, 0),
                               pipeline_mode=pl.Buffered(1))],
        out_specs=pl.BlockSpec((tm, d), lambda i: (i, 0)),
        out_shape=jax.ShapeDtypeStruct((rows, d), BF16),
        compiler_params=_cparams("parallel"),
        name="branch_merge",
    )(attn_o, sg_o, cv_o, y, y, y, w_branch)


def _outproj_body(m_ref, w_ref, g_ref, gt_ref, *refs, split):
    x_refs, o_ref = refs[:-1], refs[-1]
    mix = jnp.dot(m_ref[...], w_ref[...], preferred_element_type=F32)
    x = _stream_tile(x_refs, pl.program_id(0), split)
    o_ref[...] = x + gt_ref[0] * _rms(mix, g_ref[...])


def _outproj(merged, w_out, layer, stream, g, mod3, gt_idx, rows, seq, n_batch):
    d = w_out.shape[1]
    tm = _pick_tile(seq, (512, 256))
    tps = seq // tm
    grp = lambda i: _group_of_tile(i, tps, n_batch)
    x_specs, split = _stream_specs(stream, tm, lambda i: i)
    if split is None:
        out_rows, alias = stream.shape[0], {4: 0}
    else:
        out_rows, alias = rows, {}
        assert rows == stream[0].shape[0] + stream[1].shape[0]
    return pl.pallas_call(
        functools.partial(_outproj_body, split=split),
        grid=(rows // tm,),
        in_specs=[
            pl.BlockSpec((tm, d), lambda i: (i, 0)),
            pl.BlockSpec((None, d, d), lambda i: (layer, 0, 0), pipeline_mode=pl.Buffered(1)),
            pl.BlockSpec((1, d), lambda i: (0, 0)),
            pl.BlockSpec((1, 1, d), lambda i: (grp(i), 0, gt_idx)),
        ] + x_specs,
        out_specs=pl.BlockSpec((tm, d), lambda i: (i, 0)),
        out_shape=jax.ShapeDtypeStruct((out_rows, d), F32),
        input_output_aliases=alias,
        compiler_params=_cparams("parallel"),
        name="out_proj_residual",
    )(merged, w_out, g, mod3, *_stream_args(stream))


def _top1_mask(vals, idx, n, axis):
    m = jnp.max(vals, axis=axis, keepdims=True)
    first = jnp.min(jnp.where(vals == m, idx, n), axis=axis, keepdims=True)
    return idx == first, m, first


def _route_body(x_ref, g_ref, sc_ref, sh_ref, wr_ref, rb_ref, hp_ref, wt_ref, ix_ref, rk_ref, cnt_ref):
    h2 = _rms(x_ref[...], g_ref[...]) * (1.0 + sc_ref[0]) + sh_ref[0]
    _store_row_tiles(hp_ref, _pack_bf16_pairs(h2))
    n_exp = wr_ref.shape[0]
    tm = x_ref.shape[0]
    per_group = n_exp // N_GROUPS
    logits = lax.dot_general(wr_ref[...], h2, (((1,), (1,)), ((), ())),
                             precision=HIGHEST, preferred_element_type=F32)
    scores = _sigmoid(logits)
    choice = scores + rb_ref[...]
    neg = -jnp.inf
    ji = lax.broadcasted_iota(jnp.int32, (per_group, tm), 0)
    groups, gs = [], []
    for g in range(N_GROUPS):
        cg = choice[g * per_group:(g + 1) * per_group, :]
        hit, m1, _ = _top1_mask(cg, ji, per_group, 0)
        m2 = jnp.max(jnp.where(hit, neg, cg), axis=0, keepdims=True)
        groups.append(cg)
        gs.append(m1 + m2)
    kept = []
    for g in range(N_GROUPS):
        beaten_by = jnp.zeros((1, tm), jnp.int32)
        for o in range(N_GROUPS):
            if o != g:
                beats = (gs[o] >= gs[g]) if o < g else (gs[o] > gs[g])
                beaten_by = beaten_by + beats.astype(jnp.int32)
        kept.append(jnp.where(beaten_by < TOPK_GROUPS, groups[g], neg))
    cm = jnp.concatenate(kept, axis=0)
    ei = lax.broadcasted_iota(jnp.int32, cm.shape, 0)
    ws, ids, hits = [], [], []
    sel = jnp.zeros(cm.shape, F32)
    for _ in range(TOP_K):
        hit, _, first = _top1_mask(cm, ei, n_exp, 0)
        ws.append(jnp.sum(jnp.where(hit, scores, 0.0), axis=0, keepdims=True))
        ids.append(first)
        hits.append(hit)
        sel = sel + jnp.where(hit, 1.0, 0.0)
        cm = jnp.where(hit, neg, cm)
    w = jnp.concatenate(ws, axis=0)
    wt_ref[...] = w / jnp.sum(w, axis=0, keepdims=True) * ROUTED_SCALE
    ix_ref[...] = jnp.concatenate(ids, axis=0)

    @pl.when(pl.program_id(0) == 0)
    def _():
        cnt_ref[...] = jnp.zeros_like(cnt_ref)

    before = (lax.broadcasted_iota(jnp.int32, (tm, tm), 0)
              < lax.broadcasted_iota(jnp.int32, (tm, tm), 1))
    within = jnp.dot(sel.astype(BF16), jnp.where(before, 1.0, 0.0).astype(BF16),
                     preferred_element_type=F32)
    rank = within + cnt_ref[:, 0:1]
    rk_ref[...] = jnp.concatenate(
        [jnp.sum(jnp.where(h, rank, 0.0), axis=0, keepdims=True) for h in hits], axis=0).astype(jnp.int32)
    cnt_ref[...] = cnt_ref[...] + jnp.sum(sel, axis=1, keepdims=True)


def _route(xa, g, mod3, sc_idx, sh_idx, w_router_t, router_bias, rows, seq, n_batch):
    d = xa.shape[1]
    n_exp = w_router_t.shape[0]
    tm = SEQ_TILE
    tps = seq // tm
    grp = lambda i: _group_of_tile(i, tps, n_batch)
    return pl.pallas_call(
        _route_body,
        grid=(rows // tm,),
        in_specs=[
            pl.BlockSpec((tm, d), lambda i: (i, 0)),
            pl.BlockSpec((1, d), lambda i: (0, 0)),
            pl.BlockSpec((1, 1, d), lambda i: (grp(i), 0, sc_idx)),
            pl.BlockSpec((1, 1, d), lambda i: (grp(i), 0, sh_idx)),
            pl.BlockSpec((n_exp, d), lambda i: (0, 0)),
            pl.BlockSpec((n_exp, 1), lambda i: (0, 0)),
        ],
        out_specs=[
            pl.BlockSpec((tm, SUBLANES, LANES), lambda i: (i, 0, 0)),
            pl.BlockSpec((TOP_K, tm), lambda i: (0, i)),
            pl.BlockSpec((TOP_K, tm), lambda i: (0, i)),
            pl.BlockSpec((TOP_K, tm), lambda i: (0, i)),
            pl.BlockSpec((n_exp, LANES), lambda i: (0, 0)),
        ],
        out_shape=[
            jax.ShapeDtypeStruct((rows, SUBLANES, LANES), U32),
            jax.ShapeDtypeStruct((TOP_K, rows), F32),
            jax.ShapeDtypeStruct((TOP_K, rows), jnp.int32),
            jax.ShapeDtypeStruct((TOP_K, rows), jnp.int32),
            jax.ShapeDtypeStruct((n_exp, LANES), F32),
        ],
        compiler_params=_cparams("arbitrary"),
        name="moe_norm_route",
    )(xa, g, mod3, mod3, w_router_t, router_bias.reshape(n_exp, 1))


def _dispatch_plan(idx_t, rank_t, counts, tile):
    n_exp = counts.shape[0]
    rows = idx_t.shape[1]
    tiles_e = (counts + tile - 1) // tile
    tile_end = jnp.cumsum(tiles_e)
    offs = (tile_end - tiles_e) * tile
    onehot = idx_t[None, :, :] == jnp.arange(n_exp, dtype=jnp.int32)[:, None, None]
    pos = rank_t + jnp.sum(jnp.where(onehot, offs[:, None, None], 0), axis=0)
    max_tiles = (TOP_K * rows) // tile + n_exp
    t = jnp.arange(max_tiles, dtype=jnp.int32)
    used = tile_end[-1]
    t_eff = jnp.minimum(t, used - 1)
    tile_expert = jnp.minimum(jnp.sum(t_eff[:, None] >= tile_end[None, :], axis=1), n_exp - 1)
    used_rows = used * tile
    pad = (jnp.append(offs + counts, used_rows).astype(jnp.int32),
           jnp.append(tiles_e * tile - counts, max_tiles * tile - used_rows).astype(jnp.int32))
    own = tile_expert[:, None] == jnp.arange(n_exp, dtype=jnp.int32)[None, :]
    last_row = jnp.sum(jnp.where(own, (offs + counts)[None, :], 0), axis=1)
    tile_rows = jnp.where(t < used, jnp.clip(last_row - t * tile, 0, tile), 0)
    first = jnp.logical_or(t == 0, tile_expert != jnp.roll(tile_expert, 1))
    slot = (jnp.cumsum(first.astype(jnp.int32)) - 1) % 2
    after = jnp.sum(jnp.where(own, tile_end[None, :], 0), axis=1)
    nxt = jnp.where(after < used, jnp.take(tile_expert, jnp.minimum(after, max_tiles - 1)), -1)
    plan = tuple(a.astype(jnp.int32) for a in (tile_expert, t_eff, tile_rows, first, nxt, slot))
    return pos.astype(jnp.int32), pad, plan, max_tiles


def _scatter_body(pad_start_ref, pad_len_ref, pos_ref, h_ref, xs_ref, zeros, sem, zsem):
    tm = h_ref.shape[0]
    n_exp = pad_start_ref.shape[0] - 1

    @pl.when(pl.program_id(0) == 0)
    def _():
        zeros[...] = jnp.zeros_like(zeros)
        half = zeros.shape[0]
        sizes = [half >> i for i in range(half.bit_length())]
        tail_start = pad_start_ref[n_exp]
        tail_halves = pad_len_ref[n_exp] // half

        def tail_copy(i):
            return pltpu.make_async_copy(zeros, xs_ref.at[pl.ds(tail_start + i * half, half)], zsem)

        lax.fori_loop(0, tail_halves, lambda i, c: (tail_copy(i).start(), c)[1], 0)
        lax.fori_loop(0, tail_halves, lambda i, c: (tail_copy(i).wait(), c)[1], 0)

        def pieces(e, wait):
            start = pad_start_ref[e]
            n = pad_len_ref[e]
            for size in sizes:
                has = (n & size) != 0
                copy = pltpu.make_async_copy(zeros.at[pl.ds(0, size)], xs_ref.at[pl.ds(start, size)], zsem)

                @pl.when(has)
                def _():
                    copy.wait() if wait else copy.start()

                start = start + jnp.where(has, size, 0)

        lax.fori_loop(0, n_exp, lambda e, c: (pieces(e, False), c)[1], 0)
        lax.fori_loop(0, n_exp, lambda e, c: (pieces(e, True), c)[1], 0)

    def issue(r, carry):
        for k in range(TOP_K):
            pltpu.make_async_copy(h_ref.at[r], xs_ref.at[pos_ref[k, r]], sem).start(
                priority=k % 2)
        return carry

    lax.fori_loop(0, tm, issue, 0)
    all_rows = xs_ref.at[pl.ds(0, TOP_K * tm)]
    pltpu.make_async_copy(all_rows, all_rows, sem).wait()


def _scatter(pad_start, pad_len, pos, hp, slots):
    rows = hp.shape[0]
    row_tile = hp.shape[1:]
    tm = SEQ_TILE
    assert EXPERT_TILE & (EXPERT_TILE - 1) == 0
    grid_spec = pltpu.PrefetchScalarGridSpec(
        num_scalar_prefetch=2,
        grid=(rows // tm,),
        in_specs=[
            pl.BlockSpec((TOP_K, tm), lambda i, ps, pn: (0, i), memory_space=pltpu.SMEM),
            pl.BlockSpec((tm,) + row_tile, lambda i, ps, pn: (i, 0, 0)),
        ],
        out_specs=pl.BlockSpec(memory_space=pl.ANY),
        scratch_shapes=[pltpu.VMEM((EXPERT_TILE // 2,) + row_tile, U32),
                        pltpu.SemaphoreType.DMA(()), pltpu.SemaphoreType.DMA(())],
    )
    return pl.pallas_call(
        _scatter_body,
        grid_spec=grid_spec,
        out_shape=jax.ShapeDtypeStruct((slots,) + row_tile, U32),
        compiler_params=_cparams("arbitrary"),
        name="moe_dispatch",
    )(pad_start, pad_len, pos, hp)


def _ffn_rows(xw, wgu_ref, wd_ref):
    x = jnp.concatenate([_unpack_lo(xw).astype(BF16), _unpack_hi(xw).astype(BF16)], axis=1)
    gu = jnp.dot(x, wgu_ref, preferred_element_type=F32)
    f = gu.shape[1] // 2
    hdn = (_silu(gu[:, :f]) * gu[:, f:]).astype(BF16)
    return jnp.dot(hdn, wd_ref, preferred_element_type=F32)


def _experts_body(te_ref, ts_ref, tr_ref, first_ref, next_ref, slot_ref,
                  xs_ref, wg_hbm, wu_hbm, wd_hbm, ys_ref,
                  wg_buf, wu_buf, wd_buf, wgu_bf, wd_bf, sems, *, layer):
    del ts_ref
    t = pl.program_id(0)
    n_real = tr_ref[t]
    tm = xs_ref.shape[0]
    half = tm // 2
    f = wg_buf.shape[-1]

    def weight_copies(e, s):
        return [pltpu.make_async_copy(src.at[layer, e], dst.at[s], sems.at[s])
                for src, dst in ((wg_hbm, wg_buf), (wu_hbm, wu_buf), (wd_hbm, wd_buf))]

    @pl.when(t == 0)
    def _():
        for c in weight_copies(te_ref[0], slot_ref[0]):
            c.start()

    @pl.when(first_ref[t] == 1)
    def _():
        s = slot_ref[t]
        for c in weight_copies(te_ref[t], s):
            c.wait()
        wgu_bf[:, :f] = wg_buf[s].astype(BF16)
        wgu_bf[:, f:] = wu_buf[s].astype(BF16)
        wd_bf[...] = wd_buf[s].astype(BF16)

        @pl.when(next_ref[t] >= 0)
        def _():
            for c in weight_copies(next_ref[t], 1 - s):
                c.start()

    def ffn(rows):
        part = pl.ds(0, rows)
        y = _ffn_rows(_load_row_tiles(xs_ref.at[part]), wgu_bf[...], wd_bf[...])
        _store_row_tiles(ys_ref.at[part], _pack_bf16_pairs(y))

    @pl.when(n_real > half)
    def _():
        ffn(tm)

    @pl.when(jnp.logical_and(n_real > 0, n_real <= half))
    def _():
        ffn(half)
        ys_ref[pl.ds(half, half)] = jnp.zeros((half,) + ys_ref.shape[1:], ys_ref.dtype)

    @pl.when(n_real == 0)
    def _():
        ys_ref[...] = jnp.zeros_like(ys_ref)


def _experts(tile_plan, xs, w_gate, w_up, w_down, layer, max_tiles):
    slots = xs.shape[0]
    row_tile = xs.shape[1:]
    tm = EXPERT_TILE
    d, f = w_gate.shape[2], w_gate.shape[3]
    grid_spec = pltpu.PrefetchScalarGridSpec(
        num_scalar_prefetch=len(tile_plan),
        grid=(max_tiles,),
        in_specs=[
            pl.BlockSpec((tm,) + row_tile, lambda t, te, ts, *_: (ts[t], 0, 0)),
            pl.BlockSpec(memory_space=pl.ANY),
            pl.BlockSpec(memory_space=pl.ANY),
            pl.BlockSpec(memory_space=pl.ANY),
        ],
        out_specs=pl.BlockSpec((tm,) + row_tile, lambda t, *_: (t, 0, 0)),
        scratch_shapes=[pltpu.VMEM((2, d, f), F32), pltpu.VMEM((2, d, f), F32), pltpu.VMEM((2, f, d), F32),
                        pltpu.VMEM((d, 2 * f), BF16), pltpu.VMEM((f, d), BF16),
                        pltpu.SemaphoreType.DMA((2,))],
    )
    return pl.pallas_call(
        functools.partial(_experts_body, layer=layer),
        grid_spec=grid_spec,
        out_shape=jax.ShapeDtypeStruct((slots,) + row_tile, U32),
        compiler_params=_cparams("arbitrary"),
        name="moe_experts",
    )(*tile_plan, xs, w_gate, w_up, w_down)


def _combine_body(pos_ref, pos_next_ref, wt_ref, hp_ref, ys_ref, wgu_ref, wd_ref, x_ref, g_ref, gt_ref,
                  o_ref, gbuf_even, gbuf_odd, sems):
    tm = hp_ref.shape[0]
    i = pl.program_id(0)
    bufs = (gbuf_even, gbuf_odd)

    def row_copy(p_ref, slot, k, r):
        return pltpu.make_async_copy(ys_ref.at[p_ref[k, r]], bufs[slot].at[k, r], sems.at[slot])

    def wait_rows(slot):
        all_rows = ys_ref.at[pl.ds(0, TOP_K * tm)]
        pltpu.make_async_copy(all_rows, all_rows, sems.at[slot]).wait()

    @pl.when(i == 0)
    def _():
        def issue(r, carry):
            for k in range(TOP_K):
                row_copy(pos_ref, 0, k, r).start(priority=k % 2)
            return carry
        lax.fori_loop(0, tm, issue, 0)

    def step(cur):
        wait_rows(cur)
        for r in range(tm):
            for k in range(TOP_K):
                row_copy(pos_next_ref, 1 - cur, k, r).start(priority=k % 2)
        y = _ffn_rows(_load_row_tiles(hp_ref), wgu_ref[...], wd_ref[...])
        half = y.shape[1] // 2
        lo, hi = y[:, :half], y[:, half:]
        wt = wt_ref[...]
        for k in range(TOP_K):
            wk = wt[:, k:k + 1]
            gk = _load_row_tiles(bufs[cur].at[k])
            lo = lo + wk * _unpack_lo(gk)
            hi = hi + wk * _unpack_hi(gk)
        y = jnp.concatenate([lo, hi], axis=1)
        o_ref[...] = x_ref[...] + gt_ref[0] * _rms(y, g_ref[...])

        @pl.when(i == pl.num_programs(0) - 1)
        def _():
            wait_rows(1 - cur)

    for parity in range(2):
        pl.when(i % 2 == parity)(functools.partial(step, parity))


def _combine(pos, wts, hp, ys, w_sgu, w_sd, xa, g, mod3, gt_idx, rows, seq, n_batch, in_place):
    d = xa.shape[1]
    row_tile = hp.shape[1:]
    tm = SEQ_TILE
    tps = seq // tm
    grp = lambda i: _group_of_tile(i, tps, n_batch)
    last = rows // tm - 1
    return pl.pallas_call(
        _combine_body,
        grid=(rows // tm,),
        in_specs=[
            pl.BlockSpec((TOP_K, tm), lambda i: (0, i), memory_space=pltpu.SMEM),
            pl.BlockSpec((TOP_K, tm), lambda i: (0, jnp.minimum(i + 1, last)), memory_space=pltpu.SMEM),
            pl.BlockSpec((tm, TOP_K), lambda i: (i, 0)),
            pl.BlockSpec((tm,) + row_tile, lambda i: (i, 0, 0)),
            pl.BlockSpec(memory_space=pl.ANY),
            pl.BlockSpec(w_sgu.shape, lambda i: (0, 0), pipeline_mode=pl.Buffered(1)),
            pl.BlockSpec(w_sd.shape, lambda i: (0, 0), pipeline_mode=pl.Buffered(1)),
            pl.BlockSpec((tm, d), lambda i: (i, 0)),
            pl.BlockSpec((1, d), lambda i: (0, 0)),
            pl.BlockSpec((1, 1, d), lambda i: (grp(i), 0, gt_idx)),
        ],
        out_specs=pl.BlockSpec((tm, d), lambda i: (i, 0)),
        out_shape=jax.ShapeDtypeStruct(xa.shape if in_place else (rows, d), F32),
        scratch_shapes=[pltpu.VMEM((TOP_K, tm) + row_tile, U32), pltpu.VMEM((TOP_K, tm) + row_tile, U32),
                        pltpu.SemaphoreType.DMA((2,))],
        input_output_aliases={7: 0} if in_place else {},
        compiler_params=_cparams("arbitrary"),
        name="moe_combine",
    )(pos, pos, wts, hp, ys, w_sgu, w_sd, xa, g, mod3)


def kernel(x, c, ctx, c_ctx, w_ada, b_ada, norm_g, w_in, diff_lambda, g_subln, sg_ln_g, sg_ln_b, sg_w, sg_b, conv_w, w_branch, w_out, w_router, router_bias, w_exp_gate, w_exp_up, w_exp_down, w_sh_gate, w_sh_up, w_sh_down):
    n_batch, seq, d = x.shape
    ctx_len = ctx.shape[1]
    depth = w_ada.shape[0]
    mix_w = w_branch.shape[2]
    n_heads = mix_w // HEAD_W
    n_exp = w_router.shape[2]
    n_lat = n_batch * seq
    n_ctx = n_batch * ctx_len
    n_all = n_lat + n_ctx
    assert n_batch + 1 <= MOD_ROWS and seq % SEQ_TILE == 0 and ctx_len % SEQ_TILE == 0
    assert n_lat % seq == 0 and n_lat % ctx_len == 0 and mix_w // SG_GROUPS == LANES
    assert d // 2 == SUBLANES * LANES

    xa = (x.reshape(n_lat, d), ctx.reshape(n_ctx, d))
    if depth == 1:
        xa = _concat_rows(*xa)
    cond = jnp.zeros((MOD_ROWS, d), F32).at[:n_batch].set(c).at[n_batch].set(c_ctx)
    mod = _ada(cond, w_ada, b_ada)
    cos, sin = _rope_tables(seq)
    q_cb, k_cb, v_cb = 0, n_heads, 2 * n_heads
    gate_cb = (8 * mix_w) // d

    w_in_b = w_in.astype(BF16)
    w_branch_b = w_branch.astype(BF16)
    w_out_b = w_out.astype(BF16)
    n_in = w_in.shape[2]

    for l in range(depth):
        last = l == depth - 1
        lam_init = 0.8 - 0.6 * math.exp(-0.3 * l)
        mod3 = mod[l].reshape(MOD_ROWS, 1, N_MOD * d)
        ng = norm_g[l]
        rows = n_lat if last else n_all

        in_proj = functools.partial(_norm_matmul, xa, ng[0:1], mod3, 1, 0, w_in_b, l,
                                    seq=seq, n_batch=n_batch)
        if last:
            y = in_proj(n_lat, 0, n_in, 0, name="in_proj")
            yc = in_proj(n_ctx, n_lat, 2 * mix_w, mix_w, name="in_proj_ctx_kv")
            ctx_kv = (yc, 0, yc, n_heads, 0, ctx_len)
        else:
            y = in_proj(n_all, 0, n_in, 0, name="in_proj")
            ctx_kv = (y, k_cb, y, v_cb, n_lat // ctx_len, ctx_len)
        attn_o = _attention(y, q_cb, 0, y, k_cb, y, v_cb, 0, seq, seq, n_batch, n_heads,
                            diff_lambda[l], g_subln[l:l + 1], lam_init, ctx=ctx_kv, rope=(cos, sin),
                            name="attn_latent")
        if not last:
            attn_c = _attention(y, q_cb, n_lat // ctx_len, y, k_cb, y, v_cb, n_lat // ctx_len,
                                ctx_len, ctx_len, n_batch, n_heads,
                                diff_lambda[l], g_subln[l:l + 1], lam_init, name="attn_ctx")
            attn_o = jnp.concatenate([attn_o, attn_c], axis=0)
        sg_bias_map = jnp.repeat(sg_b[l].T, mix_w // SG_GROUPS, axis=1)
        sg_o, cv_o = _branches(y, rows, n_lat, seq, mix_w, 3, sg_ln_g[l:l + 1], sg_ln_b[l:l + 1],
                               sg_w[l].astype(BF16), sg_bias_map, conv_w[l])
        merged = _merge(attn_o, sg_o, cv_o, y, gate_cb, w_branch_b, l, rows)
        xa = _outproj(merged, w_out_b, l, xa, ng[1:2], mod3, 2, rows, seq, n_batch)

        hp, wt_t, idx_t, rank_t, cnt = _route(xa, ng[2:3], mod3, 4, 3, w_router[l].T, router_bias[l],
                                              rows, seq, n_batch)
        pos, pad, tile_plan, max_tiles = _dispatch_plan(
            idx_t, rank_t, cnt[:, 0].astype(jnp.int32), EXPERT_TILE)
        xs = _scatter(*pad, pos, hp, max_tiles * EXPERT_TILE)
        ys = _experts(tile_plan, xs, w_exp_gate, w_exp_up, w_exp_down, l, max_tiles)
        w_sgu = jnp.concatenate([w_sh_gate[l], w_sh_up[l]], axis=-1).astype(BF16)
        xa = _combine(pos, wt_t.T, hp, ys, w_sgu, w_sh_down[l].astype(BF16), xa, ng[3:4], mod3, 5,
                      rows, seq, n_batch, in_place=not last)

    return xa.reshape(n_batch, seq, d)
```

```python
import functools
import math

import jax
import jax.numpy as jnp
from jax import lax
from jax.experimental import pallas as pl
from jax.experimental.pallas import tpu as pltpu

GRID_W = 64
QK_DIM = 64
HEAD_W = 2 * QK_DIM
ROPE_BASE = 10000.0
SG_CHUNK = 128
SG_GROUPS = 8
TOP_K = 8
N_GROUPS = 8
TOPK_GROUPS = 4
ROUTED_SCALE = 2.5
NORM_EPS = 1e-6
N_MOD = 6

LANES = 128
SUBLANES = 8
VMEM_LIMIT = 56 * 1024 * 1024
SEQ_TILE = 256
EXPERT_TILE = 512
MOD_ROWS = 16
ATTN_SUB_ROWS = 128

F32 = jnp.float32
BF16 = jnp.bfloat16
U32 = jnp.uint32
HIGHEST = lax.Precision.HIGHEST


def _cparams(*sem):
    return pltpu.CompilerParams(dimension_semantics=sem, vmem_limit_bytes=VMEM_LIMIT)


def _pick_tile(n, cands):
    for c in cands:
        if n % c == 0:
            return c
    raise ValueError(f"no tile for {n} in {cands}")


def _sigmoid(x):
    return 1.0 / (1.0 + jnp.exp(-x))


def _silu(x):
    return x * _sigmoid(x)


def _gelu_tanh(x):
    c = math.sqrt(2.0 / math.pi)
    return x * (0.5 * (1.0 + jnp.tanh(c * (x + 0.044715 * (x * x * x)))))


def _rms(x, g):
    return x * lax.rsqrt(jnp.mean(x * x, axis=-1, keepdims=True) + NORM_EPS) * g


def _pack_bf16_pairs(x):
    w = x.shape[1] // 2
    bits = pltpu.bitcast(x.astype(BF16).astype(F32), U32)
    return (bits[:, :w] >> 16) | bits[:, w:]


def _unpack_lo(w):
    return pltpu.bitcast(w << 16, F32)


def _unpack_hi(w):
    return pltpu.bitcast(w & jnp.uint32(0xFFFF0000), F32)


def _store_row_tiles(ref, packed):
    m, s, _ = ref.shape
    flat = ref.reshape(m * s, LANES)
    for j in range(s):
        flat[pl.ds(j, m, stride=s), :] = packed[:, j * LANES:(j + 1) * LANES]


def _load_row_tiles(ref):
    m, s, _ = ref.shape
    flat = ref.reshape(m * s, LANES)
    return jnp.concatenate([flat[pl.ds(j, m, stride=s), :] for j in range(s)], axis=1)


def _concat_body(a_ref, b_ref, o_ref, *, a_tiles):
    i = pl.program_id(0)

    @pl.when(i < a_tiles)
    def _():
        o_ref[...] = a_ref[...]

    @pl.when(i >= a_tiles)
    def _():
        o_ref[...] = b_ref[...]


def _concat_rows(a, b):
    d = a.shape[1]
    tm = _pick_tile(math.gcd(a.shape[0], b.shape[0]), (512, 256))
    na, nb = a.shape[0] // tm, b.shape[0] // tm
    return pl.pallas_call(
        functools.partial(_concat_body, a_tiles=na),
        grid=(na + nb,),
        in_specs=[
            pl.BlockSpec((tm, d), lambda i: (jnp.minimum(i, na - 1), 0)),
            pl.BlockSpec((tm, d), lambda i: (jnp.maximum(i - na, 0), 0)),
        ],
        out_specs=pl.BlockSpec((tm, d), lambda i: (i, 0)),
        out_shape=jax.ShapeDtypeStruct((a.shape[0] + b.shape[0], d), a.dtype),
        compiler_params=_cparams("parallel"),
        name="concat_rows",
    )(a, b)


def _ada_body(c_ref, w_ref, b_ref, o_ref):
    a = _silu(c_ref[...])
    o_ref[0] = jnp.dot(a, w_ref[0], precision=HIGHEST, preferred_element_type=F32) + b_ref[0]


def _ada(cond, w_ada, b_ada):
    depth, d, n = w_ada.shape
    tn = _pick_tile(n, (1024, 512, 256, 128))
    return pl.pallas_call(
        _ada_body,
        grid=(depth, n // tn),
        in_specs=[
            pl.BlockSpec((MOD_ROWS, d), lambda l, j: (0, 0)),
            pl.BlockSpec((1, d, tn), lambda l, j: (l, 0, j)),
            pl.BlockSpec((1, 1, tn), lambda l, j: (l, 0, j)),
        ],
        out_specs=pl.BlockSpec((1, MOD_ROWS, tn), lambda l, j: (l, 0, j)),
        out_shape=jax.ShapeDtypeStruct((depth, MOD_ROWS, n), F32),
        compiler_params=_cparams("arbitrary", "arbitrary"),
        name="ada_mod",
    )(cond, w_ada, b_ada.reshape(depth, 1, n))


def _group_of_tile(i, tiles_per_seq, n_batch):
    return jnp.minimum(i // tiles_per_seq, n_batch)


def _stream_specs(stream, tm, tile_of):
    if not isinstance(stream, tuple):
        return [pl.BlockSpec((tm, stream.shape[1]), lambda *idx: (tile_of(*idx), 0))], None
    a, b = stream
    split = a.shape[0] // tm
    assert a.shape[0] % tm == 0 and b.shape[0] % tm == 0
    return [pl.BlockSpec((tm, a.shape[1]), lambda *idx: (jnp.minimum(tile_of(*idx), split - 1), 0)),
            pl.BlockSpec((tm, b.shape[1]), lambda *idx: (jnp.maximum(tile_of(*idx) - split, 0), 0),
                         pipeline_mode=pl.Buffered(1))], split


def _stream_tile(refs, tile, split):
    if split is None:
        return refs[0][...]
    return jnp.where(tile < split, refs[0][...], refs[1][...])


def _stream_args(stream):
    return list(stream) if isinstance(stream, tuple) else [stream]


def _norm_mm_body(*refs, split, tile0):
    n_src = 1 if split is None else 2
    g_ref, sc_ref, sh_ref, w_ref, o_ref, h_ref = refs[n_src:]

    @pl.when(pl.program_id(1) == 0)
    def _():
        y = _rms(_stream_tile(refs[:n_src], pl.program_id(0) + tile0, split), g_ref[...])
        h_ref[...] = (y * (1.0 + sc_ref[0]) + sh_ref[0]).astype(h_ref.dtype)

    o_ref[...] = jnp.dot(h_ref[...], w_ref[...], preferred_element_type=F32).astype(o_ref.dtype)


def _norm_matmul(stream, g, mod3, sc_idx, sh_idx, w, layer, rows, row_off, cols, col_off, seq, n_batch, name):
    d = w.shape[1]
    tm = _pick_tile(math.gcd(math.gcd(rows, row_off) if row_off else rows, seq), (1024, 512, 256))
    tn = _pick_tile(math.gcd(cols, col_off) if col_off else cols, (1024, 512, 256))
    ro, co = row_off // tm, col_off // tn
    tps = seq // tm
    grp = lambda i: _group_of_tile(i + ro, tps, n_batch)
    x_specs, split = _stream_specs(stream, tm, lambda i, j: i + ro)
    return pl.pallas_call(
        functools.partial(_norm_mm_body, split=split, tile0=ro),
        grid=(rows // tm, cols // tn),
        in_specs=x_specs + [
            pl.BlockSpec((1, d), lambda i, j: (0, 0)),
            pl.BlockSpec((1, 1, d), lambda i, j: (grp(i), 0, sc_idx)),
            pl.BlockSpec((1, 1, d), lambda i, j: (grp(i), 0, sh_idx)),
            pl.BlockSpec((None, d, tn), lambda i, j: (layer, 0, j + co)),
        ],
        out_specs=pl.BlockSpec((tm, tn), lambda i, j: (i, j)),
        out_shape=jax.ShapeDtypeStruct((rows, cols), BF16),
        scratch_shapes=[pltpu.VMEM((tm, d), BF16)],
        compiler_params=_cparams("arbitrary", "arbitrary"),
        name=name,
    )(*_stream_args(stream), g, mod3, mod3, w)


def _rope_apply(x, cos, sin):
    lane = lax.broadcasted_iota(jnp.int32, x.shape, 1)
    first_half = (lane % (QK_DIM // 2)) < (QK_DIM // 4)
    partner = jnp.where(first_half,
                        pltpu.roll(x, LANES - QK_DIM // 4, 1),
                        pltpu.roll(x, QK_DIM // 4, 1))
    return x * cos + partner * sin


def _rope_tables(seq):
    n_freq = QK_DIM // 4
    inv = ROPE_BASE ** (-jnp.arange(n_freq, dtype=F32) / n_freq)
    t = jnp.arange(seq)
    row = (t // GRID_W).astype(F32)
    col = (t % GRID_W).astype(F32)
    ang = jnp.stack([row[:, None] * inv, col[:, None] * inv], axis=1)
    cos = jnp.cos(ang)
    sin = jnp.sin(ang)
    cos64 = jnp.stack([cos, cos], axis=2).reshape(seq, QK_DIM)
    sin64 = jnp.stack([-sin, sin], axis=2).reshape(seq, QK_DIM)
    reps = LANES // QK_DIM
    return jnp.tile(cos64, (1, reps)), jnp.tile(sin64, (1, reps))


def _attn_body(*refs, two_seg, rope, lam_init):
    refs = list(refs)
    q_ref, k_ref, v_ref = refs[:3]
    del refs[:3]
    segs = []
    if two_seg:
        segs.append((refs[0], refs[1], False))
        del refs[:2]
    segs.append((k_ref, v_ref, rope))
    if rope:
        cos_ref, sin_ref = refs[:2]
        del refs[:2]
    dl_ref, g_ref, o_ref, kt_ref, v1_ref = refs
    tq = q_ref.shape[0]

    @pl.when(pl.program_id(2) == 0)
    def _():
        row = 0
        for ks_ref, vs_ref, rotate in segs:
            n = ks_ref.shape[0]
            ks = ks_ref[...].astype(F32)
            if rotate:
                ks = _rope_apply(ks, cos_ref[...], sin_ref[...])
            kt_ref[:, row:row + n] = ks.T.astype(BF16)
            v1_ref[row:row + n, :HEAD_W] = vs_ref[...]
            v1_ref[row:row + n, HEAD_W:] = jnp.ones(vs_ref.shape, BF16)
            row += n

    dl = dl_ref[...]
    lam = (jnp.exp(jnp.sum(dl[0:1] * dl[1:2], axis=-1, keepdims=True))
           - jnp.exp(jnp.sum(dl[2:3] * dl[3:4], axis=-1, keepdims=True)) + lam_init)
    q = q_ref[...]
    if rope:
        q_rows = pl.ds(pl.multiple_of(pl.program_id(2) * tq, tq), tq)
        q = _rope_apply(q.astype(F32), cos_ref[q_rows, :], sin_ref[q_rows, :]).astype(BF16)
    q = q * jnp.asarray(QK_DIM ** -0.5, BF16)
    lane = lax.broadcasted_iota(jnp.int32, q.shape, 1)
    zero = jnp.zeros_like(q)

    def softmax_v(qm):
        s = jnp.dot(qm, kt_ref[...], preferred_element_type=F32)
        m = jnp.max(s, axis=-1, keepdims=True)
        acc = jnp.dot(jnp.exp(s - m).astype(BF16), v1_ref[...], preferred_element_type=F32)
        return acc[:, :HEAD_W] / acc[:, HEAD_W:]

    q1 = jnp.where(lane < QK_DIM, q, zero)
    q2 = jnp.where(lane >= QK_DIM, q, zero)
    sub = min(q.shape[0], ATTN_SUB_ROWS)
    for r0 in range(0, q.shape[0], sub):
        rs = slice(r0, r0 + sub)
        o = softmax_v(q1[rs]) - lam * softmax_v(q2[rs])
        o_ref[rs, :] = (_rms(o, g_ref[...]) * (1.0 - lam_init)).astype(o_ref.dtype)


def _attention(q_arr, q_cb, q_rb, k_arr, k_cb, v_arr, v_cb, kv_rb, seq_q, seq_k, n_batch, n_heads,
               dl, g_sub, lam_init, ctx=None, rope=None, name="attn"):
    tq = _pick_tile(seq_q, (2048, 1024, 512, 256, 128))
    nq = seq_q // tq
    n_keys = seq_k + (ctx[5] if ctx is not None else 0)
    scratch = [pltpu.VMEM((HEAD_W, n_keys), BF16), pltpu.VMEM((n_keys, 2 * HEAD_W), BF16)]
    in_specs = [
        pl.BlockSpec((tq, HEAD_W), lambda b, h, i: ((q_rb + b) * nq + i, q_cb + h)),
        pl.BlockSpec((seq_k, HEAD_W), lambda b, h, i: (kv_rb + b, k_cb + h)),
        pl.BlockSpec((seq_k, HEAD_W), lambda b, h, i: (kv_rb + b, v_cb + h)),
    ]
    args = [q_arr, k_arr, v_arr]
    if ctx is not None:
        kc_arr, kc_cb, vc_arr, vc_cb, c_rb, seq_c = ctx
        in_specs += [
            pl.BlockSpec((seq_c, HEAD_W), lambda b, h, i: (c_rb + b, kc_cb + h)),
            pl.BlockSpec((seq_c, HEAD_W), lambda b, h, i: (c_rb + b, vc_cb + h)),
        ]
        args += [kc_arr, vc_arr]
    if rope is not None:
        assert seq_q == seq_k
        in_specs += [pl.BlockSpec((seq_k, LANES), lambda b, h, i: (0, 0))] * 2
        args += list(rope)
    in_specs += [
        pl.BlockSpec(dl.shape, lambda b, h, i: (0, 0)),
        pl.BlockSpec((1, HEAD_W), lambda b, h, i: (0, 0)),
    ]
    args += [dl, g_sub]
    return pl.pallas_call(
        functools.partial(_attn_body, two_seg=ctx is not None, rope=rope is not None, lam_init=lam_init),
        grid=(n_batch, n_heads, nq),
        in_specs=in_specs,
        out_specs=pl.BlockSpec((tq, HEAD_W), lambda b, h, i: (b * nq + i, h)),
        out_shape=jax.ShapeDtypeStruct((n_batch * seq_q, n_heads * HEAD_W), BF16),
        scratch_shapes=scratch,
        compiler_params=_cparams("arbitrary", "arbitrary", "arbitrary"),
        name=name,
    )(*args)


def _mixer_tail_body(u_ref, v_ref, x_ref, b_ref, c_ref, xp_ref, cp_ref, xn_ref, cn_ref, a_ref,
                     g0_ref, g1_ref, g2_ref, lng_ref, lnb_ref, sgw_ref, sgb_ref, cw_ref, wb_ref, wo_ref,
                     ng_ref, gt_ref, *refs, lat_tiles, tiles_per_seq, split):
    res_refs, o_ref, sg_ref = refs[:-2], refs[-2], refs[-1]
    i = pl.program_id(0)
    tm, w = u_ref.shape
    u = _gelu_tanh(u_ref[...].astype(F32))
    v = _gelu_tanh(v_ref[...].astype(F32))
    mu = jnp.mean(v, axis=-1, keepdims=True)
    dv = v - mu
    v = dv * lax.rsqrt(jnp.mean(dv * dv, axis=-1, keepdims=True) + NORM_EPS)
    v = (v * lng_ref[...] + lnb_ref[...]).astype(BF16)
    gw = w // SG_GROUPS
    for c in range(tm // SG_CHUNK):
        rs = slice(c * SG_CHUNK, (c + 1) * SG_CHUNK)
        for g in range(SG_GROUPS):
            cs = slice(g * gw, (g + 1) * gw)
            mixed = jnp.dot(sgw_ref[g], v[rs, cs], preferred_element_type=F32) + sgb_ref[:, cs]
            sg_ref[rs, cs] = (u[rs, cs] * mixed).astype(sg_ref.dtype)
    is_lat = i < lat_tiles
    seq_first = jnp.logical_or(jnp.logical_not(is_lat), i % tiles_per_seq == 0)
    seq_last = jnp.logical_or(jnp.logical_not(is_lat), i % tiles_per_seq == tiles_per_seq - 1)
    z = c_ref[...].astype(F32) * x_ref[...].astype(F32)
    hp = xp_ref.shape[0]
    z_prev = (cp_ref[...].astype(F32) * xp_ref[...].astype(F32))[hp - 1:hp, :]
    z_next = (cn_ref[...].astype(F32) * xn_ref[...].astype(F32))[0:1, :]
    z_prev = jnp.where(seq_first, 0.0, z_prev)
    z_next = jnp.where(seq_last, 0.0, z_next)
    row = lax.broadcasted_iota(jnp.int32, z.shape, 0)
    z_dn = jnp.where(row == 0, z_prev, pltpu.roll(z, 1, 0))
    z_up = jnp.where(row == tm - 1, z_next, pltpu.roll(z, tm - 1, 0))
    y = cw_ref[0:1, :] * z_dn + cw_ref[1:2, :] * z + cw_ref[2:3, :] * z_up
    cv = (b_ref[...].astype(F32) * y).astype(BF16)
    acc = None
    for g, (br, gl) in enumerate(((a_ref[...], g0_ref), (sg_ref[...], g1_ref), (cv, g2_ref))):
        t = _sigmoid(gl[...].astype(F32)) * jnp.dot(br, wb_ref[g], preferred_element_type=F32)
        acc = t if acc is None else acc + t
    mix = jnp.dot(acc.astype(BF16), wo_ref[...], preferred_element_type=F32)
    o_ref[...] = _stream_tile(res_refs, i, split) + gt_ref[0] * _rms(mix, ng_ref[...])


def _mixer_tail(y, attn_o, stream, rows, lat_rows, seq, n_batch, col0, gate_cb, layer,
                sg_ln_g, sg_ln_b, sg_w, sg_bias_map, conv_w, w_branch, w_out, ng, mod3, gt_idx):
    n_br, mix_w, d = w_branch.shape[1:]
    tm = SEQ_TILE
    halo = 16
    hb = tm // halo
    last_hblk = rows // halo - 1
    cur = lambda c: pl.BlockSpec((tm, mix_w), lambda i: (i, col0 + c))
    prev = lambda c: pl.BlockSpec((halo, mix_w), lambda i: (jnp.maximum(i * hb - 1, 0), col0 + c))
    nxt = lambda c: pl.BlockSpec((halo, mix_w), lambda i: (jnp.minimum((i + 1) * hb, last_hblk), col0 + c))
    gate = lambda g: pl.BlockSpec((tm, d), lambda i: (i, gate_cb + g))
    full = lambda a: pl.BlockSpec(a.shape, lambda i: (0,) * a.ndim)
    consts = [sg_ln_g, sg_ln_b, sg_w, sg_bias_map, conv_w]
    grp = lambda i: _group_of_tile(i, seq // tm, n_batch)
    res_specs, split = _stream_specs(stream, tm, lambda i: i)
    in_specs = ([cur(0), cur(1), cur(2), cur(3), cur(4), prev(2), prev(4), nxt(2), nxt(4),
                 pl.BlockSpec((tm, mix_w), lambda i: (i, 0)), gate(0), gate(1), gate(2)]
                + [full(a) for a in consts]
                + [pl.BlockSpec((None, n_br, mix_w, d), lambda i: (layer, 0, 0, 0), pipeline_mode=pl.Buffered(1)),
                   pl.BlockSpec((None, d, d), lambda i: (layer, 0, 0), pipeline_mode=pl.Buffered(1)),
                   pl.BlockSpec((1, d), lambda i: (0, 0)),
                   pl.BlockSpec((1, 1, d), lambda i: (grp(i), 0, gt_idx))])
    if split is None:
        out_rows, alias = stream.shape[0], {len(in_specs): 0}
    else:
        out_rows, alias = rows, {}
        assert rows == stream[0].shape[0] + stream[1].shape[0]
    return pl.pallas_call(
        functools.partial(_mixer_tail_body, lat_tiles=lat_rows // tm, tiles_per_seq=seq // tm, split=split),
        grid=(rows // tm,),
        in_specs=in_specs + res_specs,
        out_specs=pl.BlockSpec((tm, d), lambda i: (i, 0)),
        out_shape=jax.ShapeDtypeStruct((out_rows, d), F32),
        scratch_shapes=[pltpu.VMEM((tm, mix_w), BF16)],
        input_output_aliases=alias,
        compiler_params=_cparams("parallel"),
        name="mixer_tail",
    )(y, y, y, y, y, y, y, y, y, attn_o, y, y, y, *consts, w_branch, w_out, ng, mod3, *_stream_args(stream))


def _top1_mask(vals, idx, n, axis):
    m = jnp.max(vals, axis=axis, keepdims=True)
    first = jnp.min(jnp.where(vals == m, idx, n), axis=axis, keepdims=True)
    return idx == first, m, first


def _route_body(x_ref, g_ref, sc_ref, sh_ref, wr_ref, rb_ref, hp_ref, wt_ref, ix_ref, rk_ref, cnt_ref):
    h2 = _rms(x_ref[...], g_ref[...]) * (1.0 + sc_ref[0]) + sh_ref[0]
    _store_row_tiles(hp_ref, _pack_bf16_pairs(h2))
    n_exp = wr_ref.shape[0]
    tm = x_ref.shape[0]
    per_group = n_exp // N_GROUPS
    logits = lax.dot_general(wr_ref[...], h2, (((1,), (1,)), ((), ())),
                             precision=HIGHEST, preferred_element_type=F32)
    scores = _sigmoid(logits)
    choice = scores + rb_ref[...]
    neg = -jnp.inf
    ji = lax.broadcasted_iota(jnp.int32, (per_group, tm), 0)
    groups, gs = [], []
    for g in range(N_GROUPS):
        cg = choice[g * per_group:(g + 1) * per_group, :]
        hit, m1, _ = _top1_mask(cg, ji, per_group, 0)
        m2 = jnp.max(jnp.where(hit, neg, cg), axis=0, keepdims=True)
        groups.append(cg)
        gs.append(m1 + m2)
    kept = []
    for g in range(N_GROUPS):
        beaten_by = jnp.zeros((1, tm), jnp.int32)
        for o in range(N_GROUPS):
            if o != g:
                beats = (gs[o] >= gs[g]) if o < g else (gs[o] > gs[g])
                beaten_by = beaten_by + beats.astype(jnp.int32)
        kept.append(jnp.where(beaten_by < TOPK_GROUPS, groups[g], neg))
    cm = jnp.concatenate(kept, axis=0)
    ei = lax.broadcasted_iota(jnp.int32, cm.shape, 0)
    ws, ids, hits = [], [], []
    sel = jnp.zeros(cm.shape, F32)
    for _ in range(TOP_K):
        hit, _, first = _top1_mask(cm, ei, n_exp, 0)
        ws.append(jnp.sum(jnp.where(hit, scores, 0.0), axis=0, keepdims=True))
        ids.append(first)
        hits.append(hit)
        sel = sel + jnp.where(hit, 1.0, 0.0)
        cm = jnp.where(hit, neg, cm)
    w = jnp.concatenate(ws, axis=0)
    wt_ref[...] = w / jnp.sum(w, axis=0, keepdims=True) * ROUTED_SCALE
    ix_ref[...] = jnp.concatenate(ids, axis=0)

    @pl.when(pl.program_id(0) == 0)
    def _():
        cnt_ref[...] = jnp.zeros_like(cnt_ref)

    before = (lax.broadcasted_iota(jnp.int32, (tm, tm), 0)
              < lax.broadcasted_iota(jnp.int32, (tm, tm), 1))
    within = jnp.dot(sel.astype(BF16), jnp.where(before, 1.0, 0.0).astype(BF16),
                     preferred_element_type=F32)
    rank = within + cnt_ref[:, 0:1]
    rk_ref[...] = jnp.concatenate(
        [jnp.sum(jnp.where(h, rank, 0.0), axis=0, keepdims=True) for h in hits], axis=0).astype(jnp.int32)
    cnt_ref[...] = cnt_ref[...] + jnp.sum(sel, axis=1, keepdims=True)


def _route(xa, g, mod3, sc_idx, sh_idx, w_router_t, router_bias, rows, seq, n_batch):
    d = xa.shape[1]
    n_exp = w_router_t.shape[0]
    tm = SEQ_TILE
    tps = seq // tm
    grp = lambda i: _group_of_tile(i, tps, n_batch)
    return pl.pallas_call(
        _route_body,
        grid=(rows // tm,),
        in_specs=[
            pl.BlockSpec((tm, d), lambda i: (i, 0)),
            pl.BlockSpec((1, d), lambda i: (0, 0)),
            pl.BlockSpec((1, 1, d), lambda i: (grp(i), 0, sc_idx)),
            pl.BlockSpec((1, 1, d), lambda i: (grp(i), 0, sh_idx)),
            pl.BlockSpec((n_exp, d), lambda i: (0, 0)),
            pl.BlockSpec((n_exp, 1), lambda i: (0, 0)),
        ],
        out_specs=[
            pl.BlockSpec((tm, SUBLANES, LANES), lambda i: (i, 0, 0)),
            pl.BlockSpec((TOP_K, tm), lambda i: (0, i)),
            pl.BlockSpec((TOP_K, tm), lambda i: (0, i)),
            pl.BlockSpec((TOP_K, tm), lambda i: (0, i)),
            pl.BlockSpec((n_exp, LANES), lambda i: (0, 0)),
        ],
        out_shape=[
            jax.ShapeDtypeStruct((rows, SUBLANES, LANES), U32),
            jax.ShapeDtypeStruct((TOP_K, rows), F32),
            jax.ShapeDtypeStruct((TOP_K, rows), jnp.int32),
            jax.ShapeDtypeStruct((TOP_K, rows), jnp.int32),
            jax.ShapeDtypeStruct((n_exp, LANES), F32),
        ],
        compiler_params=_cparams("arbitrary"),
        name="moe_norm_route",
    )(xa, g, mod3, mod3, w_router_t, router_bias.reshape(n_exp, 1))


def _dispatch_plan(idx_t, rank_t, counts, tile):
    n_exp = counts.shape[0]
    rows = idx_t.shape[1]
    tiles_e = (counts + tile - 1) // tile
    tile_end = jnp.cumsum(tiles_e)
    offs = (tile_end - tiles_e) * tile
    onehot = idx_t[None, :, :] == jnp.arange(n_exp, dtype=jnp.int32)[:, None, None]
    pos = rank_t + jnp.sum(jnp.where(onehot, offs[:, None, None], 0), axis=0)
    max_tiles = (TOP_K * rows) // tile + n_exp
    t = jnp.arange(max_tiles, dtype=jnp.int32)
    used = tile_end[-1]
    t_eff = jnp.minimum(t, used - 1)
    tile_expert = jnp.minimum(jnp.sum(t_eff[:, None] >= tile_end[None, :], axis=1), n_exp - 1)
    used_rows = used * tile
    pad = (jnp.append(offs + counts, used_rows).astype(jnp.int32),
           jnp.append(tiles_e * tile - counts, max_tiles * tile - used_rows).astype(jnp.int32))
    own = tile_expert[:, None] == jnp.arange(n_exp, dtype=jnp.int32)[None, :]
    last_row = jnp.sum(jnp.where(own, (offs + counts)[None, :], 0), axis=1)
    tile_rows = jnp.where(t < used, jnp.clip(last_row - t * tile, 0, tile), 0)
    first = jnp.logical_or(t == 0, tile_expert != jnp.roll(tile_expert, 1))
    slot = (jnp.cumsum(first.astype(jnp.int32)) - 1) % 2
    after = jnp.sum(jnp.where(own, tile_end[None, :], 0), axis=1)
    nxt = jnp.where(after < used, jnp.take(tile_expert, jnp.minimum(after, max_tiles - 1)), -1)
    plan = tuple(a.astype(jnp.int32) for a in (tile_expert, t_eff, tile_rows, first, nxt, slot))
    return pos.astype(jnp.int32), pad, plan, max_tiles


def _scatter_body(pad_start_ref, pad_len_ref, pos_ref, h_ref, xs_ref, zeros, sem, zsem):
    tm = h_ref.shape[0]
    n_exp = pad_start_ref.shape[0] - 1

    @pl.when(pl.program_id(0) == 0)
    def _():
        zeros[...] = jnp.zeros_like(zeros)
        half = zeros.shape[0]
        sizes = [half >> i for i in range(half.bit_length())]
        tail_start = pad_start_ref[n_exp]
        tail_halves = pad_len_ref[n_exp] // half

        def tail_copy(i):
            return pltpu.make_async_copy(zeros, xs_ref.at[pl.ds(tail_start + i * half, half)], zsem)

        lax.fori_loop(0, tail_halves, lambda i, c: (tail_copy(i).start(), c)[1], 0)
        lax.fori_loop(0, tail_halves, lambda i, c: (tail_copy(i).wait(), c)[1], 0)

        def pieces(e, wait):
            start = pad_start_ref[e]
            n = pad_len_ref[e]
            for size in sizes:
                has = (n & size) != 0
                copy = pltpu.make_async_copy(zeros.at[pl.ds(0, size)], xs_ref.at[pl.ds(start, size)], zsem)

                @pl.when(has)
                def _():
                    copy.wait() if wait else copy.start()

                start = start + jnp.where(has, size, 0)

        lax.fori_loop(0, n_exp, lambda e, c: (pieces(e, False), c)[1], 0)
        lax.fori_loop(0, n_exp, lambda e, c: (pieces(e, True), c)[1], 0)

    def issue(r, carry):
        for k in range(TOP_K):
            pltpu.make_async_copy(h_ref.at[r], xs_ref.at[pos_ref[k, r]], sem).start(
                priority=k % 2)
        return carry

    lax.fori_loop(0, tm, issue, 0)
    all_rows = xs_ref.at[pl.ds(0, TOP_K * tm)]
    pltpu.make_async_copy(all_rows, all_rows, sem).wait()


def _scatter(pad_start, pad_len, pos, hp, slots):
    rows = hp.shape[0]
    row_tile = hp.shape[1:]
    tm = SEQ_TILE
    assert EXPERT_TILE & (EXPERT_TILE - 1) == 0
    grid_spec = pltpu.PrefetchScalarGridSpec(
        num_scalar_prefetch=2,
        grid=(rows // tm,),
        in_specs=[
            pl.BlockSpec((TOP_K, tm), lambda i, ps, pn: (0, i), memory_space=pltpu.SMEM),
            pl.BlockSpec((tm,) + row_tile, lambda i, ps, pn: (i, 0, 0)),
        ],
        out_specs=pl.BlockSpec(memory_space=pl.ANY),
        scratch_shapes=[pltpu.VMEM((EXPERT_TILE // 2,) + row_tile, U32),
                        pltpu.SemaphoreType.DMA(()), pltpu.SemaphoreType.DMA(())],
    )
    return pl.pallas_call(
        _scatter_body,
        grid_spec=grid_spec,
        out_shape=jax.ShapeDtypeStruct((slots,) + row_tile, U32),
        compiler_params=_cparams("arbitrary"),
        name="moe_dispatch",
    )(pad_start, pad_len, pos, hp)


def _ffn_rows(xw, wgu_ref, wd_ref):
    x = jnp.concatenate([_unpack_lo(xw).astype(BF16), _unpack_hi(xw).astype(BF16)], axis=1)
    gu = jnp.dot(x, wgu_ref, preferred_element_type=F32)
    f = gu.shape[1] // 2
    hdn = (_silu(gu[:, :f]) * gu[:, f:]).astype(BF16)
    return jnp.dot(hdn, wd_ref, preferred_element_type=F32)


def _experts_body(te_ref, ts_ref, tr_ref, first_ref, next_ref, slot_ref,
                  xs_ref, wg_hbm, wu_hbm, wd_hbm, ys_ref,
                  wg_buf, wu_buf, wd_buf, wgu_bf, wd_bf, sems, *, layer):
    del ts_ref
    t = pl.program_id(0)
    n_real = tr_ref[t]
    tm = xs_ref.shape[0]
    half = tm // 2
    f = wg_buf.shape[-1]

    def weight_copies(e, s):
        return [pltpu.make_async_copy(src.at[layer, e], dst.at[s], sems.at[s])
                for src, dst in ((wg_hbm, wg_buf), (wu_hbm, wu_buf), (wd_hbm, wd_buf))]

    @pl.when(t == 0)
    def _():
        for c in weight_copies(te_ref[0], slot_ref[0]):
            c.start()

    @pl.when(first_ref[t] == 1)
    def _():
        s = slot_ref[t]
        for c in weight_copies(te_ref[t], s):
            c.wait()
        wgu_bf[:, :f] = wg_buf[s].astype(BF16)
        wgu_bf[:, f:] = wu_buf[s].astype(BF16)
        wd_bf[...] = wd_buf[s].astype(BF16)

        @pl.when(next_ref[t] >= 0)
        def _():
            for c in weight_copies(next_ref[t], 1 - s):
                c.start()

    def ffn(rows):
        part = pl.ds(0, rows)
        y = _ffn_rows(_load_row_tiles(xs_ref.at[part]), wgu_bf[...], wd_bf[...])
        _store_row_tiles(ys_ref.at[part], _pack_bf16_pairs(y))

    @pl.when(n_real > half)
    def _():
        ffn(tm)

    @pl.when(jnp.logical_and(n_real > 0, n_real <= half))
    def _():
        ffn(half)
        ys_ref[pl.ds(half, half)] = jnp.zeros((half,) + ys_ref.shape[1:], ys_ref.dtype)

    @pl.when(n_real == 0)
    def _():
        ys_ref[...] = jnp.zeros_like(ys_ref)


def _experts(tile_plan, xs, w_gate, w_up, w_down, layer, max_tiles):
    slots = xs.shape[0]
    row_tile = xs.shape[1:]
    tm = EXPERT_TILE
    d, f = w_gate.shape[2], w_gate.shape[3]
    grid_spec = pltpu.PrefetchScalarGridSpec(
        num_scalar_prefetch=len(tile_plan),
        grid=(max_tiles,),
        in_specs=[
            pl.BlockSpec((tm,) + row_tile, lambda t, te, ts, *_: (ts[t], 0, 0)),
            pl.BlockSpec(memory_space=pl.ANY),
            pl.BlockSpec(memory_space=pl.ANY),
            pl.BlockSpec(memory_space=pl.ANY),
        ],
        out_specs=pl.BlockSpec((tm,) + row_tile, lambda t, *_: (t, 0, 0)),
        scratch_shapes=[pltpu.VMEM((2, d, f), F32), pltpu.VMEM((2, d, f), F32), pltpu.VMEM((2, f, d), F32),
                        pltpu.VMEM((d, 2 * f), BF16), pltpu.VMEM((f, d), BF16),
                        pltpu.SemaphoreType.DMA((2,))],
    )
    return pl.pallas_call(
        functools.partial(_experts_body, layer=layer),
        grid_spec=grid_spec,
        out_shape=jax.ShapeDtypeStruct((slots,) + row_tile, U32),
        compiler_params=_cparams("arbitrary"),
        name="moe_experts",
    )(*tile_plan, xs, w_gate, w_up, w_down)


def _combine_body(pos_ref, pos_next_ref, wt_ref, hp_ref, ys_ref, wgu_ref, wd_ref, x_ref, g_ref, gt_ref,
                  o_ref, gbuf_even, gbuf_odd, sems):
    tm = hp_ref.shape[0]
    i = pl.program_id(0)
    bufs = (gbuf_even, gbuf_odd)

    def row_copy(p_ref, slot, k, r):
        return pltpu.make_async_copy(ys_ref.at[p_ref[k, r]], bufs[slot].at[k, r], sems.at[slot])

    def wait_rows(slot):
        all_rows = ys_ref.at[pl.ds(0, TOP_K * tm)]
        pltpu.make_async_copy(all_rows, all_rows, sems.at[slot]).wait()

    @pl.when(i == 0)
    def _():
        def issue(r, carry):
            for k in range(TOP_K):
                row_copy(pos_ref, 0, k, r).start(priority=k % 2)
            return carry
        lax.fori_loop(0, tm, issue, 0)

    def step(cur):
        wait_rows(cur)
        for r in range(tm):
            for k in range(TOP_K):
                row_copy(pos_next_ref, 1 - cur, k, r).start(priority=k % 2)
        y = _ffn_rows(_load_row_tiles(hp_ref), wgu_ref[...], wd_ref[...])
        half = y.shape[1] // 2
        lo, hi = y[:, :half], y[:, half:]
        wt = wt_ref[...]
        for k in range(TOP_K):
            wk = wt[:, k:k + 1]
            gk = _load_row_tiles(bufs[cur].at[k])
            lo = lo + wk * _unpack_lo(gk)
            hi = hi + wk * _unpack_hi(gk)
        y = jnp.concatenate([lo, hi], axis=1)
        o_ref[...] = x_ref[...] + gt_ref[0] * _rms(y, g_ref[...])

        @pl.when(i == pl.num_programs(0) - 1)
        def _():
            wait_rows(1 - cur)

    for parity in range(2):
        pl.when(i % 2 == parity)(functools.partial(step, parity))


def _combine(pos, wts, hp, ys, w_sgu, w_sd, xa, g, mod3, gt_idx, rows, seq, n_batch, in_place):
    d = xa.shape[1]
    row_tile = hp.shape[1:]
    tm = SEQ_TILE
    tps = seq // tm
    grp = lambda i: _group_of_tile(i, tps, n_batch)
    last = rows // tm - 1
    return pl.pallas_call(
        _combine_body,
        grid=(rows // tm,),
        in_specs=[
            pl.BlockSpec((TOP_K, tm), lambda i: (0, i), memory_space=pltpu.SMEM),
            pl.BlockSpec((TOP_K, tm), lambda i: (0, jnp.minimum(i + 1, last)), memory_space=pltpu.SMEM),
            pl.BlockSpec((tm, TOP_K), lambda i: (i, 0)),
            pl.BlockSpec((tm,) + row_tile, lambda i: (i, 0, 0)),
            pl.BlockSpec(memory_space=pl.ANY),
            pl.BlockSpec(w_sgu.shape, lambda i: (0, 0), pipeline_mode=pl.Buffered(1)),
            pl.BlockSpec(w_sd.shape, lambda i: (0, 0), pipeline_mode=pl.Buffered(1)),
            pl.BlockSpec((tm, d), lambda i: (i, 0)),
            pl.BlockSpec((1, d), lambda i: (0, 0)),
            pl.BlockSpec((1, 1, d), lambda i: (grp(i), 0, gt_idx)),
        ],
        out_specs=pl.BlockSpec((tm, d), lambda i: (i, 0)),
        out_shape=jax.ShapeDtypeStruct(xa.shape if in_place else (rows, d), F32),
        scratch_shapes=[pltpu.VMEM((TOP_K, tm) + row_tile, U32), pltpu.VMEM((TOP_K, tm) + row_tile, U32),
                        pltpu.SemaphoreType.DMA((2,))],
        input_output_aliases={7: 0} if in_place else {},
        compiler_params=_cparams("arbitrary"),
        name="moe_combine",
    )(pos, pos, wts, hp, ys, w_sgu, w_sd, xa, g, mod3)


def kernel(x, c, ctx, c_ctx, w_ada, b_ada, norm_g, w_in, diff_lambda, g_subln, sg_ln_g, sg_ln_b, sg_w, sg_b, conv_w, w_branch, w_out, w_router, router_bias, w_exp_gate, w_exp_up, w_exp_down, w_sh_gate, w_sh_up, w_sh_down):
    n_batch, seq, d = x.shape
    ctx_len = ctx.shape[1]
    depth = w_ada.shape[0]
    mix_w = w_branch.shape[2]
    n_heads = mix_w // HEAD_W
    n_exp = w_router.shape[2]
    n_lat = n_batch * seq
    n_ctx = n_batch * ctx_len
    n_all = n_lat + n_ctx
    assert n_batch + 1 <= MOD_ROWS and seq % SEQ_TILE == 0 and ctx_len % SEQ_TILE == 0
    assert n_lat % seq == 0 and n_lat % ctx_len == 0 and mix_w // SG_GROUPS == LANES
    assert d // 2 == SUBLANES * LANES

    xa = (x.reshape(n_lat, d), ctx.reshape(n_ctx, d))
    if depth == 1:
        xa = _concat_rows(*xa)
    cond = jnp.zeros((MOD_ROWS, d), F32).at[:n_batch].set(c).at[n_batch].set(c_ctx)
    mod = _ada(cond, w_ada, b_ada)
    cos, sin = _rope_tables(seq)
    q_cb, k_cb, v_cb = 0, n_heads, 2 * n_heads
    gate_cb = (8 * mix_w) // d

    w_in_b = w_in.astype(BF16)
    w_branch_b = w_branch.astype(BF16)
    w_out_b = w_out.astype(BF16)
    n_in = w_in.shape[2]

    for l in range(depth):
        last = l == depth - 1
        lam_init = 0.8 - 0.6 * math.exp(-0.3 * l)
        mod3 = mod[l].reshape(MOD_ROWS, 1, N_MOD * d)
        ng = norm_g[l]
        rows = n_lat if last else n_all

        in_proj = functools.partial(_norm_matmul, xa, ng[0:1], mod3, 1, 0, w_in_b, l,
                                    seq=seq, n_batch=n_batch)
        if last:
            y = in_proj(n_lat, 0, n_in, 0, name="in_proj")
            yc = in_proj(n_ctx, n_lat, 2 * mix_w, mix_w, name="in_proj_ctx_kv")
            ctx_kv = (yc, 0, yc, n_heads, 0, ctx_len)
        else:
            y = in_proj(n_all, 0, n_in, 0, name="in_proj")
            ctx_kv = (y, k_cb, y, v_cb, n_lat // ctx_len, ctx_len)
        attn_o = _attention(y, q_cb, 0, y, k_cb, y, v_cb, 0, seq, seq, n_batch, n_heads,
                            diff_lambda[l], g_subln[l:l + 1], lam_init, ctx=ctx_kv, rope=(cos, sin),
                            name="attn_latent")
        if not last:
            attn_c = _attention(y, q_cb, n_lat // ctx_len, y, k_cb, y, v_cb, n_lat // ctx_len,
                                ctx_len, ctx_len, n_batch, n_heads,
                                diff_lambda[l], g_subln[l:l + 1], lam_init, name="attn_ctx")
            attn_o = jnp.concatenate([attn_o, attn_c], axis=0)
        sg_bias_map = jnp.repeat(sg_b[l].T, mix_w // SG_GROUPS, axis=1)
        xa = _mixer_tail(y, attn_o, xa, rows, n_lat, seq, n_batch, 3, gate_cb, l,
                         sg_ln_g[l:l + 1], sg_ln_b[l:l + 1], sg_w[l].astype(BF16), sg_bias_map, conv_w[l],
                         w_branch_b, w_out_b, ng[1:2], mod3, 2)

        hp, wt_t, idx_t, rank_t, cnt = _route(xa, ng[2:3], mod3, 4, 3, w_router[l].T, router_bias[l],
                                              rows, seq, n_batch)
        pos, pad, tile_plan, max_tiles = _dispatch_plan(
            idx_t, rank_t, cnt[:, 0].astype(jnp.int32), EXPERT_TILE)
        xs = _scatter(*pad, pos, hp, max_tiles * EXPERT_TILE)
        ys = _experts(tile_plan, xs, w_exp_gate, w_exp_up, w_exp_down, l, max_tiles)
        w_sgu = jnp.concatenate([w_sh_gate[l], w_sh_up[l]], axis=-1).astype(BF16)
        xa = _combine(pos, wt_t.T, hp, ys, w_sgu, w_sh_down[l].astype(BF16), xa, ng[3:4], mod3, 5,
                      rows, seq, n_batch, in_place=not last)

    return xa.reshape(n_batch, seq, d)
```

```python
import functools
import math

import jax
import jax.numpy as jnp
from jax import lax
from jax.experimental import pallas as pl
from jax.experimental.pallas import tpu as pltpu

GRID_W = 64
QK_DIM = 64
HEAD_W = 2 * QK_DIM
ROPE_BASE = 10000.0
SG_CHUNK = 128
SG_GROUPS = 8
TOP_K = 8
N_GROUPS = 8
TOPK_GROUPS = 4
ROUTED_SCALE = 2.5
NORM_EPS = 1e-6
N_MOD = 6

LANES = 128
SUBLANES = 8
VMEM_LIMIT = 56 * 1024 * 1024
SEQ_TILE = 256
EXPERT_TILE = 512
MOD_ROWS = 16
ATTN_SUB_ROWS = 128

F32 = jnp.float32
BF16 = jnp.bfloat16
U32 = jnp.uint32
HIGHEST = lax.Precision.HIGHEST


def _cparams(*sem):
    return pltpu.CompilerParams(dimension_semantics=sem, vmem_limit_bytes=VMEM_LIMIT)


def _pick_tile(n, cands):
    for c in cands:
        if n % c == 0:
            return c
    raise ValueError(f"no tile for {n} in {cands}")


def _sigmoid(x):
    return 1.0 / (1.0 + jnp.exp(-x))


def _silu(x):
    return x * _sigmoid(x)


def _gelu_tanh(x):
    c = math.sqrt(2.0 / math.pi)
    return x * (0.5 * (1.0 + jnp.tanh(c * (x + 0.044715 * (x * x * x)))))


def _rms(x, g):
    return x * lax.rsqrt(jnp.mean(x * x, axis=-1, keepdims=True) + NORM_EPS) * g


def _pack_bf16_pairs(x):
    w = x.shape[1] // 2
    bits = pltpu.bitcast(x.astype(BF16).astype(F32), U32)
    return (bits[:, :w] >> 16) | bits[:, w:]


def _unpack_lo(w):
    return pltpu.bitcast(w << 16, F32)


def _unpack_hi(w):
    return pltpu.bitcast(w & jnp.uint32(0xFFFF0000), F32)


def _store_row_tiles(ref, packed):
    m, s, _ = ref.shape
    flat = ref.reshape(m * s, LANES)
    for j in range(s):
        flat[pl.ds(j, m, stride=s), :] = packed[:, j * LANES:(j + 1) * LANES]


def _load_row_tiles(ref):
    m, s, _ = ref.shape
    flat = ref.reshape(m * s, LANES)
    return jnp.concatenate([flat[pl.ds(j, m, stride=s), :] for j in range(s)], axis=1)


def _concat_body(a_ref, b_ref, o_ref, *, a_tiles):
    i = pl.program_id(0)

    @pl.when(i < a_tiles)
    def _():
        o_ref[...] = a_ref[...]

    @pl.when(i >= a_tiles)
    def _():
        o_ref[...] = b_ref[...]


def _concat_rows(a, b):
    d = a.shape[1]
    tm = _pick_tile(math.gcd(a.shape[0], b.shape[0]), (512, 256))
    na, nb = a.shape[0] // tm, b.shape[0] // tm
    return pl.pallas_call(
        functools.partial(_concat_body, a_tiles=na),
        grid=(na + nb,),
        in_specs=[
            pl.BlockSpec((tm, d), lambda i: (jnp.minimum(i, na - 1), 0)),
            pl.BlockSpec((tm, d), lambda i: (jnp.maximum(i - na, 0), 0)),
        ],
        out_specs=pl.BlockSpec((tm, d), lambda i: (i, 0)),
        out_shape=jax.ShapeDtypeStruct((a.shape[0] + b.shape[0], d), a.dtype),
        compiler_params=_cparams("parallel"),
        name="concat_rows",
    )(a, b)


def _ada_body(c_ref, w_ref, b_ref, o_ref):
    a = _silu(c_ref[...])
    o_ref[0] = jnp.dot(a, w_ref[0], precision=HIGHEST, preferred_element_type=F32) + b_ref[0]


def _ada(cond, w_ada, b_ada):
    depth, d, n = w_ada.shape
    tn = _pick_tile(n, (1024, 512, 256, 128))
    return pl.pallas_call(
        _ada_body,
        grid=(depth, n // tn),
        in_specs=[
            pl.BlockSpec((MOD_ROWS, d), lambda l, j: (0, 0)),
            pl.BlockSpec((1, d, tn), lambda l, j: (l, 0, j)),
            pl.BlockSpec((1, 1, tn), lambda l, j: (l, 0, j)),
        ],
        out_specs=pl.BlockSpec((1, MOD_ROWS, tn), lambda l, j: (l, 0, j)),
        out_shape=jax.ShapeDtypeStruct((depth, MOD_ROWS, n), F32),
        compiler_params=_cparams("arbitrary", "arbitrary"),
        name="ada_mod",
    )(cond, w_ada, b_ada.reshape(depth, 1, n))


def _group_of_tile(i, tiles_per_seq, n_batch):
    return jnp.minimum(i // tiles_per_seq, n_batch)


def _stream_specs(stream, tm, tile_of):
    if not isinstance(stream, tuple):
        return [pl.BlockSpec((tm, stream.shape[1]), lambda *idx: (tile_of(*idx), 0))], None
    a, b = stream
    split = a.shape[0] // tm
    assert a.shape[0] % tm == 0 and b.shape[0] % tm == 0
    return [pl.BlockSpec((tm, a.shape[1]), lambda *idx: (jnp.minimum(tile_of(*idx), split - 1), 0)),
            pl.BlockSpec((tm, b.shape[1]), lambda *idx: (jnp.maximum(tile_of(*idx) - split, 0), 0),
                         pipeline_mode=pl.Buffered(1))], split


def _stream_tile(refs, tile, split):
    if split is None:
        return refs[0][...]
    return jnp.where(tile < split, refs[0][...], refs[1][...])


def _stream_args(stream):
    return list(stream) if isinstance(stream, tuple) else [stream]


def _norm_mm_body(*refs, split, tile0, n_tiles, chunks):
    n_src = 1 if split is None else 2
    g_ref, sc_ref, sh_ref, w_ref, o_ref, h_even, h_odd = refs[n_src:]
    bufs = (h_even, h_odd)
    i, j = pl.program_id(0), pl.program_id(1)
    rpc = h_even.shape[0] // chunks
    rows = pl.ds(pl.multiple_of(jnp.minimum(j, chunks - 1) * rpc, rpc), rpc)
    tile = jnp.minimum(i, n_tiles - 1) + tile0

    def fill(h_ref):
        x = refs[0][rows, :]
        if split is not None:
            x = jnp.where(tile < split, x, refs[1][rows, :])
        y = _rms(x, g_ref[...])
        h_ref[rows, :] = (y * (1.0 + sc_ref[0]) + sh_ref[0]).astype(h_ref.dtype)

    @pl.when(i == 0)
    def _():
        fill(h_even)
        o_ref[...] = jnp.zeros_like(o_ref)

    for p in range(2):
        @pl.when(jnp.logical_and(i >= 1, i % 2 == p))
        def _(p=p):
            fill(bufs[p])
            o_ref[...] = jnp.dot(bufs[1 - p][...], w_ref[...],
                                 preferred_element_type=F32).astype(o_ref.dtype)


def _norm_matmul(stream, g, mod3, sc_idx, sh_idx, w, layer, rows, row_off, cols, col_off, seq, n_batch, name):
    d = w.shape[1]
    tm = _pick_tile(math.gcd(math.gcd(rows, row_off) if row_off else rows, seq), (1024, 512, 256))
    tn = _pick_tile(math.gcd(cols, col_off) if col_off else cols, (1024, 512, 256))
    ro, co = row_off // tm, col_off // tn
    tps = seq // tm
    n_tiles, n_col = rows // tm, cols // tn
    chunks = min(8, n_col)
    assert chunks & (chunks - 1) == 0
    src_tile = lambda i, j: jnp.minimum(i, n_tiles - 1) + ro
    grp = lambda i: _group_of_tile(src_tile(i, 0), tps, n_batch)
    x_specs, split = _stream_specs(stream, tm, src_tile)
    return pl.pallas_call(
        functools.partial(_norm_mm_body, split=split, tile0=ro, n_tiles=n_tiles, chunks=chunks),
        grid=(n_tiles + 1, n_col),
        in_specs=x_specs + [
            pl.BlockSpec((1, d), lambda i, j: (0, 0)),
            pl.BlockSpec((1, 1, d), lambda i, j: (grp(i), 0, sc_idx)),
            pl.BlockSpec((1, 1, d), lambda i, j: (grp(i), 0, sh_idx)),
            pl.BlockSpec((None, d, tn), lambda i, j: (layer, 0, j + co)),
        ],
        out_specs=pl.BlockSpec((tm, tn), lambda i, j: (jnp.where(i == 0, n_tiles, i - 1), j)),
        out_shape=jax.ShapeDtypeStruct((rows + tm, cols), BF16),
        scratch_shapes=[pltpu.VMEM((tm, d), BF16), pltpu.VMEM((tm, d), BF16)],
        compiler_params=_cparams("arbitrary", "arbitrary"),
        name=name,
    )(*_stream_args(stream), g, mod3, mod3, w)


def _rope_apply(x, cos, sin):
    lane = lax.broadcasted_iota(jnp.int32, x.shape, 1)
    first_half = (lane % (QK_DIM // 2)) < (QK_DIM // 4)
    partner = jnp.where(first_half,
                        pltpu.roll(x, LANES - QK_DIM // 4, 1),
                        pltpu.roll(x, QK_DIM // 4, 1))
    return x * cos + partner * sin


def _rope_tables(seq):
    n_freq = QK_DIM // 4
    inv = ROPE_BASE ** (-jnp.arange(n_freq, dtype=F32) / n_freq)
    t = jnp.arange(seq)
    row = (t // GRID_W).astype(F32)
    col = (t % GRID_W).astype(F32)
    ang = jnp.stack([row[:, None] * inv, col[:, None] * inv], axis=1)
    cos = jnp.cos(ang)
    sin = jnp.sin(ang)
    cos64 = jnp.stack([cos, cos], axis=2).reshape(seq, QK_DIM)
    sin64 = jnp.stack([-sin, sin], axis=2).reshape(seq, QK_DIM)
    reps = LANES // QK_DIM
    return jnp.tile(cos64, (1, reps)), jnp.tile(sin64, (1, reps))


def _attn_body(*refs, two_seg, rope, lam_init):
    refs = list(refs)
    q_ref, k_ref, v_ref = refs[:3]
    del refs[:3]
    segs = []
    if two_seg:
        segs.append((refs[0], refs[1], False))
        del refs[:2]
    segs.append((k_ref, v_ref, rope))
    if rope:
        cos_ref, sin_ref = refs[:2]
        del refs[:2]
    dl_ref, g_ref, o_ref, kt_ref, v1_ref = refs
    tq = q_ref.shape[0]

    @pl.when(pl.program_id(2) == 0)
    def _():
        row = 0
        for ks_ref, vs_ref, rotate in segs:
            n = ks_ref.shape[0]
            ks = ks_ref[...].astype(F32)
            if rotate:
                ks = _rope_apply(ks, cos_ref[...], sin_ref[...])
            kt_ref[:, row:row + n] = ks.T.astype(BF16)
            v1_ref[row:row + n, :HEAD_W] = vs_ref[...]
            v1_ref[row:row + n, HEAD_W:] = jnp.ones(vs_ref.shape, BF16)
            row += n

    dl = dl_ref[...]
    lam = (jnp.exp(jnp.sum(dl[0:1] * dl[1:2], axis=-1, keepdims=True))
           - jnp.exp(jnp.sum(dl[2:3] * dl[3:4], axis=-1, keepdims=True)) + lam_init)
    q = q_ref[...]
    if rope:
        q_rows = pl.ds(pl.multiple_of(pl.program_id(2) * tq, tq), tq)
        q = _rope_apply(q.astype(F32), cos_ref[q_rows, :], sin_ref[q_rows, :]).astype(BF16)
    q = q * jnp.asarray(QK_DIM ** -0.5, BF16)
    lane = lax.broadcasted_iota(jnp.int32, q.shape, 1)
    zero = jnp.zeros_like(q)

    def softmax_v(qm):
        s = jnp.dot(qm, kt_ref[...], preferred_element_type=F32)
        m = jnp.max(s, axis=-1, keepdims=True)
        acc = jnp.dot(jnp.exp(s - m).astype(BF16), v1_ref[...], preferred_element_type=F32)
        return acc[:, :HEAD_W] / acc[:, HEAD_W:]

    q1 = jnp.where(lane < QK_DIM, q, zero)
    q2 = jnp.where(lane >= QK_DIM, q, zero)
    sub = min(q.shape[0], ATTN_SUB_ROWS)
    for r0 in range(0, q.shape[0], sub):
        rs = slice(r0, r0 + sub)
        o = softmax_v(q1[rs]) - lam * softmax_v(q2[rs])
        o_ref[rs, :] = (_rms(o, g_ref[...]) * (1.0 - lam_init)).astype(o_ref.dtype)


def _attention(q_arr, q_cb, q_rb, k_arr, k_cb, v_arr, v_cb, kv_rb, seq_q, seq_k, n_batch, n_heads,
               dl, g_sub, lam_init, ctx=None, rope=None, name="attn"):
    tq = _pick_tile(seq_q, (2048, 1024, 512, 256, 128))
    nq = seq_q // tq
    n_keys = seq_k + (ctx[5] if ctx is not None else 0)
    scratch = [pltpu.VMEM((HEAD_W, n_keys), BF16), pltpu.VMEM((n_keys, 2 * HEAD_W), BF16)]
    in_specs = [
        pl.BlockSpec((tq, HEAD_W), lambda b, h, i: ((q_rb + b) * nq + i, q_cb + h)),
        pl.BlockSpec((seq_k, HEAD_W), lambda b, h, i: (kv_rb + b, k_cb + h)),
        pl.BlockSpec((seq_k, HEAD_W), lambda b, h, i: (kv_rb + b, v_cb + h)),
    ]
    args = [q_arr, k_arr, v_arr]
    if ctx is not None:
        kc_arr, kc_cb, vc_arr, vc_cb, c_rb, seq_c = ctx
        in_specs += [
            pl.BlockSpec((seq_c, HEAD_W), lambda b, h, i: (c_rb + b, kc_cb + h)),
            pl.BlockSpec((seq_c, HEAD_W), lambda b, h, i: (c_rb + b, vc_cb + h)),
        ]
        args += [kc_arr, vc_arr]
    if rope is not None:
        assert seq_q == seq_k
        in_specs += [pl.BlockSpec((seq_k, LANES), lambda b, h, i: (0, 0))] * 2
        args += list(rope)
    in_specs += [
        pl.BlockSpec(dl.shape, lambda b, h, i: (0, 0)),
        pl.BlockSpec((1, HEAD_W), lambda b, h, i: (0, 0)),
    ]
    args += [dl, g_sub]
    return pl.pallas_call(
        functools.partial(_attn_body, two_seg=ctx is not None, rope=rope is not None, lam_init=lam_init),
        grid=(n_batch, n_heads, nq),
        in_specs=in_specs,
        out_specs=pl.BlockSpec((tq, HEAD_W), lambda b, h, i: (b * nq + i, h)),
        out_shape=jax.ShapeDtypeStruct((n_batch * seq_q, n_heads * HEAD_W), BF16),
        scratch_shapes=scratch,
        compiler_params=_cparams("arbitrary", "arbitrary", "arbitrary"),
        name=name,
    )(*args)


def _mixer_tail_body(u_ref, v_ref, x_ref, b_ref, c_ref, xp_ref, cp_ref, xn_ref, cn_ref, a_ref,
                     g0_ref, g1_ref, g2_ref, lng_ref, lnb_ref, sgw_ref, sgb_ref, cw_ref, wb_ref, wo_ref,
                     ng_ref, gt_ref, *refs, lat_tiles, tiles_per_seq, split):
    res_refs, o_ref, sg_ref = refs[:-2], refs[-2], refs[-1]
    i = pl.program_id(0)
    tm, w = u_ref.shape

    def gated(branch, gate_ref, g):
        return _sigmoid(gate_ref[...].astype(F32)) * jnp.dot(branch, wb_ref[g], preferred_element_type=F32)

    acc = gated(a_ref[...], g0_ref, 0)
    u = _gelu_tanh(u_ref[...].astype(F32))
    v = _gelu_tanh(v_ref[...].astype(F32))
    mu = jnp.mean(v, axis=-1, keepdims=True)
    dv = v - mu
    v = dv * lax.rsqrt(jnp.mean(dv * dv, axis=-1, keepdims=True) + NORM_EPS)
    v = (v * lng_ref[...] + lnb_ref[...]).astype(BF16)
    gw = w // SG_GROUPS
    for c in range(tm // SG_CHUNK):
        rs = slice(c * SG_CHUNK, (c + 1) * SG_CHUNK)
        for g in range(SG_GROUPS):
            cs = slice(g * gw, (g + 1) * gw)
            mixed = jnp.dot(sgw_ref[g], v[rs, cs], preferred_element_type=F32) + sgb_ref[:, cs]
            sg_ref[rs, cs] = (u[rs, cs] * mixed).astype(sg_ref.dtype)
    is_lat = i < lat_tiles
    seq_first = jnp.logical_or(jnp.logical_not(is_lat), i % tiles_per_seq == 0)
    seq_last = jnp.logical_or(jnp.logical_not(is_lat), i % tiles_per_seq == tiles_per_seq - 1)
    z = c_ref[...].astype(F32) * x_ref[...].astype(F32)
    hp = xp_ref.shape[0]
    z_prev = (cp_ref[...].astype(F32) * xp_ref[...].astype(F32))[hp - 1:hp, :]
    z_next = (cn_ref[...].astype(F32) * xn_ref[...].astype(F32))[0:1, :]
    z_prev = jnp.where(seq_first, 0.0, z_prev)
    z_next = jnp.where(seq_last, 0.0, z_next)
    row = lax.broadcasted_iota(jnp.int32, z.shape, 0)
    z_dn = jnp.where(row == 0, z_prev, pltpu.roll(z, 1, 0))
    z_up = jnp.where(row == tm - 1, z_next, pltpu.roll(z, tm - 1, 0))
    y = cw_ref[0:1, :] * z_dn + cw_ref[1:2, :] * z + cw_ref[2:3, :] * z_up
    cv = (b_ref[...].astype(F32) * y).astype(BF16)
    acc = acc + gated(sg_ref[...], g1_ref, 1) + gated(cv, g2_ref, 2)
    mix = jnp.dot(acc.astype(BF16), wo_ref[...], preferred_element_type=F32)
    o_ref[...] = _stream_tile(res_refs, i, split) + gt_ref[0] * _rms(mix, ng_ref[...])


def _mixer_tail(y, attn_o, stream, rows, lat_rows, seq, n_batch, col0, gate_cb, layer,
                sg_ln_g, sg_ln_b, sg_w, sg_bias_map, conv_w, w_branch, w_out, ng, mod3, gt_idx):
    n_br, mix_w, d = w_branch.shape[1:]
    tm = SEQ_TILE
    halo = 16
    hb = tm // halo
    last_hblk = rows // halo - 1
    cur = lambda c: pl.BlockSpec((tm, mix_w), lambda i: (i, col0 + c))
    prev = lambda c: pl.BlockSpec((halo, mix_w), lambda i: (jnp.maximum(i * hb - 1, 0), col0 + c))
    nxt = lambda c: pl.BlockSpec((halo, mix_w), lambda i: (jnp.minimum((i + 1) * hb, last_hblk), col0 + c))
    gate = lambda g: pl.BlockSpec((tm, d), lambda i: (i, gate_cb + g))
    full = lambda a: pl.BlockSpec(a.shape, lambda i: (0,) * a.ndim)
    consts = [sg_ln_g, sg_ln_b, sg_w, sg_bias_map, conv_w]
    grp = lambda i: _group_of_tile(i, seq // tm, n_batch)
    res_specs, split = _stream_specs(stream, tm, lambda i: i)
    in_specs = ([cur(0), cur(1), cur(2), cur(3), cur(4), prev(2), prev(4), nxt(2), nxt(4),
                 pl.BlockSpec((tm, mix_w), lambda i: (i, 0)), gate(0), gate(1), gate(2)]
                + [full(a) for a in consts]
                + [pl.BlockSpec((None, n_br, mix_w, d), lambda i: (layer, 0, 0, 0), pipeline_mode=pl.Buffered(1)),
                   pl.BlockSpec((None, d, d), lambda i: (layer, 0, 0), pipeline_mode=pl.Buffered(1)),
                   pl.BlockSpec((1, d), lambda i: (0, 0)),
                   pl.BlockSpec((1, 1, d), lambda i: (grp(i), 0, gt_idx))])
    if split is None:
        out_rows, alias = stream.shape[0], {len(in_specs): 0}
    else:
        out_rows, alias = rows, {}
        assert rows == stream[0].shape[0] + stream[1].shape[0]
    return pl.pallas_call(
        functools.partial(_mixer_tail_body, lat_tiles=lat_rows // tm, tiles_per_seq=seq // tm, split=split),
        grid=(rows // tm,),
        in_specs=in_specs + res_specs,
        out_specs=pl.BlockSpec((tm, d), lambda i: (i, 0)),
        out_shape=jax.ShapeDtypeStruct((out_rows, d), F32),
        scratch_shapes=[pltpu.VMEM((tm, mix_w), BF16)],
        input_output_aliases=alias,
        compiler_params=_cparams("parallel"),
        name="mixer_tail",
    )(y, y, y, y, y, y, y, y, y, attn_o, y, y, y, *consts, w_branch, w_out, ng, mod3, *_stream_args(stream))


def _top1_mask(vals, idx, n, axis):
    m = jnp.max(vals, axis=axis, keepdims=True)
    first = jnp.min(jnp.where(vals == m, idx, n), axis=axis, keepdims=True)
    return idx == first, m, first


def _route_body(x_ref, g_ref, sc_ref, sh_ref, wr_ref, rb_ref, hp_ref, wt_ref, ix_ref, rk_ref, cnt_ref):
    h2 = _rms(x_ref[...], g_ref[...]) * (1.0 + sc_ref[0]) + sh_ref[0]
    _store_row_tiles(hp_ref, _pack_bf16_pairs(h2))
    n_exp = wr_ref.shape[0]
    tm = x_ref.shape[0]
    per_group = n_exp // N_GROUPS
    logits = lax.dot_general(wr_ref[...], h2, (((1,), (1,)), ((), ())),
                             precision=HIGHEST, preferred_element_type=F32)
    scores = _sigmoid(logits)
    choice = scores + rb_ref[...]
    neg = -jnp.inf
    ji = lax.broadcasted_iota(jnp.int32, (per_group, tm), 0)
    groups, gs = [], []
    for g in range(N_GROUPS):
        cg = choice[g * per_group:(g + 1) * per_group, :]
        hit, m1, _ = _top1_mask(cg, ji, per_group, 0)
        m2 = jnp.max(jnp.where(hit, neg, cg), axis=0, keepdims=True)
        groups.append(cg)
        gs.append(m1 + m2)
    kept = []
    for g in range(N_GROUPS):
        beaten_by = jnp.zeros((1, tm), jnp.int32)
        for o in range(N_GROUPS):
            if o != g:
                beats = (gs[o] >= gs[g]) if o < g else (gs[o] > gs[g])
                beaten_by = beaten_by + beats.astype(jnp.int32)
        kept.append(jnp.where(beaten_by < TOPK_GROUPS, groups[g], neg))
    cm = jnp.concatenate(kept, axis=0)
    ei = lax.broadcasted_iota(jnp.int32, cm.shape, 0)
    ws, ids, hits = [], [], []
    sel = jnp.zeros(cm.shape, F32)
    for _ in range(TOP_K):
        hit, _, first = _top1_mask(cm, ei, n_exp, 0)
        ws.append(jnp.sum(jnp.where(hit, scores, 0.0), axis=0, keepdims=True))
        ids.append(first)
        hits.append(hit)
        sel = sel + jnp.where(hit, 1.0, 0.0)
        cm = jnp.where(hit, neg, cm)
    w = jnp.concatenate(ws, axis=0)
    wt_ref[...] = w / jnp.sum(w, axis=0, keepdims=True) * ROUTED_SCALE
    ix_ref[...] = jnp.concatenate(ids, axis=0)

    @pl.when(pl.program_id(0) == 0)
    def _():
        cnt_ref[...] = jnp.zeros_like(cnt_ref)

    before = (lax.broadcasted_iota(jnp.int32, (tm, tm), 0)
              < lax.broadcasted_iota(jnp.int32, (tm, tm), 1))
    within = jnp.dot(sel.astype(BF16), jnp.where(before, 1.0, 0.0).astype(BF16),
                     preferred_element_type=F32)
    rank = within + cnt_ref[:, 0:1]
    rk_ref[...] = jnp.concatenate(
        [jnp.sum(jnp.where(h, rank, 0.0), axis=0, keepdims=True) for h in hits], axis=0).astype(jnp.int32)
    cnt_ref[...] = cnt_ref[...] + jnp.sum(sel, axis=1, keepdims=True)


def _route(xa, g, mod3, sc_idx, sh_idx, w_router_t, router_bias, rows, seq, n_batch):
    d = xa.shape[1]
    n_exp = w_router_t.shape[0]
    tm = SEQ_TILE
    tps = seq // tm
    grp = lambda i: _group_of_tile(i, tps, n_batch)
    return pl.pallas_call(
        _route_body,
        grid=(rows // tm,),
        in_specs=[
            pl.BlockSpec((tm, d), lambda i: (i, 0)),
            pl.BlockSpec((1, d), lambda i: (0, 0)),
            pl.BlockSpec((1, 1, d), lambda i: (grp(i), 0, sc_idx)),
            pl.BlockSpec((1, 1, d), lambda i: (grp(i), 0, sh_idx)),
            pl.BlockSpec((n_exp, d), lambda i: (0, 0)),
            pl.BlockSpec((n_exp, 1), lambda i: (0, 0)),
        ],
        out_specs=[
            pl.BlockSpec((tm, SUBLANES, LANES), lambda i: (i, 0, 0)),
            pl.BlockSpec((TOP_K, tm), lambda i: (0, i)),
            pl.BlockSpec((TOP_K, tm), lambda i: (0, i)),
            pl.BlockSpec((TOP_K, tm), lambda i: (0, i)),
            pl.BlockSpec((n_exp, LANES), lambda i: (0, 0)),
        ],
        out_shape=[
            jax.ShapeDtypeStruct((rows, SUBLANES, LANES), U32),
            jax.ShapeDtypeStruct((TOP_K, rows), F32),
            jax.ShapeDtypeStruct((TOP_K, rows), jnp.int32),
            jax.ShapeDtypeStruct((TOP_K, rows), jnp.int32),
            jax.ShapeDtypeStruct((n_exp, LANES), F32),
        ],
        compiler_params=_cparams("arbitrary"),
        name="moe_norm_route",
    )(xa, g, mod3, mod3, w_router_t, router_bias.reshape(n_exp, 1))


def _dispatch_plan(idx_t, rank_t, counts, tile):
    n_exp = counts.shape[0]
    rows = idx_t.shape[1]
    tiles_e = (counts + tile - 1) // tile
    tile_end = jnp.cumsum(tiles_e)
    offs = (tile_end - tiles_e) * tile
    onehot = idx_t[None, :, :] == jnp.arange(n_exp, dtype=jnp.int32)[:, None, None]
    pos = rank_t + jnp.sum(jnp.where(onehot, offs[:, None, None], 0), axis=0)
    max_tiles = (TOP_K * rows) // tile + n_exp
    t = jnp.arange(max_tiles, dtype=jnp.int32)
    used = tile_end[-1]
    t_eff = jnp.minimum(t, used - 1)
    tile_expert = jnp.minimum(jnp.sum(t_eff[:, None] >= tile_end[None, :], axis=1), n_exp - 1)
    used_rows = used * tile
    pad = (jnp.append(offs + counts, used_rows).astype(jnp.int32),
           jnp.append(tiles_e * tile - counts, max_tiles * tile - used_rows).astype(jnp.int32))
    own = tile_expert[:, None] == jnp.arange(n_exp, dtype=jnp.int32)[None, :]
    last_row = jnp.sum(jnp.where(own, (offs + counts)[None, :], 0), axis=1)
    tile_rows = jnp.where(t < used, jnp.clip(last_row - t * tile, 0, tile), 0)
    first = jnp.logical_or(t == 0, tile_expert != jnp.roll(tile_expert, 1))
    slot = (jnp.cumsum(first.astype(jnp.int32)) - 1) % 2
    after = jnp.sum(jnp.where(own, tile_end[None, :], 0), axis=1)
    nxt = jnp.where(after < used, jnp.take(tile_expert, jnp.minimum(after, max_tiles - 1)), -1)
    plan = tuple(a.astype(jnp.int32) for a in (tile_expert, t_eff, tile_rows, first, nxt, slot))
    return pos.astype(jnp.int32), pad, plan, max_tiles


def _scatter_body(pad_start_ref, pad_len_ref, pos_ref, h_ref, xs_ref, zeros, sem, zsem):
    tm = h_ref.shape[0]
    n_exp = pad_start_ref.shape[0] - 1

    @pl.when(pl.program_id(0) == 0)
    def _():
        zeros[...] = jnp.zeros_like(zeros)
        half = zeros.shape[0]
        sizes = [half >> i for i in range(half.bit_length())]
        tail_start = pad_start_ref[n_exp]
        tail_halves = pad_len_ref[n_exp] // half

        def tail_copy(i):
            return pltpu.make_async_copy(zeros, xs_ref.at[pl.ds(tail_start + i * half, half)], zsem)

        lax.fori_loop(0, tail_halves, lambda i, c: (tail_copy(i).start(), c)[1], 0)
        lax.fori_loop(0, tail_halves, lambda i, c: (tail_copy(i).wait(), c)[1], 0)

        def pieces(e, wait):
            start = pad_start_ref[e]
            n = pad_len_ref[e]
            for size in sizes:
                has = (n & size) != 0
                copy = pltpu.make_async_copy(zeros.at[pl.ds(0, size)], xs_ref.at[pl.ds(start, size)], zsem)

                @pl.when(has)
                def _():
                    copy.wait() if wait else copy.start()

                start = start + jnp.where(has, size, 0)

        lax.fori_loop(0, n_exp, lambda e, c: (pieces(e, False), c)[1], 0)
        lax.fori_loop(0, n_exp, lambda e, c: (pieces(e, True), c)[1], 0)

    def issue(r, carry):
        for k in range(TOP_K):
            pltpu.make_async_copy(h_ref.at[r], xs_ref.at[pos_ref[k, r]], sem).start(
                priority=k % 2)
        return carry

    lax.fori_loop(0, tm, issue, 0)
    all_rows = xs_ref.at[pl.ds(0, TOP_K * tm)]
    pltpu.make_async_copy(all_rows, all_rows, sem).wait()


def _scatter(pad_start, pad_len, pos, hp, slots):
    rows = hp.shape[0]
    row_tile = hp.shape[1:]
    tm = SEQ_TILE
    assert EXPERT_TILE & (EXPERT_TILE - 1) == 0
    grid_spec = pltpu.PrefetchScalarGridSpec(
        num_scalar_prefetch=2,
        grid=(rows // tm,),
        in_specs=[
            pl.BlockSpec((TOP_K, tm), lambda i, ps, pn: (0, i), memory_space=pltpu.SMEM),
            pl.BlockSpec((tm,) + row_tile, lambda i, ps, pn: (i, 0, 0)),
        ],
        out_specs=pl.BlockSpec(memory_space=pl.ANY),
        scratch_shapes=[pltpu.VMEM((EXPERT_TILE // 2,) + row_tile, U32),
                        pltpu.SemaphoreType.DMA(()), pltpu.SemaphoreType.DMA(())],
    )
    return pl.pallas_call(
        _scatter_body,
        grid_spec=grid_spec,
        out_shape=jax.ShapeDtypeStruct((slots,) + row_tile, U32),
        compiler_params=_cparams("arbitrary"),
        name="moe_dispatch",
    )(pad_start, pad_len, pos, hp)


def _ffn_rows(xw, wgu_ref, wd_ref):
    x = jnp.concatenate([_unpack_lo(xw).astype(BF16), _unpack_hi(xw).astype(BF16)], axis=1)
    gu = jnp.dot(x, wgu_ref, preferred_element_type=F32)
    f = gu.shape[1] // 2
    hdn = (_silu(gu[:, :f]) * gu[:, f:]).astype(BF16)
    return jnp.dot(hdn, wd_ref, preferred_element_type=F32)


def _experts_body(te_ref, ts_ref, tr_ref, first_ref, next_ref, slot_ref,
                  xs_ref, wg_hbm, wu_hbm, wd_hbm, ys_ref,
                  wg_buf, wu_buf, wd_buf, wgu_bf, wd_bf, sems, *, layer):
    del ts_ref
    t = pl.program_id(0)
    n_real = tr_ref[t]
    tm = xs_ref.shape[0]
    half = tm // 2
    f = wg_buf.shape[-1]

    def weight_copies(e, s):
        return [pltpu.make_async_copy(src.at[layer, e], dst.at[s], sems.at[s])
                for src, dst in ((wg_hbm, wg_buf), (wu_hbm, wu_buf), (wd_hbm, wd_buf))]

    @pl.when(t == 0)
    def _():
        for c in weight_copies(te_ref[0], slot_ref[0]):
            c.start()

    @pl.when(first_ref[t] == 1)
    def _():
        s = slot_ref[t]
        for c in weight_copies(te_ref[t], s):
            c.wait()
        for slot in range(2):
            @pl.when(s == slot)
            def _(slot=slot):
                wgu_bf[:, :f] = wg_buf[slot].astype(BF16)
                wgu_bf[:, f:] = wu_buf[slot].astype(BF16)
                wd_bf[...] = wd_buf[slot].astype(BF16)

        @pl.when(next_ref[t] >= 0)
        def _():
            for c in weight_copies(next_ref[t], 1 - s):
                c.start()

    def ffn(rows):
        part = pl.ds(0, rows)
        y = _ffn_rows(_load_row_tiles(xs_ref.at[part]), wgu_bf[...], wd_bf[...])
        _store_row_tiles(ys_ref.at[part], _pack_bf16_pairs(y))

    @pl.when(n_real > half)
    def _():
        ffn(tm)

    @pl.when(jnp.logical_and(n_real > 0, n_real <= half))
    def _():
        ffn(half)
        ys_ref[pl.ds(half, half)] = jnp.zeros((half,) + ys_ref.shape[1:], ys_ref.dtype)

    @pl.when(n_real == 0)
    def _():
        ys_ref[...] = jnp.zeros_like(ys_ref)


def _experts(tile_plan, xs, w_gate, w_up, w_down, layer, max_tiles):
    slots = xs.shape[0]
    row_tile = xs.shape[1:]
    tm = EXPERT_TILE
    d, f = w_gate.shape[2], w_gate.shape[3]
    grid_spec = pltpu.PrefetchScalarGridSpec(
        num_scalar_prefetch=len(tile_plan),
        grid=(max_tiles,),
        in_specs=[
            pl.BlockSpec((tm,) + row_tile, lambda t, te, ts, *_: (ts[t], 0, 0)),
            pl.BlockSpec(memory_space=pl.ANY),
            pl.BlockSpec(memory_space=pl.ANY),
            pl.BlockSpec(memory_space=pl.ANY),
        ],
        out_specs=pl.BlockSpec((tm,) + row_tile, lambda t, *_: (t, 0, 0)),
        scratch_shapes=[pltpu.VMEM((2, d, f), F32), pltpu.VMEM((2, d, f), F32), pltpu.VMEM((2, f, d), F32),
                        pltpu.VMEM((d, 2 * f), BF16), pltpu.VMEM((f, d), BF16),
                        pltpu.SemaphoreType.DMA((2,))],
    )
    return pl.pallas_call(
        functools.partial(_experts_body, layer=layer),
        grid_spec=grid_spec,
        out_shape=jax.ShapeDtypeStruct((slots,) + row_tile, U32),
        compiler_params=_cparams("arbitrary"),
        name="moe_experts",
    )(*tile_plan, xs, w_gate, w_up, w_down)


def _combine_body(pos_ref, pos_next_ref, wt_ref, hp_ref, ys_ref, wgu_ref, wd_ref, x_ref, g_ref, gt_ref,
                  o_ref, gbuf_even, gbuf_odd, sems):
    tm = hp_ref.shape[0]
    i = pl.program_id(0)
    bufs = (gbuf_even, gbuf_odd)

    def row_copy(p_ref, slot, k, r):
        return pltpu.make_async_copy(ys_ref.at[p_ref[k, r]], bufs[slot].at[k, r], sems.at[slot])

    def wait_rows(slot):
        all_rows = ys_ref.at[pl.ds(0, TOP_K * tm)]
        pltpu.make_async_copy(all_rows, all_rows, sems.at[slot]).wait()

    @pl.when(i == 0)
    def _():
        def issue(r, carry):
            for k in range(TOP_K):
                row_copy(pos_ref, 0, k, r).start(priority=k % 2)
            return carry
        lax.fori_loop(0, tm, issue, 0)

    def step(cur):
        wait_rows(cur)
        for r in range(tm):
            for k in range(TOP_K):
                row_copy(pos_next_ref, 1 - cur, k, r).start(priority=k % 2)
        y = _ffn_rows(_load_row_tiles(hp_ref), wgu_ref[...], wd_ref[...])
        half = y.shape[1] // 2
        lo, hi = y[:, :half], y[:, half:]
        wt = wt_ref[...]
        for k in range(TOP_K):
            wk = wt[:, k:k + 1]
            gk = _load_row_tiles(bufs[cur].at[k])
            lo = lo + wk * _unpack_lo(gk)
            hi = hi + wk * _unpack_hi(gk)
        y = jnp.concatenate([lo, hi], axis=1)
        o_ref[...] = x_ref[...] + gt_ref[0] * _rms(y, g_ref[...])

        @pl.when(i == pl.num_programs(0) - 1)
        def _():
            wait_rows(1 - cur)

    for parity in range(2):
        pl.when(i % 2 == parity)(functools.partial(step, parity))


def _combine(pos, wts, hp, ys, w_sgu, w_sd, xa, g, mod3, gt_idx, rows, seq, n_batch, in_place):
    d = xa.shape[1]
    row_tile = hp.shape[1:]
    tm = SEQ_TILE
    tps = seq // tm
    grp = lambda i: _group_of_tile(i, tps, n_batch)
    last = rows // tm - 1
    return pl.pallas_call(
        _combine_body,
        grid=(rows // tm,),
        in_specs=[
            pl.BlockSpec((TOP_K, tm), lambda i: (0, i), memory_space=pltpu.SMEM),
            pl.BlockSpec((TOP_K, tm), lambda i: (0, jnp.minimum(i + 1, last)), memory_space=pltpu.SMEM),
            pl.BlockSpec((tm, TOP_K), lambda i: (i, 0)),
            pl.BlockSpec((tm,) + row_tile, lambda i: (i, 0, 0)),
            pl.BlockSpec(memory_space=pl.ANY),
            pl.BlockSpec(w_sgu.shape, lambda i: (0, 0), pipeline_mode=pl.Buffered(1)),
            pl.BlockSpec(w_sd.shape, lambda i: (0, 0), pipeline_mode=pl.Buffered(1)),
            pl.BlockSpec((tm, d), lambda i: (i, 0)),
            pl.BlockSpec((1, d), lambda i: (0, 0)),
            pl.BlockSpec((1, 1, d), lambda i: (grp(i), 0, gt_idx)),
        ],
        out_specs=pl.BlockSpec((tm, d), lambda i: (i, 0)),
        out_shape=jax.ShapeDtypeStruct(xa.shape if in_place else (rows, d), F32),
        scratch_shapes=[pltpu.VMEM((TOP_K, tm) + row_tile, U32), pltpu.VMEM((TOP_K, tm) + row_tile, U32),
                        pltpu.SemaphoreType.DMA((2,))],
        input_output_aliases={7: 0} if in_place else {},
        compiler_params=_cparams("arbitrary"),
        name="moe_combine",
    )(pos, pos, wts, hp, ys, w_sgu, w_sd, xa, g, mod3)


def kernel(x, c, ctx, c_ctx, w_ada, b_ada, norm_g, w_in, diff_lambda, g_subln, sg_ln_g, sg_ln_b, sg_w, sg_b, conv_w, w_branch, w_out, w_router, router_bias, w_exp_gate, w_exp_up, w_exp_down, w_sh_gate, w_sh_up, w_sh_down):
    n_batch, seq, d = x.shape
    ctx_len = ctx.shape[1]
    depth = w_ada.shape[0]
    mix_w = w_branch.shape[2]
    n_heads = mix_w // HEAD_W
    n_exp = w_router.shape[2]
    n_lat = n_batch * seq
    n_ctx = n_batch * ctx_len
    n_all = n_lat + n_ctx
    assert n_batch + 1 <= MOD_ROWS and seq % SEQ_TILE == 0 and ctx_len % SEQ_TILE == 0
    assert n_lat % seq == 0 and n_lat % ctx_len == 0 and mix_w // SG_GROUPS == LANES
    assert d // 2 == SUBLANES * LANES

    xa = (x.reshape(n_lat, d), ctx.reshape(n_ctx, d))
    if depth == 1:
        xa = _concat_rows(*xa)
    cond = jnp.zeros((MOD_ROWS, d), F32).at[:n_batch].set(c).at[n_batch].set(c_ctx)
    mod = _ada(cond, w_ada, b_ada)
    cos, sin = _rope_tables(seq)
    q_cb, k_cb, v_cb = 0, n_heads, 2 * n_heads
    gate_cb = (8 * mix_w) // d

    w_in_b = w_in.astype(BF16)
    w_branch_b = w_branch.astype(BF16)
    w_out_b = w_out.astype(BF16)
    n_in = w_in.shape[2]

    for l in range(depth):
        last = l == depth - 1
        lam_init = 0.8 - 0.6 * math.exp(-0.3 * l)
        mod3 = mod[l].reshape(MOD_ROWS, 1, N_MOD * d)
        ng = norm_g[l]
        rows = n_lat if last else n_all

        in_proj = functools.partial(_norm_matmul, xa, ng[0:1], mod3, 1, 0, w_in_b, l,
                                    seq=seq, n_batch=n_batch)
        if last:
            y = in_proj(n_lat, 0, n_in, 0, name="in_proj")
            yc = in_proj(n_ctx, n_lat, 2 * mix_w, mix_w, name="in_proj_ctx_kv")
            ctx_kv = (yc, 0, yc, n_heads, 0, ctx_len)
        else:
            y = in_proj(n_all, 0, n_in, 0, name="in_proj")
            ctx_kv = (y, k_cb, y, v_cb, n_lat // ctx_len, ctx_len)
        attn_o = _attention(y, q_cb, 0, y, k_cb, y, v_cb, 0, seq, seq, n_batch, n_heads,
                            diff_lambda[l], g_subln[l:l + 1], lam_init, ctx=ctx_kv, rope=(cos, sin),
                            name="attn_latent")
        if not last:
            attn_c = _attention(y, q_cb, n_lat // ctx_len, y, k_cb, y, v_cb, n_lat // ctx_len,
                                ctx_len, ctx_len, n_batch, n_heads,
                                diff_lambda[l], g_subln[l:l + 1], lam_init, name="attn_ctx")
            attn_o = jnp.concatenate([attn_o, attn_c], axis=0)
        sg_bias_map = jnp.repeat(sg_b[l].T, mix_w // SG_GROUPS, axis=1)
        xa = _mixer_tail(y, attn_o, xa, rows, n_lat, seq, n_batch, 3, gate_cb, l,
                         sg_ln_g[l:l + 1], sg_ln_b[l:l + 1], sg_w[l].astype(BF16), sg_bias_map, conv_w[l],
                         w_branch_b, w_out_b, ng[1:2], mod3, 2)

        hp, wt_t, idx_t, rank_t, cnt = _route(xa, ng[2:3], mod3, 4, 3, w_router[l].T, router_bias[l],
                                              rows, seq, n_batch)
        pos, pad, tile_plan, max_tiles = _dispatch_plan(
            idx_t, rank_t, cnt[:, 0].astype(jnp.int32), EXPERT_TILE)
        xs = _scatter(*pad, pos, hp, max_tiles * EXPERT_TILE)
        ys = _experts(tile_plan, xs, w_exp_gate, w_exp_up, w_exp_down, l, max_tiles)
        w_sgu = jnp.concatenate([w_sh_gate[l], w_sh_up[l]], axis=-1).astype(BF16)
        xa = _combine(pos, wt_t.T, hp, ys, w_sgu, w_sh_down[l].astype(BF16), xa, ng[3:4], mod3, 5,
                      rows, seq, n_batch, in_place=not last)

    return xa.reshape(n_batch, seq, d)
```

```python
import functools
import math

import jax
import jax.numpy as jnp
from jax import lax
from jax.experimental import pallas as pl
from jax.experimental.pallas import tpu as pltpu

GRID_W = 64
QK_DIM = 64
HEAD_W = 2 * QK_DIM
ROPE_BASE = 10000.0
SG_CHUNK = 128
SG_GROUPS = 8
TOP_K = 8
N_GROUPS = 8
TOPK_GROUPS = 4
ROUTED_SCALE = 2.5
NORM_EPS = 1e-6
N_MOD = 6

LANES = 128
SUBLANES = 8
VMEM_LIMIT = 56 * 1024 * 1024
SEQ_TILE = 256
EXPERT_TILE = 512
MOD_ROWS = 16
ATTN_SUB_ROWS = 128

F32 = jnp.float32
BF16 = jnp.bfloat16
U32 = jnp.uint32
HIGHEST = lax.Precision.HIGHEST


def _cparams(*sem):
    return pltpu.CompilerParams(dimension_semantics=sem, vmem_limit_bytes=VMEM_LIMIT)


def _pick_tile(n, cands):
    for c in cands:
        if n % c == 0:
            return c
    raise ValueError(f"no tile for {n} in {cands}")


def _sigmoid(x):
    return 1.0 / (1.0 + jnp.exp(-x))


def _silu(x):
    return x * _sigmoid(x)


def _gelu_tanh(x):
    c = math.sqrt(2.0 / math.pi)
    return x * (0.5 * (1.0 + jnp.tanh(c * (x + 0.044715 * (x * x * x)))))


def _rms(x, g):
    return x * lax.rsqrt(jnp.mean(x * x, axis=-1, keepdims=True) + NORM_EPS) * g


def _pack_bf16_pairs(x):
    w = x.shape[1] // 2
    bits = pltpu.bitcast(x.astype(BF16).astype(F32), U32)
    return (bits[:, :w] >> 16) | bits[:, w:]


def _unpack_lo(w):
    return pltpu.bitcast(w << 16, F32)


def _unpack_hi(w):
    return pltpu.bitcast(w & jnp.uint32(0xFFFF0000), F32)


def _store_row_tiles(ref, packed):
    m, s, _ = ref.shape
    flat = ref.reshape(m * s, LANES)
    for j in range(s):
        flat[pl.ds(j, m, stride=s), :] = packed[:, j * LANES:(j + 1) * LANES]


def _load_row_tiles(ref):
    m, s, _ = ref.shape
    flat = ref.reshape(m * s, LANES)
    return jnp.concatenate([flat[pl.ds(j, m, stride=s), :] for j in range(s)], axis=1)


def _concat_body(a_ref, b_ref, o_ref, *, a_tiles):
    i = pl.program_id(0)

    @pl.when(i < a_tiles)
    def _():
        o_ref[...] = a_ref[...]

    @pl.when(i >= a_tiles)
    def _():
        o_ref[...] = b_ref[...]


def _concat_rows(a, b):
    d = a.shape[1]
    tm = _pick_tile(math.gcd(a.shape[0], b.shape[0]), (512, 256))
    na, nb = a.shape[0] // tm, b.shape[0] // tm
    return pl.pallas_call(
        functools.partial(_concat_body, a_tiles=na),
        grid=(na + nb,),
        in_specs=[
            pl.BlockSpec((tm, d), lambda i: (jnp.minimum(i, na - 1), 0)),
            pl.BlockSpec((tm, d), lambda i: (jnp.maximum(i - na, 0), 0)),
        ],
        out_specs=pl.BlockSpec((tm, d), lambda i: (i, 0)),
        out_shape=jax.ShapeDtypeStruct((a.shape[0] + b.shape[0], d), a.dtype),
        compiler_params=_cparams("parallel"),
        name="concat_rows",
    )(a, b)


def _ada_body(c_ref, w_ref, b_ref, o_ref):
    a = _silu(c_ref[...])
    o_ref[0] = jnp.dot(a, w_ref[0], precision=HIGHEST, preferred_element_type=F32) + b_ref[0]


def _ada(cond, w_ada, b_ada):
    depth, d, n = w_ada.shape
    tn = _pick_tile(n, (1024, 512, 256, 128))
    return pl.pallas_call(
        _ada_body,
        grid=(depth, n // tn),
        in_specs=[
            pl.BlockSpec((MOD_ROWS, d), lambda l, j: (0, 0)),
            pl.BlockSpec((1, d, tn), lambda l, j: (l, 0, j)),
            pl.BlockSpec((1, 1, tn), lambda l, j: (l, 0, j)),
        ],
        out_specs=pl.BlockSpec((1, MOD_ROWS, tn), lambda l, j: (l, 0, j)),
        out_shape=jax.ShapeDtypeStruct((depth, MOD_ROWS, n), F32),
        compiler_params=_cparams("arbitrary", "arbitrary"),
        name="ada_mod",
    )(cond, w_ada, b_ada.reshape(depth, 1, n))


def _group_of_tile(i, tiles_per_seq, n_batch):
    return jnp.minimum(i // tiles_per_seq, n_batch)


def _stream_specs(stream, tm, tile_of):
    if not isinstance(stream, tuple):
        return [pl.BlockSpec((tm, stream.shape[1]), lambda *idx: (tile_of(*idx), 0))], None
    a, b = stream
    split = a.shape[0] // tm
    assert a.shape[0] % tm == 0 and b.shape[0] % tm == 0
    return [pl.BlockSpec((tm, a.shape[1]), lambda *idx: (jnp.minimum(tile_of(*idx), split - 1), 0)),
            pl.BlockSpec((tm, b.shape[1]), lambda *idx: (jnp.maximum(tile_of(*idx) - split, 0), 0),
                         pipeline_mode=pl.Buffered(1))], split


def _stream_tile(refs, tile, split):
    if split is None:
        return refs[0][...]
    return jnp.where(tile < split, refs[0][...], refs[1][...])


def _stream_args(stream):
    return list(stream) if isinstance(stream, tuple) else [stream]


def _norm_mm_body(*refs, split, tile0):
    n_src = 1 if split is None else 2
    g_ref, sc_ref, sh_ref, w_ref, o_ref, h_ref = refs[n_src:]

    @pl.when(pl.program_id(1) == 0)
    def _():
        y = _rms(_stream_tile(refs[:n_src], pl.program_id(0) + tile0, split), g_ref[...])
        h_ref[...] = (y * (1.0 + sc_ref[0]) + sh_ref[0]).astype(h_ref.dtype)

    o_ref[...] = jnp.dot(h_ref[...], w_ref[...], preferred_element_type=F32).astype(o_ref.dtype)


def _norm_matmul(stream, g, mod3, sc_idx, sh_idx, w, layer, rows, row_off, cols, col_off, seq, n_batch, name):
    d = w.shape[1]
    tm = _pick_tile(math.gcd(math.gcd(rows, row_off) if row_off else rows, seq), (1024, 512, 256))
    tn_cands = (1024, 512, 256) if isinstance(stream, tuple) else (2048, 1024, 512, 256)
    tn = _pick_tile(math.gcd(cols, col_off) if col_off else cols, tn_cands)
    ro, co = row_off // tm, col_off // tn
    tps = seq // tm
    grp = lambda i: _group_of_tile(i + ro, tps, n_batch)
    x_specs, split = _stream_specs(stream, tm, lambda i, j: i + ro)
    return pl.pallas_call(
        functools.partial(_norm_mm_body, split=split, tile0=ro),
        grid=(rows // tm, cols // tn),
        in_specs=x_specs + [
            pl.BlockSpec((1, d), lambda i, j: (0, 0)),
            pl.BlockSpec((1, 1, d), lambda i, j: (grp(i), 0, sc_idx)),
            pl.BlockSpec((1, 1, d), lambda i, j: (grp(i), 0, sh_idx)),
            pl.BlockSpec((None, d, tn), lambda i, j: (layer, 0, j + co)),
        ],
        out_specs=pl.BlockSpec((tm, tn), lambda i, j: (i, j)),
        out_shape=jax.ShapeDtypeStruct((rows, cols), BF16),
        scratch_shapes=[pltpu.VMEM((tm, d), BF16)],
        compiler_params=_cparams("arbitrary", "arbitrary"),
        name=name,
    )(*_stream_args(stream), g, mod3, mod3, w)


def _rope_apply(x, cos, sin):
    lane = lax.broadcasted_iota(jnp.int32, x.shape, 1)
    first_half = (lane % (QK_DIM // 2)) < (QK_DIM // 4)
    partner = jnp.where(first_half,
                        pltpu.roll(x, LANES - QK_DIM // 4, 1),
                        pltpu.roll(x, QK_DIM // 4, 1))
    return x * cos + partner * sin


def _rope_tables(seq):
    n_freq = QK_DIM // 4
    inv = ROPE_BASE ** (-jnp.arange(n_freq, dtype=F32) / n_freq)
    t = jnp.arange(seq)
    row = (t // GRID_W).astype(F32)
    col = (t % GRID_W).astype(F32)
    ang = jnp.stack([row[:, None] * inv, col[:, None] * inv], axis=1)
    cos = jnp.cos(ang)
    sin = jnp.sin(ang)
    cos64 = jnp.stack([cos, cos], axis=2).reshape(seq, QK_DIM)
    sin64 = jnp.stack([-sin, sin], axis=2).reshape(seq, QK_DIM)
    reps = LANES // QK_DIM
    return jnp.tile(cos64, (1, reps)), jnp.tile(sin64, (1, reps))


def _attn_body(*refs, two_seg, rope, lam_init):
    refs = list(refs)
    q_ref, k_ref, v_ref = refs[:3]
    del refs[:3]
    segs = []
    if two_seg:
        segs.append((refs[0], refs[1], False))
        del refs[:2]
    segs.append((k_ref, v_ref, rope))
    if rope:
        cos_ref, sin_ref = refs[:2]
        del refs[:2]
    dl_ref, g_ref, o_ref, kt_ref, v1_ref = refs
    tq = q_ref.shape[0]

    @pl.when(pl.program_id(2) == 0)
    def _():
        row = 0
        for ks_ref, vs_ref, rotate in segs:
            n = ks_ref.shape[0]
            ks = ks_ref[...].astype(F32)
            if rotate:
                ks = _rope_apply(ks, cos_ref[...], sin_ref[...])
            kt_ref[:, row:row + n] = ks.T.astype(BF16)
            v1_ref[row:row + n, :HEAD_W] = vs_ref[...]
            v1_ref[row:row + n, HEAD_W:] = jnp.ones(vs_ref.shape, BF16)
            row += n

    dl = dl_ref[...]
    lam = (jnp.exp(jnp.sum(dl[0:1] * dl[1:2], axis=-1, keepdims=True))
           - jnp.exp(jnp.sum(dl[2:3] * dl[3:4], axis=-1, keepdims=True)) + lam_init)
    q = q_ref[...]
    if rope:
        q_rows = pl.ds(pl.multiple_of(pl.program_id(2) * tq, tq), tq)
        q = _rope_apply(q.astype(F32), cos_ref[q_rows, :], sin_ref[q_rows, :]).astype(BF16)
    q = q * jnp.asarray(QK_DIM ** -0.5, BF16)
    lane = lax.broadcasted_iota(jnp.int32, q.shape, 1)
    zero = jnp.zeros_like(q)

    def softmax_v(qm):
        s = jnp.dot(qm, kt_ref[...], preferred_element_type=F32)
        m = jnp.max(s, axis=-1, keepdims=True)
        acc = jnp.dot(jnp.exp(s - m).astype(BF16), v1_ref[...], preferred_element_type=F32)
        return acc[:, :HEAD_W] / acc[:, HEAD_W:]

    q1 = jnp.where(lane < QK_DIM, q, zero)
    q2 = jnp.where(lane >= QK_DIM, q, zero)
    sub = min(q.shape[0], ATTN_SUB_ROWS)
    for r0 in range(0, q.shape[0], sub):
        rs = slice(r0, r0 + sub)
        o = softmax_v(q1[rs]) - lam * softmax_v(q2[rs])
        o_ref[rs, :] = (_rms(o, g_ref[...]) * (1.0 - lam_init)).astype(o_ref.dtype)


def _attention(q_arr, q_cb, q_rb, k_arr, k_cb, v_arr, v_cb, kv_rb, seq_q, seq_k, n_batch, n_heads,
               dl, g_sub, lam_init, ctx=None, rope=None, name="attn"):
    tq = _pick_tile(seq_q, (2048, 1024, 512, 256, 128))
    nq = seq_q // tq
    n_keys = seq_k + (ctx[5] if ctx is not None else 0)
    scratch = [pltpu.VMEM((HEAD_W, n_keys), BF16), pltpu.VMEM((n_keys, 2 * HEAD_W), BF16)]
    in_specs = [
        pl.BlockSpec((tq, HEAD_W), lambda b, h, i: ((q_rb + b) * nq + i, q_cb + h)),
        pl.BlockSpec((seq_k, HEAD_W), lambda b, h, i: (kv_rb + b, k_cb + h)),
        pl.BlockSpec((seq_k, HEAD_W), lambda b, h, i: (kv_rb + b, v_cb + h)),
    ]
    args = [q_arr, k_arr, v_arr]
    if ctx is not None:
        kc_arr, kc_cb, vc_arr, vc_cb, c_rb, seq_c = ctx
        in_specs += [
            pl.BlockSpec((seq_c, HEAD_W), lambda b, h, i: (c_rb + b, kc_cb + h)),
            pl.BlockSpec((seq_c, HEAD_W), lambda b, h, i: (c_rb + b, vc_cb + h)),
        ]
        args += [kc_arr, vc_arr]
    if rope is not None:
        assert seq_q == seq_k
        in_specs += [pl.BlockSpec((seq_k, LANES), lambda b, h, i: (0, 0))] * 2
        args += list(rope)
    in_specs += [
        pl.BlockSpec(dl.shape, lambda b, h, i: (0, 0)),
        pl.BlockSpec((1, HEAD_W), lambda b, h, i: (0, 0)),
    ]
    args += [dl, g_sub]
    return pl.pallas_call(
        functools.partial(_attn_body, two_seg=ctx is not None, rope=rope is not None, lam_init=lam_init),
        grid=(n_batch, n_heads, nq),
        in_specs=in_specs,
        out_specs=pl.BlockSpec((tq, HEAD_W), lambda b, h, i: (b * nq + i, h)),
        out_shape=jax.ShapeDtypeStruct((n_batch * seq_q, n_heads * HEAD_W), BF16),
        scratch_shapes=scratch,
        compiler_params=_cparams("arbitrary", "arbitrary", "arbitrary"),
        name=name,
    )(*args)


def _mixer_tail_body(u_ref, v_ref, x_ref, b_ref, c_ref, xp_ref, cp_ref, xn_ref, cn_ref, a_ref,
                     g0_ref, g1_ref, g2_ref, lng_ref, lnb_ref, sgw_ref, sgb_ref, cw_ref, wb_ref, wo_ref,
                     ng_ref, gt_ref, *refs, lat_tiles, tiles_per_seq, split):
    res_refs, o_ref, sg_ref = refs[:-2], refs[-2], refs[-1]
    i = pl.program_id(0)
    tm, w = u_ref.shape

    def gated(branch, gate_ref, g):
        return _sigmoid(gate_ref[...].astype(F32)) * jnp.dot(branch, wb_ref[g], preferred_element_type=F32)

    acc = gated(a_ref[...], g0_ref, 0)
    u = _gelu_tanh(u_ref[...].astype(F32))
    v = _gelu_tanh(v_ref[...].astype(F32))
    mu = jnp.mean(v, axis=-1, keepdims=True)
    dv = v - mu
    v = dv * lax.rsqrt(jnp.mean(dv * dv, axis=-1, keepdims=True) + NORM_EPS)
    v = (v * lng_ref[...] + lnb_ref[...]).astype(BF16)
    gw = w // SG_GROUPS
    for c in range(tm // SG_CHUNK):
        rs = slice(c * SG_CHUNK, (c + 1) * SG_CHUNK)
        for g in range(SG_GROUPS):
            cs = slice(g * gw, (g + 1) * gw)
            mixed = jnp.dot(sgw_ref[g], v[rs, cs], preferred_element_type=F32) + sgb_ref[:, cs]
            sg_ref[rs, cs] = (u[rs, cs] * mixed).astype(sg_ref.dtype)
    is_lat = i < lat_tiles
    seq_first = jnp.logical_or(jnp.logical_not(is_lat), i % tiles_per_seq == 0)
    seq_last = jnp.logical_or(jnp.logical_not(is_lat), i % tiles_per_seq == tiles_per_seq - 1)
    z = c_ref[...].astype(F32) * x_ref[...].astype(F32)
    hp = xp_ref.shape[0]
    z_prev = (cp_ref[...].astype(F32) * xp_ref[...].astype(F32))[hp - 1:hp, :]
    z_next = (cn_ref[...].astype(F32) * xn_ref[...].astype(F32))[0:1, :]
    z_prev = jnp.where(seq_first, 0.0, z_prev)
    z_next = jnp.where(seq_last, 0.0, z_next)
    row = lax.broadcasted_iota(jnp.int32, z.shape, 0)
    z_dn = jnp.where(row == 0, z_prev, pltpu.roll(z, 1, 0))
    z_up = jnp.where(row == tm - 1, z_next, pltpu.roll(z, tm - 1, 0))
    y = cw_ref[0:1, :] * z_dn + cw_ref[1:2, :] * z + cw_ref[2:3, :] * z_up
    cv = (b_ref[...].astype(F32) * y).astype(BF16)
    acc = acc + gated(sg_ref[...], g1_ref, 1) + gated(cv, g2_ref, 2)
    mix = jnp.dot(acc.astype(BF16), wo_ref[...], preferred_element_type=F32)
    o_ref[...] = _stream_tile(res_refs, i, split) + gt_ref[0] * _rms(mix, ng_ref[...])


def _mixer_tail(y, attn_o, stream, rows, lat_rows, seq, n_batch, col0, gate_cb, layer,
                sg_ln_g, sg_ln_b, sg_w, sg_bias_map, conv_w, w_branch, w_out, ng, mod3, gt_idx):
    n_br, mix_w, d = w_branch.shape[1:]
    tm = SEQ_TILE
    halo = 16
    hb = tm // halo
    last_hblk = rows // halo - 1
    cur = lambda c: pl.BlockSpec((tm, mix_w), lambda i: (i, col0 + c))
    prev = lambda c: pl.BlockSpec((halo, mix_w), lambda i: (jnp.maximum(i * hb - 1, 0), col0 + c))
    nxt = lambda c: pl.BlockSpec((halo, mix_w), lambda i: (jnp.minimum((i + 1) * hb, last_hblk), col0 + c))
    gate = lambda g: pl.BlockSpec((tm, d), lambda i: (i, gate_cb + g))
    full = lambda a: pl.BlockSpec(a.shape, lambda i: (0,) * a.ndim)
    consts = [sg_ln_g, sg_ln_b, sg_w, sg_bias_map, conv_w]
    grp = lambda i: _group_of_tile(i, seq // tm, n_batch)
    res_specs, split = _stream_specs(stream, tm, lambda i: i)
    in_specs = ([cur(0), cur(1), cur(2), cur(3), cur(4), prev(2), prev(4), nxt(2), nxt(4),
                 pl.BlockSpec((tm, mix_w), lambda i: (i, 0)), gate(0), gate(1), gate(2)]
                + [full(a) for a in consts]
                + [pl.BlockSpec((None, n_br, mix_w, d), lambda i: (layer, 0, 0, 0), pipeline_mode=pl.Buffered(1)),
                   pl.BlockSpec((None, d, d), lambda i: (layer, 0, 0), pipeline_mode=pl.Buffered(1)),
                   pl.BlockSpec((1, d), lambda i: (0, 0)),
                   pl.BlockSpec((1, 1, d), lambda i: (grp(i), 0, gt_idx))])
    if split is None:
        out_rows, alias = stream.shape[0], {len(in_specs): 0}
    else:
        out_rows, alias = rows, {}
        assert rows == stream[0].shape[0] + stream[1].shape[0]
    return pl.pallas_call(
        functools.partial(_mixer_tail_body, lat_tiles=lat_rows // tm, tiles_per_seq=seq // tm, split=split),
        grid=(rows // tm,),
        in_specs=in_specs + res_specs,
        out_specs=pl.BlockSpec((tm, d), lambda i: (i, 0)),
        out_shape=jax.ShapeDtypeStruct((out_rows, d), F32),
        scratch_shapes=[pltpu.VMEM((tm, mix_w), BF16)],
        input_output_aliases=alias,
        compiler_params=_cparams("parallel"),
        name="mixer_tail",
    )(y, y, y, y, y, y, y, y, y, attn_o, y, y, y, *consts, w_branch, w_out, ng, mod3, *_stream_args(stream))


def _top1_mask(vals, idx, n, axis):
    m = jnp.max(vals, axis=axis, keepdims=True)
    first = jnp.min(jnp.where(vals == m, idx, n), axis=axis, keepdims=True)
    return idx == first, m, first


def _route_body(x_ref, g_ref, sc_ref, sh_ref, wr_ref, rb_ref, hp_ref, wt_ref, ix_ref, rk_ref, cnt_ref):
    h2 = _rms(x_ref[...], g_ref[...]) * (1.0 + sc_ref[0]) + sh_ref[0]
    _store_row_tiles(hp_ref, _pack_bf16_pairs(h2))
    n_exp = wr_ref.shape[0]
    tm = x_ref.shape[0]
    per_group = n_exp // N_GROUPS
    logits = lax.dot_general(wr_ref[...], h2, (((1,), (1,)), ((), ())),
                             precision=HIGHEST, preferred_element_type=F32)
    scores = _sigmoid(logits)
    choice = scores + rb_ref[...]
    neg = -jnp.inf
    ji = lax.broadcasted_iota(jnp.int32, (per_group, tm), 0)
    groups, gs = [], []
    for g in range(N_GROUPS):
        cg = choice[g * per_group:(g + 1) * per_group, :]
        hit, m1, _ = _top1_mask(cg, ji, per_group, 0)
        m2 = jnp.max(jnp.where(hit, neg, cg), axis=0, keepdims=True)
        groups.append(cg)
        gs.append(m1 + m2)
    kept = []
    for g in range(N_GROUPS):
        beaten_by = jnp.zeros((1, tm), jnp.int32)
        for o in range(N_GROUPS):
            if o != g:
                beats = (gs[o] >= gs[g]) if o < g else (gs[o] > gs[g])
                beaten_by = beaten_by + beats.astype(jnp.int32)
        kept.append(jnp.where(beaten_by < TOPK_GROUPS, groups[g], neg))
    cm = jnp.concatenate(kept, axis=0)
    ei = lax.broadcasted_iota(jnp.int32, cm.shape, 0)
    ws, ids, hits = [], [], []
    sel = jnp.zeros(cm.shape, F32)
    for _ in range(TOP_K):
        hit, _, first = _top1_mask(cm, ei, n_exp, 0)
        ws.append(jnp.sum(jnp.where(hit, scores, 0.0), axis=0, keepdims=True))
        ids.append(first)
        hits.append(hit)
        sel = sel + jnp.where(hit, 1.0, 0.0)
        cm = jnp.where(hit, neg, cm)
    w = jnp.concatenate(ws, axis=0)
    wt_ref[...] = w / jnp.sum(w, axis=0, keepdims=True) * ROUTED_SCALE
    ix_ref[...] = jnp.concatenate(ids, axis=0)

    @pl.when(pl.program_id(0) == 0)
    def _():
        cnt_ref[...] = jnp.zeros_like(cnt_ref)

    before = (lax.broadcasted_iota(jnp.int32, (tm, tm), 0)
              < lax.broadcasted_iota(jnp.int32, (tm, tm), 1))
    within = jnp.dot(sel.astype(BF16), jnp.where(before, 1.0, 0.0).astype(BF16),
                     preferred_element_type=F32)
    rank = within + cnt_ref[:, 0:1]
    rk_ref[...] = jnp.concatenate(
        [jnp.sum(jnp.where(h, rank, 0.0), axis=0, keepdims=True) for h in hits], axis=0).astype(jnp.int32)
    cnt_ref[...] = cnt_ref[...] + jnp.sum(sel, axis=1, keepdims=True)


def _route(xa, g, mod3, sc_idx, sh_idx, w_router_t, router_bias, rows, seq, n_batch):
    d = xa.shape[1]
    n_exp = w_router_t.shape[0]
    tm = SEQ_TILE
    tps = seq // tm
    grp = lambda i: _group_of_tile(i, tps, n_batch)
    return pl.pallas_call(
        _route_body,
        grid=(rows // tm,),
        in_specs=[
            pl.BlockSpec((tm, d), lambda i: (i, 0)),
            pl.BlockSpec((1, d), lambda i: (0, 0)),
            pl.BlockSpec((1, 1, d), lambda i: (grp(i), 0, sc_idx)),
            pl.BlockSpec((1, 1, d), lambda i: (grp(i), 0, sh_idx)),
            pl.BlockSpec((n_exp, d), lambda i: (0, 0)),
            pl.BlockSpec((n_exp, 1), lambda i: (0, 0)),
        ],
        out_specs=[
            pl.BlockSpec((tm, SUBLANES, LANES), lambda i: (i, 0, 0)),
            pl.BlockSpec((TOP_K, tm), lambda i: (0, i)),
            pl.BlockSpec((TOP_K, tm), lambda i: (0, i)),
            pl.BlockSpec((TOP_K, tm), lambda i: (0, i)),
            pl.BlockSpec((n_exp, LANES), lambda i: (0, 0)),
        ],
        out_shape=[
            jax.ShapeDtypeStruct((rows, SUBLANES, LANES), U32),
            jax.ShapeDtypeStruct((TOP_K, rows), F32),
            jax.ShapeDtypeStruct((TOP_K, rows), jnp.int32),
            jax.ShapeDtypeStruct((TOP_K, rows), jnp.int32),
            jax.ShapeDtypeStruct((n_exp, LANES), F32),
        ],
        compiler_params=_cparams("arbitrary"),
        name="moe_norm_route",
    )(xa, g, mod3, mod3, w_router_t, router_bias.reshape(n_exp, 1))


def _dispatch_plan(idx_t, rank_t, counts, tile):
    n_exp = counts.shape[0]
    rows = idx_t.shape[1]
    tiles_e = (counts + tile - 1) // tile
    tile_end = jnp.cumsum(tiles_e)
    offs = (tile_end - tiles_e) * tile
    onehot = idx_t[None, :, :] == jnp.arange(n_exp, dtype=jnp.int32)[:, None, None]
    pos = rank_t + jnp.sum(jnp.where(onehot, offs[:, None, None], 0), axis=0)
    max_tiles = (TOP_K * rows) // tile + n_exp
    t = jnp.arange(max_tiles, dtype=jnp.int32)
    used = tile_end[-1]
    t_eff = jnp.minimum(t, used - 1)
    tile_expert = jnp.minimum(jnp.sum(t_eff[:, None] >= tile_end[None, :], axis=1), n_exp - 1)
    used_rows = used * tile
    pad = (jnp.append(offs + counts, used_rows).astype(jnp.int32),
           jnp.append(tiles_e * tile - counts, max_tiles * tile - used_rows).astype(jnp.int32))
    own = tile_expert[:, None] == jnp.arange(n_exp, dtype=jnp.int32)[None, :]
    last_row = jnp.sum(jnp.where(own, (offs + counts)[None, :], 0), axis=1)
    tile_rows = jnp.where(t < used, jnp.clip(last_row - t * tile, 0, tile), 0)
    first = jnp.logical_or(t == 0, tile_expert != jnp.roll(tile_expert, 1))
    slot = (jnp.cumsum(first.astype(jnp.int32)) - 1) % 2
    after = jnp.sum(jnp.where(own, tile_end[None, :], 0), axis=1)
    nxt = jnp.where(after < used, jnp.take(tile_expert, jnp.minimum(after, max_tiles - 1)), -1)
    plan = tuple(a.astype(jnp.int32) for a in (tile_expert, t_eff, tile_rows, first, nxt, slot))
    return pos.astype(jnp.int32), pad, plan, max_tiles


def _scatter_body(pad_start_ref, pad_len_ref, pos_ref, h_ref, xs_ref, zeros, sem, zsem):
    tm = h_ref.shape[0]
    n_exp = pad_start_ref.shape[0] - 1

    @pl.when(pl.program_id(0) == 0)
    def _():
        zeros[...] = jnp.zeros_like(zeros)
        half = zeros.shape[0]
        sizes = [half >> i for i in range(half.bit_length())]
        tail_start = pad_start_ref[n_exp]
        tail_halves = pad_len_ref[n_exp] // half

        def tail_copy(i):
            return pltpu.make_async_copy(zeros, xs_ref.at[pl.ds(tail_start + i * half, half)], zsem)

        lax.fori_loop(0, tail_halves, lambda i, c: (tail_copy(i).start(), c)[1], 0)
        lax.fori_loop(0, tail_halves, lambda i, c: (tail_copy(i).wait(), c)[1], 0)

        def pieces(e, wait):
            start = pad_start_ref[e]
            n = pad_len_ref[e]
            for size in sizes:
                has = (n & size) != 0
                copy = pltpu.make_async_copy(zeros.at[pl.ds(0, size)], xs_ref.at[pl.ds(start, size)], zsem)

                @pl.when(has)
                def _():
                    copy.wait() if wait else copy.start()

                start = start + jnp.where(has, size, 0)

        lax.fori_loop(0, n_exp, lambda e, c: (pieces(e, False), c)[1], 0)
        lax.fori_loop(0, n_exp, lambda e, c: (pieces(e, True), c)[1], 0)

    def issue(r, carry):
        for k in range(TOP_K):
            pltpu.make_async_copy(h_ref.at[r], xs_ref.at[pos_ref[k, r]], sem).start(
                priority=k % 2)
        return carry

    lax.fori_loop(0, tm, issue, 0)
    all_rows = xs_ref.at[pl.ds(0, TOP_K * tm)]
    pltpu.make_async_copy(all_rows, all_rows, sem).wait()


def _scatter(pad_start, pad_len, pos, hp, slots):
    rows = hp.shape[0]
    row_tile = hp.shape[1:]
    tm = SEQ_TILE
    assert EXPERT_TILE & (EXPERT_TILE - 1) == 0
    grid_spec = pltpu.PrefetchScalarGridSpec(
        num_scalar_prefetch=2,
        grid=(rows // tm,),
        in_specs=[
            pl.BlockSpec((TOP_K, tm), lambda i, ps, pn: (0, i), memory_space=pltpu.SMEM),
            pl.BlockSpec((tm,) + row_tile, lambda i, ps, pn: (i, 0, 0)),
        ],
        out_specs=pl.BlockSpec(memory_space=pl.ANY),
        scratch_shapes=[pltpu.VMEM((EXPERT_TILE // 2,) + row_tile, U32),
                        pltpu.SemaphoreType.DMA(()), pltpu.SemaphoreType.DMA(())],
    )
    return pl.pallas_call(
        _scatter_body,
        grid_spec=grid_spec,
        out_shape=jax.ShapeDtypeStruct((slots,) + row_tile, U32),
        compiler_params=_cparams("arbitrary"),
        name="moe_dispatch",
    )(pad_start, pad_len, pos, hp)


def _ffn_rows(xw, wgu_ref, wd_ref):
    x = jnp.concatenate([_unpack_lo(xw).astype(BF16), _unpack_hi(xw).astype(BF16)], axis=1)
    gu = jnp.dot(x, wgu_ref, preferred_element_type=F32)
    f = gu.shape[1] // 2
    hdn = (_silu(gu[:, :f]) * gu[:, f:]).astype(BF16)
    return jnp.dot(hdn, wd_ref, preferred_element_type=F32)


def _experts_body(te_ref, ts_ref, tr_ref, first_ref, next_ref, slot_ref,
                  xs_ref, wg_hbm, wu_hbm, wd_hbm, ys_ref,
                  wg_buf, wu_buf, wd_buf, wgu_bf, wd_bf, sems, *, layer):
    del ts_ref
    t = pl.program_id(0)
    n_real = tr_ref[t]
    tm = xs_ref.shape[0]
    half = tm // 2
    f = wg_buf.shape[-1]

    def weight_copies(e, s):
        return [pltpu.make_async_copy(src.at[layer, e], dst.at[s], sems.at[s])
                for src, dst in ((wg_hbm, wg_buf), (wu_hbm, wu_buf), (wd_hbm, wd_buf))]

    @pl.when(t == 0)
    def _():
        for c in weight_copies(te_ref[0], slot_ref[0]):
            c.start()

    @pl.when(first_ref[t] == 1)
    def _():
        s = slot_ref[t]
        for c in weight_copies(te_ref[t], s):
            c.wait()
        for slot in range(2):
            @pl.when(s == slot)
            def _(slot=slot):
                wgu_bf[:, :f] = wg_buf[slot].astype(BF16)
                wgu_bf[:, f:] = wu_buf[slot].astype(BF16)
                wd_bf[...] = wd_buf[slot].astype(BF16)

        @pl.when(next_ref[t] >= 0)
        def _():
            for c in weight_copies(next_ref[t], 1 - s):
                c.start()

    def ffn(rows):
        part = pl.ds(0, rows)
        y = _ffn_rows(_load_row_tiles(xs_ref.at[part]), wgu_bf[...], wd_bf[...])
        _store_row_tiles(ys_ref.at[part], _pack_bf16_pairs(y))

    @pl.when(n_real > half)
    def _():
        ffn(tm)

    @pl.when(jnp.logical_and(n_real > 0, n_real <= half))
    def _():
        ffn(half)
        ys_ref[pl.ds(half, half)] = jnp.zeros((half,) + ys_ref.shape[1:], ys_ref.dtype)

    @pl.when(n_real == 0)
    def _():
        ys_ref[...] = jnp.zeros_like(ys_ref)


def _experts(tile_plan, xs, w_gate, w_up, w_down, layer, max_tiles):
    slots = xs.shape[0]
    row_tile = xs.shape[1:]
    tm = EXPERT_TILE
    d, f = w_gate.shape[2], w_gate.shape[3]
    grid_spec = pltpu.PrefetchScalarGridSpec(
        num_scalar_prefetch=len(tile_plan),
        grid=(max_tiles,),
        in_specs=[
            pl.BlockSpec((tm,) + row_tile, lambda t, te, ts, *_: (ts[t], 0, 0)),
            pl.BlockSpec(memory_space=pl.ANY),
            pl.BlockSpec(memory_space=pl.ANY),
            pl.BlockSpec(memory_space=pl.ANY),
        ],
        out_specs=pl.BlockSpec((tm,) + row_tile, lambda t, *_: (t, 0, 0)),
        scratch_shapes=[pltpu.VMEM((2, d, f), F32), pltpu.VMEM((2, d, f), F32), pltpu.VMEM((2, f, d), F32),
                        pltpu.VMEM((d, 2 * f), BF16), pltpu.VMEM((f, d), BF16),
                        pltpu.SemaphoreType.DMA((2,))],
    )
    return pl.pallas_call(
        functools.partial(_experts_body, layer=layer),
        grid_spec=grid_spec,
        out_shape=jax.ShapeDtypeStruct((slots,) + row_tile, U32),
        compiler_params=_cparams("arbitrary"),
        name="moe_experts",
    )(*tile_plan, xs, w_gate, w_up, w_down)


def _combine_body(pos_ref, pos_next_ref, wt_ref, hp_ref, ys_ref, wgu_ref, wd_ref, x_ref, g_ref, gt_ref,
                  o_ref, gbuf_even, gbuf_odd, sems):
    tm = hp_ref.shape[0]
    i = pl.program_id(0)
    bufs = (gbuf_even, gbuf_odd)

    def row_copy(p_ref, slot, k, r):
        return pltpu.make_async_copy(ys_ref.at[p_ref[k, r]], bufs[slot].at[k, r], sems.at[slot])

    def wait_rows(slot):
        all_rows = ys_ref.at[pl.ds(0, TOP_K * tm)]
        pltpu.make_async_copy(all_rows, all_rows, sems.at[slot]).wait()

    @pl.when(i == 0)
    def _():
        def issue(r, carry):
            for k in range(TOP_K):
                row_copy(pos_ref, 0, k, r).start(priority=k % 2)
            return carry
        lax.fori_loop(0, tm, issue, 0)

    def step(cur):
        wait_rows(cur)
        for r in range(tm):
            for k in range(TOP_K):
                row_copy(pos_next_ref, 1 - cur, k, r).start(priority=k % 2)
        y = _ffn_rows(_load_row_tiles(hp_ref), wgu_ref[...], wd_ref[...])
        half = y.shape[1] // 2
        lo, hi = y[:, :half], y[:, half:]
        wt = wt_ref[...]
        for k in range(TOP_K):
            wk = wt[:, k:k + 1]
            gk = _load_row_tiles(bufs[cur].at[k])
            lo = lo + wk * _unpack_lo(gk)
            hi = hi + wk * _unpack_hi(gk)
        y = jnp.concatenate([lo, hi], axis=1)
        o_ref[...] = x_ref[...] + gt_ref[0] * _rms(y, g_ref[...])

        @pl.when(i == pl.num_programs(0) - 1)
        def _():
            wait_rows(1 - cur)

    for parity in range(2):
        pl.when(i % 2 == parity)(functools.partial(step, parity))


def _combine(pos, wts, hp, ys, w_sgu, w_sd, xa, g, mod3, gt_idx, rows, seq, n_batch, in_place):
    d = xa.shape[1]
    row_tile = hp.shape[1:]
    tm = SEQ_TILE
    tps = seq // tm
    grp = lambda i: _group_of_tile(i, tps, n_batch)
    last = rows // tm - 1
    return pl.pallas_call(
        _combine_body,
        grid=(rows // tm,),
        in_specs=[
            pl.BlockSpec((TOP_K, tm), lambda i: (0, i), memory_space=pltpu.SMEM),
            pl.BlockSpec((TOP_K, tm), lambda i: (0, jnp.minimum(i + 1, last)), memory_space=pltpu.SMEM),
            pl.BlockSpec((tm, TOP_K), lambda i: (i, 0)),
            pl.BlockSpec((tm,) + row_tile, lambda i: (i, 0, 0)),
            pl.BlockSpec(memory_space=pl.ANY),
            pl.BlockSpec(w_sgu.shape, lambda i: (0, 0), pipeline_mode=pl.Buffered(1)),
            pl.BlockSpec(w_sd.shape, lambda i: (0, 0), pipeline_mode=pl.Buffered(1)),
            pl.BlockSpec((tm, d), lambda i: (i, 0)),
            pl.BlockSpec((1, d), lambda i: (0, 0)),
            pl.BlockSpec((1, 1, d), lambda i: (grp(i), 0, gt_idx)),
        ],
        out_specs=pl.BlockSpec((tm, d), lambda i: (i, 0)),
        out_shape=jax.ShapeDtypeStruct(xa.shape if in_place else (rows, d), F32),
        scratch_shapes=[pltpu.VMEM((TOP_K, tm) + row_tile, U32), pltpu.VMEM((TOP_K, tm) + row_tile, U32),
                        pltpu.SemaphoreType.DMA((2,))],
        input_output_aliases={7: 0} if in_place else {},
        compiler_params=_cparams("arbitrary"),
        name="moe_combine",
    )(pos, pos, wts, hp, ys, w_sgu, w_sd, xa, g, mod3)


def kernel(x, c, ctx, c_ctx, w_ada, b_ada, norm_g, w_in, diff_lambda, g_subln, sg_ln_g, sg_ln_b, sg_w, sg_b, conv_w, w_branch, w_out, w_router, router_bias, w_exp_gate, w_exp_up, w_exp_down, w_sh_gate, w_sh_up, w_sh_down):
    n_batch, seq, d = x.shape
    ctx_len = ctx.shape[1]
    depth = w_ada.shape[0]
    mix_w = w_branch.shape[2]
    n_heads = mix_w // HEAD_W
    n_exp = w_router.shape[2]
    n_lat = n_batch * seq
    n_ctx = n_batch * ctx_len
    n_all = n_lat + n_ctx
    assert n_batch + 1 <= MOD_ROWS and seq % SEQ_TILE == 0 and ctx_len % SEQ_TILE == 0
    assert n_lat % seq == 0 and n_lat % ctx_len == 0 and mix_w // SG_GROUPS == LANES
    assert d // 2 == SUBLANES * LANES

    xa = (x.reshape(n_lat, d), ctx.reshape(n_ctx, d))
    if depth == 1:
        xa = _concat_rows(*xa)
    cond = jnp.zeros((MOD_ROWS, d), F32).at[:n_batch].set(c).at[n_batch].set(c_ctx)
    mod = _ada(cond, w_ada, b_ada)
    cos, sin = _rope_tables(seq)
    q_cb, k_cb, v_cb = 0, n_heads, 2 * n_heads
    gate_cb = (8 * mix_w) // d

    w_in_b = w_in.astype(BF16)
    w_branch_b = w_branch.astype(BF16)
    w_out_b = w_out.astype(BF16)
    n_in = w_in.shape[2]

    for l in range(depth):
        last = l == depth - 1
        lam_init = 0.8 - 0.6 * math.exp(-0.3 * l)
        mod3 = mod[l].reshape(MOD_ROWS, 1, N_MOD * d)
        ng = norm_g[l]
        rows = n_lat if last else n_all

        in_proj = functools.partial(_norm_matmul, xa, ng[0:1], mod3, 1, 0, w_in_b, l,
                                    seq=seq, n_batch=n_batch)
        if last:
            y = in_proj(n_lat, 0, n_in, 0, name="in_proj")
            yc = in_proj(n_ctx, n_lat, 2 * mix_w, mix_w, name="in_proj_ctx_kv")
            ctx_kv = (yc, 0, yc, n_heads, 0, ctx_len)
        else:
            y = in_proj(n_all, 0, n_in, 0, name="in_proj")
            ctx_kv = (y, k_cb, y, v_cb, n_lat // ctx_len, ctx_len)
        attn_o = _attention(y, q_cb, 0, y, k_cb, y, v_cb, 0, seq, seq, n_batch, n_heads,
                            diff_lambda[l], g_subln[l:l + 1], lam_init, ctx=ctx_kv, rope=(cos, sin),
                            name="attn_latent")
        if not last:
            attn_c = _attention(y, q_cb, n_lat // ctx_len, y, k_cb, y, v_cb, n_lat // ctx_len,
                                ctx_len, ctx_len, n_batch, n_heads,
                                diff_lambda[l], g_subln[l:l + 1], lam_init, name="attn_ctx")
            attn_o = jnp.concatenate([attn_o, attn_c], axis=0)
        sg_bias_map = jnp.repeat(sg_b[l].T, mix_w // SG_GROUPS, axis=1)
        xa = _mixer_tail(y, attn_o, xa, rows, n_lat, seq, n_batch, 3, gate_cb, l,
                         sg_ln_g[l:l + 1], sg_ln_b[l:l + 1], sg_w[l].astype(BF16), sg_bias_map, conv_w[l],
                         w_branch_b, w_out_b, ng[1:2], mod3, 2)

        hp, wt_t, idx_t, rank_t, cnt = _route(xa, ng[2:3], mod3, 4, 3, w_router[l].T, router_bias[l],
                                              rows, seq, n_batch)
        pos, pad, tile_plan, max_tiles = _dispatch_plan(
            idx_t, rank_t, cnt[:, 0].astype(jnp.int32), EXPERT_TILE)
        xs = _scatter(*pad, pos, hp, max_tiles * EXPERT_TILE)
        ys = _experts(tile_plan, xs, w_exp_gate, w_exp_up, w_exp_down, l, max_tiles)
        w_sgu = jnp.concatenate([w_sh_gate[l], w_sh_up[l]], axis=-1).astype(BF16)
        xa = _combine(pos, wt_t.T, hp, ys, w_sgu, w_sh_down[l].astype(BF16), xa, ng[3:4], mod3, 5,
                      rows, seq, n_batch, in_place=not last)

    return xa.reshape(n_batch, seq, d)
```

```python
import functools
import math

import jax
import jax.numpy as jnp
from jax import lax
from jax.experimental import pallas as pl
from jax.experimental.pallas import tpu as pltpu

GRID_W = 64
QK_DIM = 64
HEAD_W = 2 * QK_DIM
ROPE_BASE = 10000.0
SG_CHUNK = 128
SG_GROUPS = 8
TOP_K = 8
N_GROUPS = 8
TOPK_GROUPS = 4
ROUTED_SCALE = 2.5
NORM_EPS = 1e-6
N_MOD = 6

LANES = 128
SUBLANES = 8
VMEM_LIMIT = 56 * 1024 * 1024
SEQ_TILE = 256
EXPERT_TILE = 512
MOD_ROWS = 16
ATTN_SUB_ROWS = 128

F32 = jnp.float32
BF16 = jnp.bfloat16
U32 = jnp.uint32
HIGHEST = lax.Precision.HIGHEST


def _cparams(*sem):
    return pltpu.CompilerParams(dimension_semantics=sem, vmem_limit_bytes=VMEM_LIMIT)


def _pick_tile(n, cands):
    for c in cands:
        if n % c == 0:
            return c
    raise ValueError(f"no tile for {n} in {cands}")


def _sigmoid(x):
    return 1.0 / (1.0 + jnp.exp(-x))


def _silu(x):
    return x * _sigmoid(x)


def _gelu_tanh(x):
    c = math.sqrt(2.0 / math.pi)
    return x * (0.5 * (1.0 + jnp.tanh(c * (x + 0.044715 * (x * x * x)))))


def _rms(x, g):
    return x * lax.rsqrt(jnp.mean(x * x, axis=-1, keepdims=True) + NORM_EPS) * g


def _pack_bf16_pairs(x):
    w = x.shape[1] // 2
    bits = pltpu.bitcast(x.astype(BF16).astype(F32), U32)
    return (bits[:, :w] >> 16) | bits[:, w:]


def _unpack_lo(w):
    return pltpu.bitcast(w << 16, F32)


def _unpack_hi(w):
    return pltpu.bitcast(w & jnp.uint32(0xFFFF0000), F32)


def _store_row_tiles(ref, packed):
    m, s, _ = ref.shape
    flat = ref.reshape(m * s, LANES)
    for j in range(s):
        flat[pl.ds(j, m, stride=s), :] = packed[:, j * LANES:(j + 1) * LANES]


def _load_row_tiles(ref):
    m, s, _ = ref.shape
    flat = ref.reshape(m * s, LANES)
    return jnp.concatenate([flat[pl.ds(j, m, stride=s), :] for j in range(s)], axis=1)


def _concat_body(a_ref, b_ref, o_ref, *, a_tiles):
    i = pl.program_id(0)

    @pl.when(i < a_tiles)
    def _():
        o_ref[...] = a_ref[...]

    @pl.when(i >= a_tiles)
    def _():
        o_ref[...] = b_ref[...]


def _concat_rows(a, b):
    d = a.shape[1]
    tm = _pick_tile(math.gcd(a.shape[0], b.shape[0]), (512, 256))
    na, nb = a.shape[0] // tm, b.shape[0] // tm
    return pl.pallas_call(
        functools.partial(_concat_body, a_tiles=na),
        grid=(na + nb,),
        in_specs=[
            pl.BlockSpec((tm, d), lambda i: (jnp.minimum(i, na - 1), 0)),
            pl.BlockSpec((tm, d), lambda i: (jnp.maximum(i - na, 0), 0)),
        ],
        out_specs=pl.BlockSpec((tm, d), lambda i: (i, 0)),
        out_shape=jax.ShapeDtypeStruct((a.shape[0] + b.shape[0], d), a.dtype),
        compiler_params=_cparams("parallel"),
        name="concat_rows",
    )(a, b)


def _ada_body(c_ref, w_ref, b_ref, o_ref):
    a = _silu(c_ref[...])
    o_ref[0] = jnp.dot(a, w_ref[0], precision=HIGHEST, preferred_element_type=F32) + b_ref[0]


def _ada(cond, w_ada, b_ada):
    depth, d, n = w_ada.shape
    tn = _pick_tile(n, (1024, 512, 256, 128))
    return pl.pallas_call(
        _ada_body,
        grid=(depth, n // tn),
        in_specs=[
            pl.BlockSpec((MOD_ROWS, d), lambda l, j: (0, 0)),
            pl.BlockSpec((1, d, tn), lambda l, j: (l, 0, j)),
            pl.BlockSpec((1, 1, tn), lambda l, j: (l, 0, j)),
        ],
        out_specs=pl.BlockSpec((1, MOD_ROWS, tn), lambda l, j: (l, 0, j)),
        out_shape=jax.ShapeDtypeStruct((depth, MOD_ROWS, n), F32),
        compiler_params=_cparams("arbitrary", "arbitrary"),
        name="ada_mod",
    )(cond, w_ada, b_ada.reshape(depth, 1, n))


def _group_of_tile(i, tiles_per_seq, n_batch):
    return jnp.minimum(i // tiles_per_seq, n_batch)


def _stream_specs(stream, tm, tile_of):
    if not isinstance(stream, tuple):
        return [pl.BlockSpec((tm, stream.shape[1]), lambda *idx: (tile_of(*idx), 0))], None
    a, b = stream
    split = a.shape[0] // tm
    assert a.shape[0] % tm == 0 and b.shape[0] % tm == 0
    return [pl.BlockSpec((tm, a.shape[1]), lambda *idx: (jnp.minimum(tile_of(*idx), split - 1), 0)),
            pl.BlockSpec((tm, b.shape[1]), lambda *idx: (jnp.maximum(tile_of(*idx) - split, 0), 0),
                         pipeline_mode=pl.Buffered(1))], split


def _stream_tile(refs, tile, split):
    if split is None:
        return refs[0][...]
    return jnp.where(tile < split, refs[0][...], refs[1][...])


def _stream_args(stream):
    return list(stream) if isinstance(stream, tuple) else [stream]


def _norm_mm_body(*refs, split, tile0):
    n_src = 1 if split is None else 2
    g_ref, sc_ref, sh_ref, w_ref, o_ref, h_ref = refs[n_src:]

    @pl.when(pl.program_id(1) == 0)
    def _():
        y = _rms(_stream_tile(refs[:n_src], pl.program_id(0) + tile0, split), g_ref[...])
        h_ref[...] = (y * (1.0 + sc_ref[0]) + sh_ref[0]).astype(h_ref.dtype)

    o_ref[...] = jnp.dot(h_ref[...], w_ref[...], preferred_element_type=F32).astype(o_ref.dtype)


def _norm_matmul(stream, g, mod3, sc_idx, sh_idx, w, layer, rows, row_off, cols, col_off, seq, n_batch, name):
    d = w.shape[1]
    tm = _pick_tile(math.gcd(math.gcd(rows, row_off) if row_off else rows, seq), (1024, 512, 256))
    tn_cands = (1024, 512, 256) if isinstance(stream, tuple) else (2048, 1024, 512, 256)
    tn = _pick_tile(math.gcd(cols, col_off) if col_off else cols, tn_cands)
    ro, co = row_off // tm, col_off // tn
    tps = seq // tm
    grp = lambda i: _group_of_tile(i + ro, tps, n_batch)
    x_specs, split = _stream_specs(stream, tm, lambda i, j: i + ro)
    return pl.pallas_call(
        functools.partial(_norm_mm_body, split=split, tile0=ro),
        grid=(rows // tm, cols // tn),
        in_specs=x_specs + [
            pl.BlockSpec((1, d), lambda i, j: (0, 0)),
            pl.BlockSpec((1, 1, d), lambda i, j: (grp(i), 0, sc_idx)),
            pl.BlockSpec((1, 1, d), lambda i, j: (grp(i), 0, sh_idx)),
            pl.BlockSpec((None, d, tn), lambda i, j: (layer, 0, j + co)),
        ],
        out_specs=pl.BlockSpec((tm, tn), lambda i, j: (i, j)),
        out_shape=jax.ShapeDtypeStruct((rows, cols), BF16),
        scratch_shapes=[pltpu.VMEM((tm, d), BF16)],
        compiler_params=_cparams("arbitrary", "arbitrary"),
        name=name,
    )(*_stream_args(stream), g, mod3, mod3, w)


def _rope_apply(x, cos, sin):
    lane = lax.broadcasted_iota(jnp.int32, x.shape, 1)
    first_half = (lane % (QK_DIM // 2)) < (QK_DIM // 4)
    partner = jnp.where(first_half,
                        pltpu.roll(x, LANES - QK_DIM // 4, 1),
                        pltpu.roll(x, QK_DIM // 4, 1))
    return x * cos + partner * sin


def _rope_tables(seq):
    n_freq = QK_DIM // 4
    inv = ROPE_BASE ** (-jnp.arange(n_freq, dtype=F32) / n_freq)
    t = jnp.arange(seq)
    row = (t // GRID_W).astype(F32)
    col = (t % GRID_W).astype(F32)
    ang = jnp.stack([row[:, None] * inv, col[:, None] * inv], axis=1)
    cos = jnp.cos(ang)
    sin = jnp.sin(ang)
    cos64 = jnp.stack([cos, cos], axis=2).reshape(seq, QK_DIM)
    sin64 = jnp.stack([-sin, sin], axis=2).reshape(seq, QK_DIM)
    reps = LANES // QK_DIM
    return jnp.tile(cos64, (1, reps)), jnp.tile(sin64, (1, reps))


def _attn_body(*refs, two_seg, rope, lam_init):
    refs = list(refs)
    q_ref, k_ref, v_ref = refs[:3]
    del refs[:3]
    segs = []
    if two_seg:
        segs.append((refs[0], refs[1], False))
        del refs[:2]
    segs.append((k_ref, v_ref, rope))
    if rope:
        cos_ref, sin_ref = refs[:2]
        del refs[:2]
    dl_ref, g_ref, o_ref, kt_ref, v1_ref = refs
    tq = q_ref.shape[0]

    @pl.when(pl.program_id(2) == 0)
    def _():
        row = 0
        for ks_ref, vs_ref, rotate in segs:
            n = ks_ref.shape[0]
            ks = ks_ref[...].astype(F32)
            if rotate:
                ks = _rope_apply(ks, cos_ref[...], sin_ref[...])
            kt_ref[:, row:row + n] = ks.T.astype(BF16)
            v1_ref[row:row + n, :HEAD_W] = vs_ref[...]
            v1_ref[row:row + n, HEAD_W:] = jnp.ones(vs_ref.shape, BF16)
            row += n

    dl = dl_ref[...]
    lam = (jnp.exp(jnp.sum(dl[0:1] * dl[1:2], axis=-1, keepdims=True))
           - jnp.exp(jnp.sum(dl[2:3] * dl[3:4], axis=-1, keepdims=True)) + lam_init)
    q = q_ref[...]
    if rope:
        q_rows = pl.ds(pl.multiple_of(pl.program_id(2) * tq, tq), tq)
        q = _rope_apply(q.astype(F32), cos_ref[q_rows, :], sin_ref[q_rows, :]).astype(BF16)
    q = q * jnp.asarray(QK_DIM ** -0.5, BF16)
    lane = lax.broadcasted_iota(jnp.int32, q.shape, 1)
    zero = jnp.zeros_like(q)

    def softmax_v(qm):
        s = jnp.dot(qm, kt_ref[...], preferred_element_type=F32)
        m = jnp.max(s, axis=-1, keepdims=True)
        acc = jnp.dot(jnp.exp(s - m).astype(BF16), v1_ref[...], preferred_element_type=F32)
        return acc[:, :HEAD_W] / acc[:, HEAD_W:]

    q1 = jnp.where(lane < QK_DIM, q, zero)
    q2 = jnp.where(lane >= QK_DIM, q, zero)
    sub = min(q.shape[0], ATTN_SUB_ROWS)
    for r0 in range(0, q.shape[0], sub):
        rs = slice(r0, r0 + sub)
        o = softmax_v(q1[rs]) - lam * softmax_v(q2[rs])
        o_ref[rs, :] = (_rms(o, g_ref[...]) * (1.0 - lam_init)).astype(o_ref.dtype)


def _attention(q_arr, q_cb, q_rb, k_arr, k_cb, v_arr, v_cb, kv_rb, seq_q, seq_k, n_batch, n_heads,
               dl, g_sub, lam_init, ctx=None, rope=None, name="attn"):
    tq = _pick_tile(seq_q, (2048, 1024, 512, 256, 128))
    nq = seq_q // tq
    n_keys = seq_k + (ctx[5] if ctx is not None else 0)
    scratch = [pltpu.VMEM((HEAD_W, n_keys), BF16), pltpu.VMEM((n_keys, 2 * HEAD_W), BF16)]
    in_specs = [
        pl.BlockSpec((tq, HEAD_W), lambda b, h, i: ((q_rb + b) * nq + i, q_cb + h)),
        pl.BlockSpec((seq_k, HEAD_W), lambda b, h, i: (kv_rb + b, k_cb + h)),
        pl.BlockSpec((seq_k, HEAD_W), lambda b, h, i: (kv_rb + b, v_cb + h)),
    ]
    args = [q_arr, k_arr, v_arr]
    if ctx is not None:
        kc_arr, kc_cb, vc_arr, vc_cb, c_rb, seq_c = ctx
        in_specs += [
            pl.BlockSpec((seq_c, HEAD_W), lambda b, h, i: (c_rb + b, kc_cb + h)),
            pl.BlockSpec((seq_c, HEAD_W), lambda b, h, i: (c_rb + b, vc_cb + h)),
        ]
        args += [kc_arr, vc_arr]
    if rope is not None:
        assert seq_q == seq_k
        in_specs += [pl.BlockSpec((seq_k, LANES), lambda b, h, i: (0, 0))] * 2
        args += list(rope)
    in_specs += [
        pl.BlockSpec(dl.shape, lambda b, h, i: (0, 0)),
        pl.BlockSpec((1, HEAD_W), lambda b, h, i: (0, 0)),
    ]
    args += [dl, g_sub]
    return pl.pallas_call(
        functools.partial(_attn_body, two_seg=ctx is not None, rope=rope is not None, lam_init=lam_init),
        grid=(n_batch, n_heads, nq),
        in_specs=in_specs,
        out_specs=pl.BlockSpec((tq, HEAD_W), lambda b, h, i: (b * nq + i, h)),
        out_shape=jax.ShapeDtypeStruct((n_batch * seq_q, n_heads * HEAD_W), BF16),
        scratch_shapes=scratch,
        compiler_params=_cparams("arbitrary", "arbitrary", "arbitrary"),
        name=name,
    )(*args)


def _mixer_tail_body(u_ref, v_ref, x_ref, b_ref, c_ref, xp_ref, cp_ref, xn_ref, cn_ref, a_ref,
                     g0_ref, g1_ref, g2_ref, lng_ref, lnb_ref, sgw_ref, sgb_ref, cw_ref, wb_ref, wo_ref,
                     ng_ref, gt_ref, *refs, lat_tiles, tiles_per_seq, split):
    res_refs, o_ref, sg_ref = refs[:-2], refs[-2], refs[-1]
    i = pl.program_id(0)
    tm, w = u_ref.shape

    def gated(branch, gate_ref, g):
        return _sigmoid(gate_ref[...].astype(F32)) * jnp.dot(branch, wb_ref[g], preferred_element_type=F32)

    acc = gated(a_ref[...], g0_ref, 0)
    u = _gelu_tanh(u_ref[...].astype(F32))
    v = _gelu_tanh(v_ref[...].astype(F32))
    mu = jnp.mean(v, axis=-1, keepdims=True)
    dv = v - mu
    v = dv * lax.rsqrt(jnp.mean(dv * dv, axis=-1, keepdims=True) + NORM_EPS)
    v = (v * lng_ref[...] + lnb_ref[...]).astype(BF16)
    gw = w // SG_GROUPS
    for c in range(tm // SG_CHUNK):
        rs = slice(c * SG_CHUNK, (c + 1) * SG_CHUNK)
        for g in range(SG_GROUPS):
            cs = slice(g * gw, (g + 1) * gw)
            mixed = jnp.dot(sgw_ref[g], v[rs, cs], preferred_element_type=F32) + sgb_ref[:, cs]
            sg_ref[rs, cs] = (u[rs, cs] * mixed).astype(sg_ref.dtype)
    is_lat = i < lat_tiles
    seq_first = jnp.logical_or(jnp.logical_not(is_lat), i % tiles_per_seq == 0)
    seq_last = jnp.logical_or(jnp.logical_not(is_lat), i % tiles_per_seq == tiles_per_seq - 1)
    z = c_ref[...].astype(F32) * x_ref[...].astype(F32)
    hp = xp_ref.shape[0]
    z_prev = (cp_ref[...].astype(F32) * xp_ref[...].astype(F32))[hp - 1:hp, :]
    z_next = (cn_ref[...].astype(F32) * xn_ref[...].astype(F32))[0:1, :]
    z_prev = jnp.where(seq_first, 0.0, z_prev)
    z_next = jnp.where(seq_last, 0.0, z_next)
    row = lax.broadcasted_iota(jnp.int32, z.shape, 0)
    z_dn = jnp.where(row == 0, z_prev, pltpu.roll(z, 1, 0))
    z_up = jnp.where(row == tm - 1, z_next, pltpu.roll(z, tm - 1, 0))
    y = cw_ref[0:1, :] * z_dn + cw_ref[1:2, :] * z + cw_ref[2:3, :] * z_up
    cv = (b_ref[...].astype(F32) * y).astype(BF16)
    acc = acc + gated(sg_ref[...], g1_ref, 1) + gated(cv, g2_ref, 2)
    mix = jnp.dot(acc.astype(BF16), wo_ref[...], preferred_element_type=F32)
    o_ref[...] = _stream_tile(res_refs, i, split) + gt_ref[0] * _rms(mix, ng_ref[...])


def _mixer_tail(y, attn_o, stream, rows, lat_rows, seq, n_batch, col0, gate_cb, layer,
                sg_ln_g, sg_ln_b, sg_w, sg_bias_map, conv_w, w_branch, w_out, ng, mod3, gt_idx):
    n_br, mix_w, d = w_branch.shape[1:]
    tm = SEQ_TILE
    halo = 16
    hb = tm // halo
    last_hblk = rows // halo - 1
    cur = lambda c: pl.BlockSpec((tm, mix_w), lambda i: (i, col0 + c))
    prev = lambda c: pl.BlockSpec((halo, mix_w), lambda i: (jnp.maximum(i * hb - 1, 0), col0 + c))
    nxt = lambda c: pl.BlockSpec((halo, mix_w), lambda i: (jnp.minimum((i + 1) * hb, last_hblk), col0 + c))
    gate = lambda g: pl.BlockSpec((tm, d), lambda i: (i, gate_cb + g))
    full = lambda a: pl.BlockSpec(a.shape, lambda i: (0,) * a.ndim)
    consts = [sg_ln_g, sg_ln_b, sg_w, sg_bias_map, conv_w]
    grp = lambda i: _group_of_tile(i, seq // tm, n_batch)
    res_specs, split = _stream_specs(stream, tm, lambda i: i)
    in_specs = ([cur(0), cur(1), cur(2), cur(3), cur(4), prev(2), prev(4), nxt(2), nxt(4),
                 pl.BlockSpec((tm, mix_w), lambda i: (i, 0)), gate(0), gate(1), gate(2)]
                + [full(a) for a in consts]
                + [pl.BlockSpec((None, n_br, mix_w, d), lambda i: (layer, 0, 0, 0), pipeline_mode=pl.Buffered(1)),
                   pl.BlockSpec((None, d, d), lambda i: (layer, 0, 0), pipeline_mode=pl.Buffered(1)),
                   pl.BlockSpec((1, d), lambda i: (0, 0)),
                   pl.BlockSpec((1, 1, d), lambda i: (grp(i), 0, gt_idx))])
    if split is None:
        out_rows, alias = stream.shape[0], {len(in_specs): 0}
    else:
        out_rows, alias = rows, {}
        assert rows == stream[0].shape[0] + stream[1].shape[0]
    return pl.pallas_call(
        functools.partial(_mixer_tail_body, lat_tiles=lat_rows // tm, tiles_per_seq=seq // tm, split=split),
        grid=(rows // tm,),
        in_specs=in_specs + res_specs,
        out_specs=pl.BlockSpec((tm, d), lambda i: (i, 0)),
        out_shape=jax.ShapeDtypeStruct((out_rows, d), F32),
        scratch_shapes=[pltpu.VMEM((tm, mix_w), BF16)],
        input_output_aliases=alias,
        compiler_params=_cparams("parallel"),
        name="mixer_tail",
    )(y, y, y, y, y, y, y, y, y, attn_o, y, y, y, *consts, w_branch, w_out, ng, mod3, *_stream_args(stream))


def _top1_mask(vals, idx, n, axis):
    m = jnp.max(vals, axis=axis, keepdims=True)
    first = jnp.min(jnp.where(vals == m, idx, n), axis=axis, keepdims=True)
    return idx == first, m, first


def _route_body(x_ref, g_ref, sc_ref, sh_ref, wr_ref, rb_ref, hp_ref, wt_ref, ix_ref, rk_ref, cnt_ref):
    h2 = _rms(x_ref[...], g_ref[...]) * (1.0 + sc_ref[0]) + sh_ref[0]
    _store_row_tiles(hp_ref, _pack_bf16_pairs(h2))
    n_exp = wr_ref.shape[0]
    tm = x_ref.shape[0]
    per_group = n_exp // N_GROUPS
    logits = lax.dot_general(wr_ref[...], h2, (((1,), (1,)), ((), ())),
                             precision=HIGHEST, preferred_element_type=F32)
    scores = _sigmoid(logits)
    choice = scores + rb_ref[...]
    neg = -jnp.inf
    ji = lax.broadcasted_iota(jnp.int32, (per_group, tm), 0)
    groups, gs = [], []
    for g in range(N_GROUPS):
        cg = choice[g * per_group:(g + 1) * per_group, :]
        hit, m1, _ = _top1_mask(cg, ji, per_group, 0)
        m2 = jnp.max(jnp.where(hit, neg, cg), axis=0, keepdims=True)
        groups.append(cg)
        gs.append(m1 + m2)
    kept = []
    for g in range(N_GROUPS):
        beaten_by = jnp.zeros((1, tm), jnp.int32)
        for o in range(N_GROUPS):
            if o != g:
                beats = (gs[o] >= gs[g]) if o < g else (gs[o] > gs[g])
                beaten_by = beaten_by + beats.astype(jnp.int32)
        kept.append(jnp.where(beaten_by < TOPK_GROUPS, groups[g], neg))
    cm = jnp.concatenate(kept, axis=0)
    ei = lax.broadcasted_iota(jnp.int32, cm.shape, 0)
    ws, ids, hits = [], [], []
    sel = jnp.zeros(cm.shape, F32)
    for _ in range(TOP_K):
        hit, _, first = _top1_mask(cm, ei, n_exp, 0)
        ws.append(jnp.sum(jnp.where(hit, scores, 0.0), axis=0, keepdims=True))
        ids.append(first)
        hits.append(hit)
        sel = sel + jnp.where(hit, 1.0, 0.0)
        cm = jnp.where(hit, neg, cm)
    w = jnp.concatenate(ws, axis=0)
    wt_ref[...] = w / jnp.sum(w, axis=0, keepdims=True) * ROUTED_SCALE
    ix_ref[...] = jnp.concatenate(ids, axis=0)

    @pl.when(pl.program_id(0) == 0)
    def _():
        cnt_ref[...] = jnp.zeros_like(cnt_ref)

    before = (lax.broadcasted_iota(jnp.int32, (tm, tm), 0)
              < lax.broadcasted_iota(jnp.int32, (tm, tm), 1))
    within = jnp.dot(sel.astype(BF16), jnp.where(before, 1.0, 0.0).astype(BF16),
                     preferred_element_type=F32)
    rank = within + cnt_ref[:, 0:1]
    rk_ref[...] = jnp.concatenate(
        [jnp.sum(jnp.where(h, rank, 0.0), axis=0, keepdims=True) for h in hits], axis=0).astype(jnp.int32)
    cnt_ref[...] = cnt_ref[...] + jnp.sum(sel, axis=1, keepdims=True)


def _route(xa, g, mod3, sc_idx, sh_idx, w_router_t, router_bias, rows, seq, n_batch):
    d = xa.shape[1]
    n_exp = w_router_t.shape[0]
    tm = SEQ_TILE
    tps = seq // tm
    grp = lambda i: _group_of_tile(i, tps, n_batch)
    return pl.pallas_call(
        _route_body,
        grid=(rows // tm,),
        in_specs=[
            pl.BlockSpec((tm, d), lambda i: (i, 0)),
            pl.BlockSpec((1, d), lambda i: (0, 0)),
            pl.BlockSpec((1, 1, d), lambda i: (grp(i), 0, sc_idx)),
            pl.BlockSpec((1, 1, d), lambda i: (grp(i), 0, sh_idx)),
            pl.BlockSpec((n_exp, d), lambda i: (0, 0)),
            pl.BlockSpec((n_exp, 1), lambda i: (0, 0)),
        ],
        out_specs=[
            pl.BlockSpec((tm, SUBLANES, LANES), lambda i: (i, 0, 0)),
            pl.BlockSpec((TOP_K, tm), lambda i: (0, i)),
            pl.BlockSpec((TOP_K, tm), lambda i: (0, i)),
            pl.BlockSpec((TOP_K, tm), lambda i: (0, i)),
            pl.BlockSpec((n_exp, LANES), lambda i: (0, 0)),
        ],
        out_shape=[
            jax.ShapeDtypeStruct((rows, SUBLANES, LANES), U32),
            jax.ShapeDtypeStruct((TOP_K, rows), F32),
            jax.ShapeDtypeStruct((TOP_K, rows), jnp.int32),
            jax.ShapeDtypeStruct((TOP_K, rows), jnp.int32),
            jax.ShapeDtypeStruct((n_exp, LANES), F32),
        ],
        compiler_params=_cparams("arbitrary"),
        name="moe_norm_route",
    )(xa, g, mod3, mod3, w_router_t, router_bias.reshape(n_exp, 1))


def _dispatch_plan(idx_t, rank_t, counts, tile):
    n_exp = counts.shape[0]
    rows = idx_t.shape[1]
    tiles_e = (counts + tile - 1) // tile
    tile_end = jnp.cumsum(tiles_e)
    offs = (tile_end - tiles_e) * tile
    onehot = idx_t[None, :, :] == jnp.arange(n_exp, dtype=jnp.int32)[:, None, None]
    pos = rank_t + jnp.sum(jnp.where(onehot, offs[:, None, None], 0), axis=0)
    max_tiles = (TOP_K * rows) // tile + n_exp
    t = jnp.arange(max_tiles, dtype=jnp.int32)
    used = tile_end[-1]
    t_eff = jnp.minimum(t, used - 1)
    tile_expert = jnp.minimum(jnp.sum(t_eff[:, None] >= tile_end[None, :], axis=1), n_exp - 1)
    used_rows = used * tile
    pad = (jnp.append(offs + counts, used_rows).astype(jnp.int32),
           jnp.append(tiles_e * tile - counts, max_tiles * tile - used_rows).astype(jnp.int32))
    own = tile_expert[:, None] == jnp.arange(n_exp, dtype=jnp.int32)[None, :]
    last_row = jnp.sum(jnp.where(own, (offs + counts)[None, :], 0), axis=1)
    tile_rows = jnp.where(t < used, jnp.clip(last_row - t * tile, 0, tile), 0)
    first = jnp.logical_or(t == 0, tile_expert != jnp.roll(tile_expert, 1))
    slot = (jnp.cumsum(first.astype(jnp.int32)) - 1) % 2
    after = jnp.sum(jnp.where(own, tile_end[None, :], 0), axis=1)
    nxt = jnp.where(after < used, jnp.take(tile_expert, jnp.minimum(after, max_tiles - 1)), -1)
    plan = tuple(a.astype(jnp.int32) for a in (tile_expert, t_eff, tile_rows, first, nxt, slot))
    return pos.astype(jnp.int32), pad, plan, max_tiles


def _scatter_body(pad_start_ref, pad_len_ref, pos_ref, h_ref, xs_ref, zeros, sem, zsem):
    tm = h_ref.shape[0]
    n_exp = pad_start_ref.shape[0] - 1

    half = zeros.shape[0]
    sizes = [half >> i for i in range(half.bit_length())]

    def zero_fill(wait):
        tail_start = pad_start_ref[n_exp]
        tail_halves = pad_len_ref[n_exp] // half

        def tail(i, c):
            copy = pltpu.make_async_copy(zeros, xs_ref.at[pl.ds(tail_start + i * half, half)], zsem)
            copy.wait() if wait else copy.start()
            return c

        def pieces(e, c):
            start = pad_start_ref[e]
            n = pad_len_ref[e]
            for size in sizes:
                has = (n & size) != 0
                copy = pltpu.make_async_copy(zeros.at[pl.ds(0, size)], xs_ref.at[pl.ds(start, size)], zsem)

                @pl.when(has)
                def _():
                    copy.wait() if wait else copy.start()

                start = start + jnp.where(has, size, 0)
            return c

        lax.fori_loop(0, tail_halves, tail, 0)
        lax.fori_loop(0, n_exp, pieces, 0)

    @pl.when(pl.program_id(0) == 0)
    def _():
        zeros[...] = jnp.zeros_like(zeros)
        zero_fill(wait=False)

    def issue(r, carry):
        for k in range(TOP_K):
            pltpu.make_async_copy(h_ref.at[r], xs_ref.at[pos_ref[k, r]], sem).start(
                priority=k % 2)
        return carry

    lax.fori_loop(0, tm, issue, 0)
    all_rows = xs_ref.at[pl.ds(0, TOP_K * tm)]
    pltpu.make_async_copy(all_rows, all_rows, sem).wait()

    @pl.when(pl.program_id(0) == 0)
    def _():
        zero_fill(wait=True)


def _scatter(pad_start, pad_len, pos, hp, slots):
    rows = hp.shape[0]
    row_tile = hp.shape[1:]
    tm = SEQ_TILE
    assert EXPERT_TILE & (EXPERT_TILE - 1) == 0
    grid_spec = pltpu.PrefetchScalarGridSpec(
        num_scalar_prefetch=2,
        grid=(rows // tm,),
        in_specs=[
            pl.BlockSpec((TOP_K, tm), lambda i, ps, pn: (0, i), memory_space=pltpu.SMEM),
            pl.BlockSpec((tm,) + row_tile, lambda i, ps, pn: (i, 0, 0)),
        ],
        out_specs=pl.BlockSpec(memory_space=pl.ANY),
        scratch_shapes=[pltpu.VMEM((EXPERT_TILE // 2,) + row_tile, U32),
                        pltpu.SemaphoreType.DMA(()), pltpu.SemaphoreType.DMA(())],
    )
    return pl.pallas_call(
        _scatter_body,
        grid_spec=grid_spec,
        out_shape=jax.ShapeDtypeStruct((slots,) + row_tile, U32),
        compiler_params=_cparams("arbitrary"),
        name="moe_dispatch",
    )(pad_start, pad_len, pos, hp)


def _ffn_rows(xw, wgu_ref, wd_ref):
    x = jnp.concatenate([_unpack_lo(xw).astype(BF16), _unpack_hi(xw).astype(BF16)], axis=1)
    gu = jnp.dot(x, wgu_ref, preferred_element_type=F32)
    f = gu.shape[1] // 2
    hdn = (_silu(gu[:, :f]) * gu[:, f:]).astype(BF16)
    return jnp.dot(hdn, wd_ref, preferred_element_type=F32)


def _experts_body(te_ref, ts_ref, tr_ref, first_ref, next_ref, slot_ref,
                  xs_ref, wg_hbm, wu_hbm, wd_hbm, ys_ref,
                  wg_buf, wu_buf, wd_buf, wgu_bf, wd_bf, sems, *, layer):
    del ts_ref
    t = pl.program_id(0)
    n_real = tr_ref[t]
    tm = xs_ref.shape[0]
    half = tm // 2
    f = wg_buf.shape[-1]

    def weight_copies(e, s):
        return [pltpu.make_async_copy(src.at[layer, e], dst.at[s], sems.at[s])
                for src, dst in ((wg_hbm, wg_buf), (wu_hbm, wu_buf), (wd_hbm, wd_buf))]

    @pl.when(t == 0)
    def _():
        for c in weight_copies(te_ref[0], slot_ref[0]):
            c.start()

    @pl.when(first_ref[t] == 1)
    def _():
        s = slot_ref[t]
        for c in weight_copies(te_ref[t], s):
            c.wait()
        for slot in range(2):
            @pl.when(s == slot)
            def _(slot=slot):
                wgu_bf[:, :f] = wg_buf[slot].astype(BF16)
                wgu_bf[:, f:] = wu_buf[slot].astype(BF16)
                wd_bf[...] = wd_buf[slot].astype(BF16)

        @pl.when(next_ref[t] >= 0)
        def _():
            for c in weight_copies(next_ref[t], 1 - s):
                c.start()

    def ffn(rows):
        part = pl.ds(0, rows)
        y = _ffn_rows(_load_row_tiles(xs_ref.at[part]), wgu_bf[...], wd_bf[...])
        _store_row_tiles(ys_ref.at[part], _pack_bf16_pairs(y))

    @pl.when(n_real > half)
    def _():
        ffn(tm)

    @pl.when(jnp.logical_and(n_real > 0, n_real <= half))
    def _():
        ffn(half)
        ys_ref[pl.ds(half, half)] = jnp.zeros((half,) + ys_ref.shape[1:], ys_ref.dtype)

    @pl.when(n_real == 0)
    def _():
        ys_ref[...] = jnp.zeros_like(ys_ref)


def _experts(tile_plan, xs, w_gate, w_up, w_down, layer, max_tiles):
    slots = xs.shape[0]
    row_tile = xs.shape[1:]
    tm = EXPERT_TILE
    d, f = w_gate.shape[2], w_gate.shape[3]
    grid_spec = pltpu.PrefetchScalarGridSpec(
        num_scalar_prefetch=len(tile_plan),
        grid=(max_tiles,),
        in_specs=[
            pl.BlockSpec((tm,) + row_tile, lambda t, te, ts, *_: (ts[t], 0, 0)),
            pl.BlockSpec(memory_space=pl.ANY),
            pl.BlockSpec(memory_space=pl.ANY),
            pl.BlockSpec(memory_space=pl.ANY),
        ],
        out_specs=pl.BlockSpec((tm,) + row_tile, lambda t, *_: (t, 0, 0)),
        scratch_shapes=[pltpu.VMEM((2, d, f), F32), pltpu.VMEM((2, d, f), F32), pltpu.VMEM((2, f, d), F32),
                        pltpu.VMEM((d, 2 * f), BF16), pltpu.VMEM((f, d), BF16),
                        pltpu.SemaphoreType.DMA((2,))],
    )
    return pl.pallas_call(
        functools.partial(_experts_body, layer=layer),
        grid_spec=grid_spec,
        out_shape=jax.ShapeDtypeStruct((slots,) + row_tile, U32),
        compiler_params=_cparams("arbitrary"),
        name="moe_experts",
    )(*tile_plan, xs, w_gate, w_up, w_down)


def _combine_body(pos_ref, pos_next_ref, wt_ref, hp_ref, ys_ref, wgu_ref, wd_ref, x_ref, g_ref, gt_ref,
                  o_ref, gbuf_even, gbuf_odd, sems):
    tm = hp_ref.shape[0]
    i = pl.program_id(0)
    bufs = (gbuf_even, gbuf_odd)

    def row_copy(p_ref, slot, k, r):
        return pltpu.make_async_copy(ys_ref.at[p_ref[k, r]], bufs[slot].at[k, r], sems.at[slot])

    def wait_rows(slot):
        all_rows = ys_ref.at[pl.ds(0, TOP_K * tm)]
        pltpu.make_async_copy(all_rows, all_rows, sems.at[slot]).wait()

    @pl.when(i == 0)
    def _():
        def issue(r, carry):
            for k in range(TOP_K):
                row_copy(pos_ref, 0, k, r).start(priority=k % 2)
            return carry
        lax.fori_loop(0, tm, issue, 0)

    def step(cur):
        wait_rows(cur)
        for r in range(tm):
            for k in range(TOP_K):
                row_copy(pos_next_ref, 1 - cur, k, r).start(priority=k % 2)
        y = _ffn_rows(_load_row_tiles(hp_ref), wgu_ref[...], wd_ref[...])
        half = y.shape[1] // 2
        lo, hi = y[:, :half], y[:, half:]
        wt = wt_ref[...]
        for k in range(TOP_K):
            wk = wt[:, k:k + 1]
            gk = _load_row_tiles(bufs[cur].at[k])
            lo = lo + wk * _unpack_lo(gk)
            hi = hi + wk * _unpack_hi(gk)
        y = jnp.concatenate([lo, hi], axis=1)
        o_ref[...] = x_ref[...] + gt_ref[0] * _rms(y, g_ref[...])

        @pl.when(i == pl.num_programs(0) - 1)
        def _():
            wait_rows(1 - cur)

    for parity in range(2):
        pl.when(i % 2 == parity)(functools.partial(step, parity))


def _combine(pos, wts, hp, ys, w_sgu, w_sd, xa, g, mod3, gt_idx, rows, seq, n_batch, in_place):
    d = xa.shape[1]
    row_tile = hp.shape[1:]
    tm = SEQ_TILE
    tps = seq // tm
    grp = lambda i: _group_of_tile(i, tps, n_batch)
    last = rows // tm - 1
    return pl.pallas_call(
        _combine_body,
        grid=(rows // tm,),
        in_specs=[
            pl.BlockSpec((TOP_K, tm), lambda i: (0, i), memory_space=pltpu.SMEM),
            pl.BlockSpec((TOP_K, tm), lambda i: (0, jnp.minimum(i + 1, last)), memory_space=pltpu.SMEM),
            pl.BlockSpec((tm, TOP_K), lambda i: (i, 0)),
            pl.BlockSpec((tm,) + row_tile, lambda i: (i, 0, 0)),
            pl.BlockSpec(memory_space=pl.ANY),
            pl.BlockSpec(w_sgu.shape, lambda i: (0, 0), pipeline_mode=pl.Buffered(1)),
            pl.BlockSpec(w_sd.shape, lambda i: (0, 0), pipeline_mode=pl.Buffered(1)),
            pl.BlockSpec((tm, d), lambda i: (i, 0)),
            pl.BlockSpec((1, d), lambda i: (0, 0)),
            pl.BlockSpec((1, 1, d), lambda i: (grp(i), 0, gt_idx)),
        ],
        out_specs=pl.BlockSpec((tm, d), lambda i: (i, 0)),
        out_shape=jax.ShapeDtypeStruct(xa.shape if in_place else (rows, d), F32),
        scratch_shapes=[pltpu.VMEM((TOP_K, tm) + row_tile, U32), pltpu.VMEM((TOP_K, tm) + row_tile, U32),
                        pltpu.SemaphoreType.DMA((2,))],
        input_output_aliases={7: 0} if in_place else {},
        compiler_params=_cparams("arbitrary"),
        name="moe_combine",
    )(pos, pos, wts, hp, ys, w_sgu, w_sd, xa, g, mod3)


def kernel(x, c, ctx, c_ctx, w_ada, b_ada, norm_g, w_in, diff_lambda, g_subln, sg_ln_g, sg_ln_b, sg_w, sg_b, conv_w, w_branch, w_out, w_router, router_bias, w_exp_gate, w_exp_up, w_exp_down, w_sh_gate, w_sh_up, w_sh_down):
    n_batch, seq, d = x.shape
    ctx_len = ctx.shape[1]
    depth = w_ada.shape[0]
    mix_w = w_branch.shape[2]
    n_heads = mix_w // HEAD_W
    n_exp = w_router.shape[2]
    n_lat = n_batch * seq
    n_ctx = n_batch * ctx_len
    n_all = n_lat + n_ctx
    assert n_batch + 1 <= MOD_ROWS and seq % SEQ_TILE == 0 and ctx_len % SEQ_TILE == 0
    assert n_lat % seq == 0 and n_lat % ctx_len == 0 and mix_w // SG_GROUPS == LANES
    assert d // 2 == SUBLANES * LANES

    xa = (x.reshape(n_lat, d), ctx.reshape(n_ctx, d))
    if depth == 1:
        xa = _concat_rows(*xa)
    cond = jnp.zeros((MOD_ROWS, d), F32).at[:n_batch].set(c).at[n_batch].set(c_ctx)
    mod = _ada(cond, w_ada, b_ada)
    cos, sin = _rope_tables(seq)
    q_cb, k_cb, v_cb = 0, n_heads, 2 * n_heads
    gate_cb = (8 * mix_w) // d

    w_in_b = w_in.astype(BF16)
    w_branch_b = w_branch.astype(BF16)
    w_out_b = w_out.astype(BF16)
    n_in = w_in.shape[2]

    for l in range(depth):
        last = l == depth - 1
        lam_init = 0.8 - 0.6 * math.exp(-0.3 * l)
        mod3 = mod[l].reshape(MOD_ROWS, 1, N_MOD * d)
        ng = norm_g[l]
        rows = n_lat if last else n_all

        in_proj = functools.partial(_norm_matmul, xa, ng[0:1], mod3, 1, 0, w_in_b, l,
                                    seq=seq, n_batch=n_batch)
        if last:
            y = in_proj(n_lat, 0, n_in, 0, name="in_proj")
            yc = in_proj(n_ctx, n_lat, 2 * mix_w, mix_w, name="in_proj_ctx_kv")
            ctx_kv = (yc, 0, yc, n_heads, 0, ctx_len)
        else:
            y = in_proj(n_all, 0, n_in, 0, name="in_proj")
            ctx_kv = (y, k_cb, y, v_cb, n_lat // ctx_len, ctx_len)
        attn_o = _attention(y, q_cb, 0, y, k_cb, y, v_cb, 0, seq, seq, n_batch, n_heads,
                            diff_lambda[l], g_subln[l:l + 1], lam_init, ctx=ctx_kv, rope=(cos, sin),
                            name="attn_latent")
        if not last:
            attn_c = _attention(y, q_cb, n_lat // ctx_len, y, k_cb, y, v_cb, n_lat // ctx_len,
                                ctx_len, ctx_len, n_batch, n_heads,
                                diff_lambda[l], g_subln[l:l + 1], lam_init, name="attn_ctx")
            attn_o = jnp.concatenate([attn_o, attn_c], axis=0)
        sg_bias_map = jnp.repeat(sg_b[l].T, mix_w // SG_GROUPS, axis=1)
        xa = _mixer_tail(y, attn_o, xa, rows, n_lat, seq, n_batch, 3, gate_cb, l,
                         sg_ln_g[l:l + 1], sg_ln_b[l:l + 1], sg_w[l].astype(BF16), sg_bias_map, conv_w[l],
                         w_branch_b, w_out_b, ng[1:2], mod3, 2)

        hp, wt_t, idx_t, rank_t, cnt = _route(xa, ng[2:3], mod3, 4, 3, w_router[l].T, router_bias[l],
                                              rows, seq, n_batch)
        pos, pad, tile_plan, max_tiles = _dispatch_plan(
            idx_t, rank_t, cnt[:, 0].astype(jnp.int32), EXPERT_TILE)
        xs = _scatter(*pad, pos, hp, max_tiles * EXPERT_TILE)
        ys = _experts(tile_plan, xs, w_exp_gate, w_exp_up, w_exp_down, l, max_tiles)
        w_sgu = jnp.concatenate([w_sh_gate[l], w_sh_up[l]], axis=-1).astype(BF16)
        xa = _combine(pos, wt_t.T, hp, ys, w_sgu, w_sh_down[l].astype(BF16), xa, ng[3:4], mod3, 5,
                      rows, seq, n_batch, in_place=not last)

    return xa.reshape(n_batch, seq, d)
```
